```python
import jax
import jax.numpy as jnp
from jax import lax
import numpy as np

D_MODEL = 1024
BATCH = 8
SEQ = 4096
DEPTH = 2

GRID_W = 64
CTX_LEN = 256
N_EVEN = (DEPTH + 1) // 2
N_ODD = DEPTH // 2
NORM_EPS = 1e-6
N_MOD = 6

A_HEADS = 8
A_KV_HEADS = 2
A_GROUP = A_HEADS // A_KV_HEADS
A_HEAD_DIM = 64
A_Q = A_HEADS * A_HEAD_DIM
A_KV = A_KV_HEADS * A_HEAD_DIM
Q_BLOCK = 128
ROPE_THETA = 10000.0

B_HEADS = 4
B_DK = 64
B_DV = 128
B_K = B_HEADS * B_DK
B_V = B_HEADS * B_DV
B_GATE_RANK = 16
B_GATE_TAU = 16.0
B_CHUNK = 64

EVEN_SIZES = (A_Q, A_KV, A_KV, B_K, B_K, B_V, B_V, B_GATE_RANK, B_GATE_RANK)
EVEN_IN = sum(EVEN_SIZES)
EVEN_MIX = A_Q + B_V

LRU_WIDTH = 1280
LRU_HEADS = 10
LRU_HEAD_DIM = LRU_WIDTH // LRU_HEADS
LRU_CONV = 4
LRU_C = 8.0

D_FF = 2816
FFN_CONV = 3

kernel_name = 'hybrid_gqa_gla_rglru_convffn_prefix_dit'


def rmsnorm(x, g):
    xf = x.astype(jnp.float32)
    y = xf * lax.rsqrt(jnp.mean(xf * xf, axis=-1, keepdims=True) + NORM_EPS)
    return (y * g.astype(jnp.float32)).astype(x.dtype)


def adaln_params(cond, w, b):
    m = jax.nn.silu(cond) @ w + b
    return m.reshape(m.shape[:-1] + (N_MOD, D_MODEL))


def depthwise_conv(x, w):
    k = w.shape[0]
    left = (k - 1) // 2
    return lax.conv_general_dilated(x, w[:, None, :].astype(x.dtype), window_strides=(1,),
                                    padding=[(left, k - 1 - left)],
                                    dimension_numbers=('NWC', 'WIO', 'NWC'),
                                    feature_group_count=x.shape[-1])


def axial_rope_tables(n_tokens):
    rows = n_tokens // GRID_W
    row = jnp.repeat(jnp.arange(rows, dtype=jnp.float32), GRID_W)
    col = jnp.tile(jnp.arange(GRID_W, dtype=jnp.float32), rows)
    n_freq = A_HEAD_DIM // 4
    inv_freq = ROPE_THETA ** (-jnp.arange(n_freq, dtype=jnp.float32) / n_freq)
    ang = jnp.concatenate([row[:, None] * inv_freq, col[:, None] * inv_freq], axis=-1)
    return jnp.cos(ang), jnp.sin(ang)


def apply_axial_rope(x, cos, sin):
    n = A_HEAD_DIM // 4
    xf = x.astype(jnp.float32)
    r1, r2, c1, c2 = jnp.split(xf, 4, axis=-1)
    cr, cc = cos[:, :n], cos[:, n:]
    sr, sc = sin[:, :n], sin[:, n:]
    out = jnp.concatenate([r1 * cr - r2 * sr, r2 * cr + r1 * sr,
                           c1 * cc - c2 * sc, c2 * cc + c1 * sc], axis=-1)
    return out.astype(x.dtype)


def attn_heads(q, k, v, q_gain, k_gain):
    b, l, _ = q.shape
    q = rmsnorm(q.reshape(b, l, A_KV_HEADS, A_GROUP, A_HEAD_DIM), q_gain).transpose(0, 2, 3, 1, 4)
    k = rmsnorm(k.reshape(b, l, A_KV_HEADS, A_HEAD_DIM), k_gain).transpose(0, 2, 1, 3)
    v = v.reshape(b, l, A_KV_HEADS, A_HEAD_DIM).transpose(0, 2, 1, 3)
    return q, k, v


def attend(q, k, v):
    s = jnp.einsum('bkgqd,bksd->bkgqs', q, k).astype(jnp.float32)
    p = jax.nn.softmax(s, axis=-1).astype(v.dtype)
    return jnp.einsum('bkgqs,bksd->bkgqd', p, v)


def gla_chunk_scan(q, k, v, log_a, s0):
    b, h, l, dk = q.shape
    dv = v.shape[-1]
    n = l // B_CHUNK

    def chunks(t):
        return t.reshape(b, h, n, B_CHUNK, t.shape[-1]).transpose(2, 0, 1, 3, 4)

    lower_tri = jnp.tril(jnp.ones((B_CHUNK, B_CHUNK), dtype=bool))[:, :, None]

    def step(state, inp):
        qc, kc, vc, ac = inp
        cum = jnp.cumsum(ac, axis=2)
        o_inter = jnp.einsum('bhtd,bhdv->bhtv', qc * jnp.exp(cum), state)
        diff = jnp.where(lower_tri, cum[:, :, :, None, :] - cum[:, :, None, :, :], -jnp.inf)
        scores = jnp.einsum('bhtd,bhsd,bhtsd->bhts', qc, kc, jnp.exp(diff))
        o_intra = jnp.einsum('bhts,bhsv->bhtv', scores, vc)
        last = cum[:, :, -1:, :]
        state = jnp.exp(last[:, :, 0, :, None]) * state + jnp.einsum('bhsd,bhsv->bhdv', kc * jnp.exp(last - cum), vc)
        return state, o_inter + o_intra

    state, o = lax.scan(step, s0, (chunks(q), chunks(k), chunks(v), chunks(log_a)))
    return o.transpose(1, 2, 0, 3, 4).reshape(b, h, l, dv), state


def flip_seq(t):
    return jnp.flip(t, axis=2)


def gla_bidir(q, k, v, la_f, la_b, s0_f, s0_b):
    o_f, s_f = gla_chunk_scan(q, k, v, la_f, s0_f)
    o_b, s_b = gla_chunk_scan(flip_seq(q), flip_seq(k), flip_seq(v), flip_seq(la_b), s0_b)
    return o_f + flip_seq(o_b), s_f, s_b


def gla_prep(qb, kb, vb, lrf, lrb, w_up, b_gate):
    b, l, _ = qb.shape

    def heads(t, d):
        return t.reshape(b, l, B_HEADS, d).transpose(0, 2, 1, 3).astype(jnp.float32)

    def log_gate(lr, d):
        z = (lr @ w_up[d] + b_gate[d]).astype(jnp.float32)
        return heads(jax.nn.log_sigmoid(z) / B_GATE_TAU, B_DK)

    return (heads(qb, B_DK) * B_DK ** -0.5, heads(kb, B_DK), heads(vb, B_DV),
            log_gate(lrf, 0), log_gate(lrb, 1))


def gla_out(o, gate, o_gain):
    b, h, l, dv = o.shape
    o = rmsnorm(o.transpose(0, 2, 1, 3), o_gain)
    g = gate.reshape(b, l, h, dv).astype(jnp.float32)
    return (o * jax.nn.silu(g)).reshape(b, l, h * dv)


def even_mixer(h_lat, h_ctx, rope, w_in, w_out, q_gain, k_gain, gate_w_up, gate_b, o_gain, need_ctx_out):
    split_at = np.cumsum(EVEN_SIZES)[:-1].tolist()
    qa, ka, va, qb, kb, vb, gb, lrf, lrb = jnp.split(h_lat @ w_in, split_at, axis=-1)
    cqa, cka, cva, cqb, ckb, cvb, cgb, clrf, clrb = jnp.split(h_ctx @ w_in, split_at, axis=-1)
    cos, sin = rope
    dt = h_lat.dtype
    q, k, v = attn_heads(qa, ka, va, q_gain, k_gain)
    q = apply_axial_rope(q, cos, sin) * A_HEAD_DIM ** -0.5
    k = apply_axial_rope(k, cos, sin)
    cq, ck, cv = attn_heads(cqa, cka, cva, q_gain, k_gain)
    k_all = jnp.concatenate([ck, k], axis=2)
    v_all = jnp.concatenate([cv, v], axis=2)
    b, hk, g, l, hd = q.shape
    nb = l // Q_BLOCK
    q_blocks = q.reshape(b, hk, g, nb, Q_BLOCK, hd).transpose(3, 0, 1, 2, 4, 5)
    a_lat = lax.map(lambda qi: attend(qi, k_all, v_all), q_blocks)
    a_lat = a_lat.transpose(1, 0, 4, 2, 3, 5).reshape(b, l, A_Q)
    q2, k2, v2, laf, lab = gla_prep(qb, kb, vb, lrf, lrb, gate_w_up, gate_b)
    cq2, ck2, cv2, claf, clab = gla_prep(cqb, ckb, cvb, clrf, clrb, gate_w_up, gate_b)
    zeros = jnp.zeros((b, B_HEADS, B_DK, B_DV), jnp.float32)
    co2, cs_f, cs_b = gla_bidir(cq2, ck2, cv2, claf, clab, zeros, zeros)
    o2, _, _ = gla_bidir(q2, k2, v2, laf, lab, cs_f, cs_b)
    y_lat = jnp.concatenate([a_lat.astype(dt), gla_out(o2, gb, o_gain).astype(dt)], axis=-1) @ w_out
    if not need_ctx_out:
        return y_lat, None
    a_ctx = attend(cq * A_HEAD_DIM ** -0.5, ck, cv).transpose(0, 3, 1, 2, 4).reshape(b, -1, A_Q)
    y_ctx = jnp.concatenate([a_ctx.astype(dt), gla_out(co2, cgb, o_gain).astype(dt)], axis=-1) @ w_out
    return y_lat, y_ctx


def rglru_inputs(u, lam, w_a, b_a, w_x, b_x):
    b, l, w = u.shape
    uf = u.astype(jnp.float32)
    ub = uf.reshape(b, l, LRU_HEADS, LRU_HEAD_DIM)
    r = jax.nn.sigmoid(jnp.einsum('blhi,hij->blhj', ub, w_a.astype(jnp.float32)).reshape(b, l, w) + b_a.astype(jnp.float32))
    i = jax.nn.sigmoid(jnp.einsum('blhi,hij->blhj', ub, w_x.astype(jnp.float32)).reshape(b, l, w) + b_x.astype(jnp.float32))
    log_a = -LRU_C * r * jax.nn.softplus(-lam.astype(jnp.float32))
    a = jnp.exp(log_a)
    mult = jnp.sqrt(-jnp.expm1(2.0 * log_a))
    return a, mult * i * uf


def linear_scan(a, x, h0, reverse):
    def combine(e1, e2):
        a1, x1 = e1
        a2, x2 = e2
        return a1 * a2, a2 * x1 + x2
    a_cum, x_cum = lax.associative_scan(combine, (a, x), reverse=reverse, axis=1)
    return a_cum * h0[:, None, :] + x_cum


def odd_mixer(h_lat, h_ctx, w_in, conv_w, lam, w_a, b_a, w_x, b_x, w_out, need_ctx_out):
    w_gate, w_rec = w_in[:, :LRU_WIDTH], w_in[:, LRU_WIDTH:]

    def dir_inputs(u, d):
        return rglru_inputs(u, lam[d], w_a[d], b_a[d], w_x[d], b_x[d])

    u_c = depthwise_conv(h_ctx @ w_rec, conv_w)
    u_l = depthwise_conv(h_lat @ w_rec, conv_w)
    h0 = jnp.zeros((h_ctx.shape[0], LRU_WIDTH), jnp.float32)
    hf_c = linear_scan(*dir_inputs(u_c, 0), h0, reverse=False)
    hb_c = linear_scan(*dir_inputs(u_c, 1), h0, reverse=True)
    hf_l = linear_scan(*dir_inputs(u_l, 0), hf_c[:, -1], reverse=False)
    hb_l = linear_scan(*dir_inputs(u_l, 1), hb_c[:, 0], reverse=True)

    def out(h, rec):
        return (jax.nn.gelu((h @ w_gate).astype(jnp.float32)) * rec).astype(h.dtype) @ w_out

    y_lat = out(h_lat, hf_l + hb_l)
    if not need_ctx_out:
        return y_lat, None
    return y_lat, out(h_ctx, hf_c + hb_c)


def conv_ffn(h, w_up, conv_w, w_down):
    u = depthwise_conv(h @ w_up, conv_w)
    gate, val = jnp.split(u, 2, axis=-1)
    return (jax.nn.silu(gate) * val) @ w_down


def setup_inputs(seed: int = 0) -> dict:
    key = jax.random.key(seed)
    ks = list(jax.random.split(key, 32))

    def nrm(i, shape, scale):
        return jax.random.normal(ks[i], shape, jnp.float32) * scale

    D = D_MODEL
    u = jax.random.uniform(ks[20], (N_ODD, 2, LRU_WIDTH), jnp.float32, minval=0.9, maxval=0.999)
    p = u ** (1.0 / LRU_C)
    lam = jnp.log(p) - jnp.log1p(-p)
    return {
        'x': nrm(0, (BATCH, SEQ, D), 1.0),
        'c': nrm(1, (BATCH, D), 1.0),
        'ctx': nrm(2, (BATCH, CTX_LEN, D), 1.0),
        'c_ctx': nrm(3, (D,), 1.0),
        'ada_w': nrm(4, (DEPTH, D, N_MOD * D), 0.5 * D ** -0.5),
        'ada_b': nrm(5, (DEPTH, N_MOD * D), 0.01),
        'norm_mix': 1.0 + nrm(6, (DEPTH, D), 0.02),
        'norm_ffn': 1.0 + nrm(7, (DEPTH, D), 0.02),
        'ffn_w_up': nrm(8, (DEPTH, D, 2 * D_FF), D ** -0.5),
        'ffn_conv': nrm(9, (DEPTH, FFN_CONV, 2 * D_FF), FFN_CONV ** -0.5),
        'ffn_w_down': nrm(10, (DEPTH, D_FF, D), D_FF ** -0.5),
        'even_w_in': nrm(11, (N_EVEN, D, EVEN_IN), D ** -0.5),
        'even_w_out': nrm(12, (N_EVEN, EVEN_MIX, D), EVEN_MIX ** -0.5),
        'attn_q_gain': 1.0 + nrm(13, (N_EVEN, A_HEAD_DIM), 0.02),
        'attn_k_gain': 1.0 + nrm(14, (N_EVEN, A_HEAD_DIM), 0.02),
        'gla_gate_w_up': nrm(15, (N_EVEN, 2, B_GATE_RANK, B_K), B_GATE_RANK ** -0.5),
        'gla_gate_b': nrm(16, (N_EVEN, 2, B_K), 0.5),
        'gla_out_gain': 1.0 + nrm(17, (N_EVEN, B_DV), 0.02),
        'lru_w_in': nrm(18, (N_ODD, D, 2 * LRU_WIDTH), D ** -0.5),
        'lru_conv': nrm(19, (N_ODD, LRU_CONV, LRU_WIDTH), LRU_CONV ** -0.5),
        'lru_lambda': lam,
        'lru_w_a': nrm(21, (N_ODD, 2, LRU_HEADS, LRU_HEAD_DIM, LRU_HEAD_DIM), LRU_HEAD_DIM ** -0.5),
        'lru_b_a': nrm(22, (N_ODD, 2, LRU_WIDTH), 0.1),
        'lru_w_x': nrm(23, (N_ODD, 2, LRU_HEADS, LRU_HEAD_DIM, LRU_HEAD_DIM), LRU_HEAD_DIM ** -0.5),
        'lru_b_x': nrm(24, (N_ODD, 2, LRU_WIDTH), 0.1),
        'lru_w_out': nrm(25, (N_ODD, LRU_WIDTH, D), LRU_WIDTH ** -0.5),
        'final_gain': 1.0 + nrm(26, (D,), 0.02),
    }


def reference(x, c, ctx, c_ctx, ada_w, ada_b, norm_mix, norm_ffn, ffn_w_up, ffn_conv, ffn_w_down,
              even_w_in, even_w_out, attn_q_gain, attn_k_gain, gla_gate_w_up, gla_gate_b, gla_out_gain,
              lru_w_in, lru_conv, lru_lambda, lru_w_a, lru_b_a, lru_w_x, lru_b_x, lru_w_out, final_gain):
    rope = axial_rope_tables(x.shape[1])
    for l in range(DEPTH):
        last = l == DEPTH - 1
        j = l // 2
        m = adaln_params(c, ada_w[l], ada_b[l])[:, None]
        mc = adaln_params(c_ctx, ada_w[l], ada_b[l])
        h_lat = rmsnorm(x, norm_mix[l]) * (1.0 + m[:, :, 1]) + m[:, :, 0]
        h_ctx = rmsnorm(ctx, norm_mix[l]) * (1.0 + mc[1]) + mc[0]
        if l % 2 == 0:
            y_lat, y_ctx = even_mixer(h_lat, h_ctx, rope, even_w_in[j], even_w_out[j], attn_q_gain[j],
                                      attn_k_gain[j], gla_gate_w_up[j], gla_gate_b[j], gla_out_gain[j],
                                      not last)
        else:
            y_lat, y_ctx = odd_mixer(h_lat, h_ctx, lru_w_in[j], lru_conv[j], lru_lambda[j], lru_w_a[j],
                                     lru_b_a[j], lru_w_x[j], lru_b_x[j], lru_w_out[j], not last)
        x = x + m[:, :, 2] * y_lat
        h = rmsnorm(x, norm_ffn[l]) * (1.0 + m[:, :, 4]) + m[:, :, 3]
        x = x + m[:, :, 5] * conv_ffn(h, ffn_w_up[l], ffn_conv[l], ffn_w_down[l])
        if not last:
            ctx = ctx + mc[2] * y_ctx
            hc = rmsnorm(ctx, norm_ffn[l]) * (1.0 + mc[4]) + mc[3]
            ctx = ctx + mc[5] * conv_ffn(hc, ffn_w_up[l], ffn_conv[l], ffn_w_down[l])
    return rmsnorm(x, final_gain)
```

```python
import functools

import numpy as np
import jax
import jax.numpy as jnp
from jax import lax
from jax.experimental import pallas as pl
from jax.experimental.pallas import tpu as pltpu

F32 = jnp.float32
BF16 = jnp.bfloat16

NORM_EPS = 1e-6
N_MOD = 6
GRID_W = 64
ROPE_THETA = 10000.0

A_HEADS = 8
A_KV_HEADS = 2
A_GROUP = A_HEADS // A_KV_HEADS
A_HEAD_DIM = 64

B_HEADS = 4
B_DK = 64
B_DV = 128
B_K = B_HEADS * B_DK
B_V = B_HEADS * B_DV
B_GATE_RANK = 16
B_GATE_TAU = 16.0
GLA_CHUNK = 64
GLA_SUB = 16

LRU_HEADS = 10
LRU_HEAD_DIM = 128
LRU_WIDTH = LRU_HEADS * LRU_HEAD_DIM
LRU_C = 8.0

LANES = 128
ROW_TILE = 256
ATT_TQ = 128
ATT_TK = 256
FFN_CN = 256
FFN_HALO = 8
LRU_TT = 32
VMEM_LIMIT = 56 * 1024 * 1024

NT_DIMS = (((1,), (1,)), ((), ()))
TN_DIMS = (((0,), (0,)), ((), ()))


def _params(*sem):
    return pltpu.CompilerParams(dimension_semantics=sem, vmem_limit_bytes=VMEM_LIMIT)


def _resident(shape):
    nd = len(shape)
    return pl.BlockSpec(shape, lambda *_: (0,) * nd, pipeline_mode=pl.Buffered(1))


def _dot(a, b):
    return jnp.dot(a, b, preferred_element_type=F32)


def _split_bf16(a):
    hi = a.astype(BF16)
    lo = (a - hi.astype(F32)).astype(BF16)
    return hi, lo


def _dot_f32(a, b):
    ah, al = _split_bf16(a)
    bh, bl = _split_bf16(b)
    return _dot(ah, bh) + _dot(ah, bl) + _dot(al, bh)


def _sigmoid(x):
    return 1.0 / (1.0 + jnp.exp(-x))


def _silu(x):
    return x * _sigmoid(x)


def _softplus(x):
    return jnp.maximum(x, 0.0) + jnp.log1p(jnp.exp(-jnp.abs(x)))


def _norm_mod(x, gain, shift, scale):
    ms = jnp.mean(x * x, axis=-1, keepdims=True)
    return (x * lax.rsqrt(ms + NORM_EPS) * gain) * (1.0 + scale) + shift


def _adaln_kernel(c_ref, w_ref, b_ref, o_ref):
    o_ref[...] = _dot_f32(_silu(c_ref[...]), w_ref[...]) + b_ref[...]


def adaln_table(cond, w, b):
    rows, d = cond.shape
    n = w.shape[1]
    tn = 768
    return pl.pallas_call(
        _adaln_kernel,
        out_shape=jax.ShapeDtypeStruct((rows, n), F32),
        grid=(n // tn,),
        in_specs=[pl.BlockSpec((rows, d), lambda j: (0, 0)),
                  pl.BlockSpec((d, tn), lambda j: (0, j)),
                  pl.BlockSpec((1, tn), lambda j: (0, j))],
        out_specs=pl.BlockSpec((rows, tn), lambda j: (0, j)),
        compiler_params=_params("arbitrary"),
        name="adaln_table",
    )(cond, w, b.reshape(1, n))


def _mods_spec(n_ctx_tiles, d):
    return pl.BlockSpec((None, None, N_MOD, d),
                        lambda b, i: (b, (i >= n_ctx_tiles).astype(jnp.int32), 0, 0))


def _even_in_kernel(x_ref, mod_ref, gain_ref, wq_ref, wkv_ref, wb_ref, wlr_ref, wgate_ref, bgate_ref,
                    qg_ref, kg_ref, cos_ref, sin_ref,
                    q_ref, k_ref, v_ref, qb_ref, kb_ref, vb_ref, gb_ref, laf_ref, lab_ref):
    h = _norm_mod(x_ref[...], gain_ref[...], mod_ref[0:1, :], mod_ref[1:2, :]).astype(BF16)
    cos = cos_ref[...]
    sin = sin_ref[...]
    lane = lax.broadcasted_iota(jnp.int32, cos.shape, 1)
    first_half = (lane % 32) < 16

    def head_norm_rope(xh, g):
        ms = jnp.sum(xh * xh, axis=-1, keepdims=True) * (1.0 / A_HEAD_DIM)
        y = xh * lax.rsqrt(ms + NORM_EPS) * g
        swapped = jnp.where(first_half, pltpu.roll(y, LANES - 16, 1), pltpu.roll(y, 16, 1))
        return y * cos + swapped * sin

    qp = _dot(h, wq_ref[...])
    for hh in range(A_HEADS):
        qh = head_norm_rope(qp[:, hh * LANES:(hh + 1) * LANES], qg_ref[...])
        q_ref[hh] = (qh * A_HEAD_DIM ** -0.5).astype(BF16)
    kvp = _dot(h, wkv_ref[...])
    for hh in range(A_KV_HEADS):
        k_ref[hh] = head_norm_rope(kvp[:, hh * LANES:(hh + 1) * LANES], kg_ref[...]).astype(BF16)
        vh = kvp[:, (A_KV_HEADS + hh) * LANES:(A_KV_HEADS + hh + 1) * LANES]
        v_ref[hh] = jnp.where(lane >= A_HEAD_DIM, 1.0, vh).astype(BF16)

    pb = _dot(h, wb_ref[...])
    qb_ref[...] = pb[:, 0:B_K] * B_DK ** -0.5
    kb_ref[...] = pb[:, B_K:2 * B_K]
    vb_ref[...] = pb[:, 2 * B_K:2 * B_K + B_V]
    gb_ref[...] = pb[:, 2 * B_K + B_V:2 * B_K + 2 * B_V]

    lr = _dot(h, wlr_ref[...])
    z = _dot_f32(lr, wgate_ref[...]) + bgate_ref[...]
    la = -_softplus(-z) * (1.0 / B_GATE_TAU)
    laf_ref[...] = la[:, 0:B_K]
    lab_ref[...] = la[:, B_K:2 * B_K]


def even_in_proj(xx, mods, gain, wq, wkv, wb, wlr, wgate, bgate, qg, kg, cos, sin, n_ctx_tiles):
    bsz, s, d = xx.shape
    t = ROW_TILE
    row = lambda w: pl.BlockSpec((None, t, w), lambda b, i: (b, i, 0))
    heads = lambda n: pl.BlockSpec((None, n, t, LANES), lambda b, i: (b, 0, i, 0))
    tab = pl.BlockSpec((t, LANES), lambda b, i: (i, 0))
    out_shape = [
        jax.ShapeDtypeStruct((bsz, A_HEADS, s, LANES), BF16),
        jax.ShapeDtypeStruct((bsz, A_KV_HEADS, s, LANES), BF16),
        jax.ShapeDtypeStruct((bsz, A_KV_HEADS, s, LANES), BF16),
        jax.ShapeDtypeStruct((bsz, s, B_K), F32),
        jax.ShapeDtypeStruct((bsz, s, B_K), F32),
        jax.ShapeDtypeStruct((bsz, s, B_V), F32),
        jax.ShapeDtypeStruct((bsz, s, B_V), F32),
        jax.ShapeDtypeStruct((bsz, s, B_K), F32),
        jax.ShapeDtypeStruct((bsz, s, B_K), F32),
    ]
    return pl.pallas_call(
        _even_in_kernel,
        out_shape=out_shape,
        grid=(bsz, s // t),
        in_specs=[row(d), _mods_spec(n_ctx_tiles, d), _resident(gain.shape),
                  _resident(wq.shape), _resident(wkv.shape), _resident(wb.shape), _resident(wlr.shape),
                  _resident(wgate.shape), _resident(bgate.shape), _resident(qg.shape), _resident(kg.shape),
                  tab, tab],
        out_specs=[heads(A_HEADS), heads(A_KV_HEADS), heads(A_KV_HEADS),
                   row(B_K), row(B_K), row(B_V), row(B_V), row(B_K), row(B_K)],
        compiler_params=_params("parallel", "parallel"),
        name="even_in_proj",
    )(xx, mods, gain, wq, wkv, wb, wlr, wgate, bgate, qg, kg, cos, sin)


def _attn_kernel(q_ref, k_ref, v_ref, o_ref, s_ref, acc_ref, *, n_ctx_q_tiles, n_ctx_chunks, n_chunks):
    i = pl.program_id(2)
    tq = q_ref.shape[1]
    rows = A_GROUP * tq
    qs = q_ref[...].reshape(rows, LANES)
    nkv = jnp.where(i < n_ctx_q_tiles, n_ctx_chunks, n_chunks)

    def scores(c, m):
        kc = k_ref[pl.ds(pl.multiple_of(c * ATT_TK, ATT_TK), ATT_TK), :]
        s = lax.dot_general(qs, kc, NT_DIMS, preferred_element_type=F32)
        s_ref[c] = s
        return jnp.maximum(m, jnp.max(s, axis=1, keepdims=True))

    m = lax.fori_loop(0, nkv, scores, jnp.full((rows, 1), -jnp.inf, F32))
    acc_ref[...] = jnp.zeros_like(acc_ref)

    def weighted(c, carry):
        p = jnp.exp(s_ref[c] - m).astype(BF16)
        vc = v_ref[pl.ds(pl.multiple_of(c * ATT_TK, ATT_TK), ATT_TK), :]
        acc_ref[...] += _dot(p, vc)
        return carry

    lax.fori_loop(0, nkv, weighted, 0)
    acc = acc_ref[...]
    out = acc / acc[:, A_HEAD_DIM:A_HEAD_DIM + 1]
    for j in range(A_GROUP):
        o_ref[:, j * LANES:(j + 1) * LANES] = out[j * tq:(j + 1) * tq].astype(BF16)


def attention(q, k, v, n_ctx_rows):
    bsz, _, s, _ = q.shape
    tq = ATT_TQ
    kernel = functools.partial(_attn_kernel, n_ctx_q_tiles=n_ctx_rows // tq,
                               n_ctx_chunks=n_ctx_rows // ATT_TK, n_chunks=s // ATT_TK)
    kv_spec = pl.BlockSpec((None, None, s, LANES), lambda b, g, i: (b, g, 0, 0))
    return pl.pallas_call(
        kernel,
        out_shape=jax.ShapeDtypeStruct((bsz, s, A_HEADS * LANES), BF16),
        grid=(bsz, A_KV_HEADS, s // tq),
        in_specs=[pl.BlockSpec((None, A_GROUP, tq, LANES), lambda b, g, i: (b, g, i, 0)), kv_spec, kv_spec],
        out_specs=pl.BlockSpec((None, tq, A_GROUP * LANES), lambda b, g, i: (b, i, g)),
        scratch_shapes=[pltpu.VMEM((s // ATT_TK, A_GROUP * tq, ATT_TK), F32),
                        pltpu.VMEM((A_GROUP * tq, LANES), F32)],
        compiler_params=_params("parallel", "parallel", "arbitrary"),
        name="gqa_attention",
    )(q, k, v)


def _gla_consts(reverse):
    c, sub = GLA_CHUNK, GLA_SUB
    t = lax.broadcasted_iota(jnp.int32, (c, c), 0)
    s = lax.broadcasted_iota(jnp.int32, (c, c), 1)
    if reverse:
        tri = s >= t
        blk = s >= (t // sub + 1) * sub
    else:
        tri = s <= t
        blk = s < (t // sub) * sub
    sums = jnp.concatenate([tri, blk], axis=0).astype(BF16)
    n = B_HEADS * c
    rt = lax.broadcasted_iota(jnp.int32, (n, n), 0) % c
    cs = lax.broadcasted_iota(jnp.int32, (n, n), 1) % c
    causal = (cs >= rt) if reverse else (cs <= rt)
    lane = lax.broadcasted_iota(jnp.int32, (1, B_K), 1)
    head_masks = [(lane // B_DK == hh).astype(F32) for hh in range(B_HEADS)]
    row_sub = lax.broadcasted_iota(jnp.int32, (n, 1), 0) % c // sub
    key_row = lax.broadcasted_iota(jnp.int32, (c, 1), 0)
    return sums, causal, head_masks, row_sub, key_row


def _gla_chunk(q, k, v, la, st, consts, reverse):
    sums, causal, head_masks, row_sub, key_row = consts
    c, sub = GLA_CHUNK, GLA_SUB

    def stack_heads(x):
        return jnp.concatenate([x * hm for hm in head_masks], axis=0)

    la_hi, la_lo = _split_bf16(la)
    cr = _dot(sums, la_hi) + _dot(sums, la_lo)
    cum = cr[0:c]
    ref = cr[c:2 * c]
    edge = 0 if reverse else c - 1
    total = cum[edge:edge + 1]

    q_in = stack_heads(q * jnp.exp(cum)).astype(BF16)
    q_loc = stack_heads(q * jnp.exp(cum - ref)).astype(BF16)
    k_out = stack_heads(k * jnp.exp(total - cum)).astype(BF16)

    scores = jnp.zeros((B_HEADS * c, B_HEADS * c), F32)
    for i in range(c // sub):
        ref_i = ref[i * sub:i * sub + 1]
        valid = (key_row >= i * sub) if reverse else (key_row < (i + 1) * sub)
        k_i = stack_heads(k * jnp.exp(jnp.where(valid, ref_i - cum, -jnp.inf))).astype(BF16)
        s_i = lax.dot_general(q_loc, k_i, NT_DIMS, preferred_element_type=F32)
        scores = jnp.where(row_sub == i, s_i, scores)
    scores = jnp.where(causal, scores, 0.0).astype(BF16)

    v_st = jnp.concatenate([v[:, hh * B_DV:(hh + 1) * B_DV] for hh in range(B_HEADS)], axis=0).astype(BF16)
    o = _dot(scores, v_st) + lax.dot_general(q_in, st.astype(BF16), NT_DIMS, preferred_element_type=F32)
    st_new = st * jnp.exp(total) + lax.dot_general(v_st, k_out, TN_DIMS, preferred_element_type=F32)
    return o, st_new


def _gla_kernel(qf_ref, kf_ref, vf_ref, laf_ref, qr_ref, kr_ref, vr_ref, lar_ref,
                of_ref, or_ref, stf_ref, str_ref):
    @pl.when(pl.program_id(1) == 0)
    def _():
        stf_ref[...] = jnp.zeros_like(stf_ref)
        str_ref[...] = jnp.zeros_like(str_ref)

    c = GLA_CHUNK
    n_chunks = qf_ref.shape[0] // c
    for reverse, (q_ref, k_ref, v_ref, la_ref, o_ref, st_ref) in (
            (False, (qf_ref, kf_ref, vf_ref, laf_ref, of_ref, stf_ref)),
            (True, (qr_ref, kr_ref, vr_ref, lar_ref, or_ref, str_ref))):
        consts = _gla_consts(reverse)
        st = st_ref[...]
        order = range(n_chunks - 1, -1, -1) if reverse else range(n_chunks)
        for n in order:
            rows = slice(n * c, (n + 1) * c)
            o, st = _gla_chunk(q_ref[rows, :], k_ref[rows, :], v_ref[rows, :], la_ref[rows, :], st, consts, reverse)
            for hh in range(B_HEADS):
                o_ref[rows, hh * B_DV:(hh + 1) * B_DV] = o[hh * c:(hh + 1) * c]
        st_ref[...] = st


def gla_bidir(qb, kb, vb, laf, lab, n_ctx_tiles):
    bsz, s, _ = qb.shape
    t = ROW_TILE
    nt = s // t

    def rev_tile(j):
        return jnp.where(j < n_ctx_tiles, n_ctx_tiles - 1 - j, nt - 1 - (j - n_ctx_tiles))

    fwd = lambda w: pl.BlockSpec((None, t, w), lambda b, j: (b, j, 0))
    rev = lambda w: pl.BlockSpec((None, t, w), lambda b, j: (b, rev_tile(j), 0))
    return pl.pallas_call(
        _gla_kernel,
        out_shape=[jax.ShapeDtypeStruct((bsz, s, B_V), F32)] * 2,
        grid=(bsz, nt),
        in_specs=[fwd(B_K), fwd(B_K), fwd(B_V), fwd(B_K), rev(B_K), rev(B_K), rev(B_V), rev(B_K)],
        out_specs=[fwd(B_V), rev(B_V)],
        scratch_shapes=[pltpu.VMEM((B_DV, B_K), F32)] * 2,
        compiler_params=_params("parallel", "arbitrary"),
        name="gla_bidir",
    )(qb, kb, vb, laf, qb, kb, vb, lab)


def _even_out_kernel(x_ref, mod_ref, a_ref, of_ref, or_ref, g_ref, og_ref, wa_ref, wb_ref, o_ref):
    o = of_ref[...] + or_ref[...]
    g = g_ref[...]
    parts = []
    for hh in range(B_HEADS):
        oh = o[:, hh * B_DV:(hh + 1) * B_DV]
        ms = jnp.mean(oh * oh, axis=-1, keepdims=True)
        y = oh * lax.rsqrt(ms + NORM_EPS) * og_ref[...]
        parts.append((y * _silu(g[:, hh * B_DV:(hh + 1) * B_DV])).astype(BF16))
    gla = jnp.concatenate(parts, axis=1)
    y = _dot(a_ref[...], wa_ref[...]) + _dot(gla, wb_ref[...])
    o_ref[...] = x_ref[...] + mod_ref[2:3, :] * y


def even_out_proj(xx, mods, a, o_f, o_r, gb, o_gain, wa, wb, n_ctx_tiles):
    bsz, s, d = xx.shape
    t = ROW_TILE
    row = lambda w: pl.BlockSpec((None, t, w), lambda b, i: (b, i, 0))
    return pl.pallas_call(
        _even_out_kernel,
        out_shape=jax.ShapeDtypeStruct((bsz, s, d), F32),
        grid=(bsz, s // t),
        in_specs=[row(d), _mods_spec(n_ctx_tiles, d), row(a.shape[-1]), row(B_V), row(B_V), row(B_V),
                  _resident(o_gain.shape), _resident(wa.shape), _resident(wb.shape)],
        out_specs=row(d),
        compiler_params=_params("parallel", "parallel"),
        name="even_out_proj",
    )(xx, mods, a, o_f, o_r, gb, o_gain, wa, wb)


def _ffn_kernel(x_ref, xp_ref, xn_ref, mod_ref, gain_ref, wu_ref, cw_ref, wd_ref, fg_ref, o_ref,
                h_ref, ug_ref, uv_ref, acc_ref, *, tile0, n_ctx_tiles, n_tiles, final_norm):
    ti = pl.program_id(1) + tile0
    t = x_ref.shape[0]
    n_chunk = wd_ref.shape[0]
    gain = gain_ref[...]
    shift = mod_ref[3:4, :]
    scale = mod_ref[4:5, :]
    x = x_ref[...]
    has_prev = jnp.logical_and(ti != 0, ti != n_ctx_tiles)
    has_next = jnp.logical_and(ti != n_ctx_tiles - 1, ti != n_tiles - 1)
    hp = jnp.where(has_prev, _norm_mod(xp_ref[...], gain, shift, scale), 0.0)
    hn = jnp.where(has_next, _norm_mod(xn_ref[...], gain, shift, scale), 0.0)
    h_ref[...] = jnp.concatenate([hp, _norm_mod(x, gain, shift, scale), hn], axis=0).astype(BF16)
    acc_ref[...] = jnp.zeros_like(acc_ref)
    lo = FFN_HALO - 1

    def chunk(c, carry):
        h = h_ref[...]
        ug_ref[...] = _dot(h, wu_ref[c])
        uv_ref[...] = _dot(h, wu_ref[c + n_chunk])
        cg = cw_ref[c]
        cv = cw_ref[c + n_chunk]
        yg = cg[0:1] * ug_ref[lo:lo + t, :] + cg[1:2] * ug_ref[lo + 1:lo + 1 + t, :] + cg[2:3] * ug_ref[lo + 2:lo + 2 + t, :]
        yv = cv[0:1] * uv_ref[lo:lo + t, :] + cv[1:2] * uv_ref[lo + 1:lo + 1 + t, :] + cv[2:3] * uv_ref[lo + 2:lo + 2 + t, :]
        act = (_silu(yg) * yv).astype(BF16)
        acc_ref[...] += _dot(act, wd_ref[c])
        return carry

    lax.fori_loop(0, n_chunk, chunk, 0)
    y = x + mod_ref[5:6, :] * acc_ref[...]
    if final_norm:
        ms = jnp.mean(y * y, axis=-1, keepdims=True)
        y = y * lax.rsqrt(ms + NORM_EPS) * fg_ref[...]
    o_ref[...] = y


def conv_ffn(xx, mods, gain, wu, cw, wd, final_gain, n_ctx_tiles, latents_only, final_norm):
    bsz, s, d = xx.shape
    t = ROW_TILE
    nt = s // t
    tile0 = n_ctx_tiles if latents_only else 0
    hb = t // FFN_HALO
    last_hb = s // FFN_HALO - 1
    kernel = functools.partial(_ffn_kernel, tile0=tile0, n_ctx_tiles=n_ctx_tiles, n_tiles=nt, final_norm=final_norm)
    return pl.pallas_call(
        kernel,
        out_shape=jax.ShapeDtypeStruct((bsz, s - tile0 * t, d), F32),
        grid=(bsz, nt - tile0),
        in_specs=[pl.BlockSpec((None, t, d), lambda b, i: (b, i + tile0, 0)),
                  pl.BlockSpec((None, FFN_HALO, d), lambda b, i: (b, jnp.maximum((i + tile0) * hb - 1, 0), 0)),
                  pl.BlockSpec((None, FFN_HALO, d), lambda b, i: (b, jnp.minimum((i + tile0 + 1) * hb, last_hb), 0)),
                  pl.BlockSpec((None, None, N_MOD, d),
                               lambda b, i: (b, (i + tile0 >= n_ctx_tiles).astype(jnp.int32), 0, 0)),
                  _resident(gain.shape), _resident(wu.shape), _resident(cw.shape), _resident(wd.shape),
                  _resident(final_gain.shape)],
        out_specs=pl.BlockSpec((None, t, d), lambda b, i: (b, i, 0)),
        scratch_shapes=[pltpu.VMEM((t + 2 * FFN_HALO, d), BF16),
                        pltpu.VMEM((t + 2 * FFN_HALO, FFN_CN), F32),
                        pltpu.VMEM((t + 2 * FFN_HALO, FFN_CN), F32),
                        pltpu.VMEM((t, d), F32)],
        compiler_params=_params("parallel", "parallel"),
        name="conv_ffn_final" if final_norm else "conv_ffn",
    )(xx, xx, xx, mods, gain, wu, cw, wd, final_gain)


def _odd_in_kernel(x_ref, mod_ref, gain_ref, wg_ref, wr_ref, gate_ref, rec_ref):
    h = _norm_mod(x_ref[...], gain_ref[...], mod_ref[0:1, :], mod_ref[1:2, :]).astype(BF16)
    gate_ref[...] = jax.nn.gelu(_dot(h, wg_ref[...]), approximate=True)
    rec_ref[...] = _dot(h, wr_ref[...])


def odd_in_proj(xx, mods, gain, wg, wr, n_ctx_tiles):
    bsz, s, d = xx.shape
    t = ROW_TILE
    w = wg.shape[1]
    return pl.pallas_call(
        _odd_in_kernel,
        out_shape=[jax.ShapeDtypeStruct((bsz, s, w), F32), jax.ShapeDtypeStruct((s, bsz * w), F32)],
        grid=(bsz, s // t),
        in_specs=[pl.BlockSpec((None, t, d), lambda b, i: (b, i, 0)), _mods_spec(n_ctx_tiles, d),
                  _resident(gain.shape), _resident(wg.shape), _resident(wr.shape)],
        out_specs=[pl.BlockSpec((None, t, w), lambda b, i: (b, i, 0)),
                   pl.BlockSpec((t, w), lambda b, i: (i, b))],
        compiler_params=_params("parallel", "parallel"),
        name="odd_in_proj",
    )(xx, mods, gain, wg, wr)


def _lru_kernel(uf_ref, ufp_ref, ufn_ref, ur_ref, urp_ref, urn_ref, cw_ref, wax_ref, bax_ref, lam_ref,
                hf_ref, hr_ref, ue_ref, a_ref, x_ref, hst_ref, *, n_ctx_blocks, n_blocks):
    j = pl.program_id(0)
    tt, bsz, w = uf_ref.shape

    @pl.when(j == 0)
    def _():
        hst_ref[...] = jnp.zeros_like(hst_ref)

    rev_blk = jnp.where(j < n_ctx_blocks, n_ctx_blocks - 1 - j, n_blocks - 1 - (j - n_ctx_blocks))
    for d, (blk, u_ref, up_ref, un_ref, o_ref) in enumerate(
            ((j, uf_ref, ufp_ref, ufn_ref, hf_ref), (rev_blk, ur_ref, urp_ref, urn_ref, hr_ref))):
        has_prev = jnp.logical_and(blk != 0, blk != n_ctx_blocks)
        has_next = jnp.logical_and(blk != n_ctx_blocks - 1, blk != n_blocks - 1)
        ue_ref[0:1] = jnp.where(has_prev, up_ref[...], 0.0)
        ue_ref[1:tt + 1] = u_ref[...]
        ue_ref[tt + 1:tt + 3] = jnp.where(has_next, un_ref[...], 0.0)
        cw = cw_ref[...]
        uc = cw[0:1] * ue_ref[0:tt] + cw[1:2] * ue_ref[1:tt + 1] + cw[2:3] * ue_ref[2:tt + 2] + cw[3:4] * ue_ref[3:tt + 3]
        uc = uc.reshape(tt * bsz, w)
        neg_c_softplus = -LRU_C * _softplus(-lam_ref[d])
        for hh in range(LRU_HEADS):
            cols = slice(hh * LRU_HEAD_DIM, (hh + 1) * LRU_HEAD_DIM)
            uh = uc[:, cols]
            pre = _dot(uh.astype(BF16), wax_ref[d, hh]) + bax_ref[d, hh]
            r = _sigmoid(pre[:, 0:LRU_HEAD_DIM])
            ig = _sigmoid(pre[:, LRU_HEAD_DIM:2 * LRU_HEAD_DIM])
            log_a = r * neg_c_softplus[:, cols]
            a = jnp.exp(log_a)
            mult = jnp.sqrt(-jnp.tanh(log_a) * (1.0 + a * a))
            a_ref[:, :, cols] = a.reshape(tt, bsz, LRU_HEAD_DIM)
            x_ref[:, :, cols] = (mult * ig * uh).reshape(tt, bsz, LRU_HEAD_DIM)
        h = hst_ref[d]
        for step in range(tt):
            tcur = tt - 1 - step if d == 1 else step
            h = a_ref[tcur] * h + x_ref[tcur]
            o_ref[tcur] = h
        hst_ref[d] = h


def lru_scan(u, cw, wax, bax, lam, n_ctx_rows):
    s, bsz, w = u.shape
    tt = LRU_TT
    nb = s // tt
    ncb = n_ctx_rows // tt

    def rev_blk(j):
        return jnp.where(j < ncb, ncb - 1 - j, nb - 1 - (j - ncb))

    def specs(blk):
        return [pl.BlockSpec((tt, bsz, w), lambda j: (blk(j), 0, 0)),
                pl.BlockSpec((1, bsz, w), lambda j: (jnp.maximum(blk(j) * tt - 1, 0), 0, 0)),
                pl.BlockSpec((2, bsz, w), lambda j: (jnp.minimum((blk(j) + 1) * (tt // 2), s // 2 - 1), 0, 0))]

    fwd_blk = lambda j: j
    kernel = functools.partial(_lru_kernel, n_ctx_blocks=ncb, n_blocks=nb)
    return pl.pallas_call(
        kernel,
        out_shape=[jax.ShapeDtypeStruct((s, bsz, w), F32)] * 2,
        grid=(nb,),
        in_specs=specs(fwd_blk) + specs(rev_blk) + [_resident(cw.shape), _resident(wax.shape),
                                                     _resident(bax.shape), _resident(lam.shape)],
        out_specs=[pl.BlockSpec((tt, bsz, w), lambda j: (j, 0, 0)),
                   pl.BlockSpec((tt, bsz, w), lambda j: (rev_blk(j), 0, 0))],
        scratch_shapes=[pltpu.VMEM((tt + 3, bsz, w), F32), pltpu.VMEM((tt, bsz, w), F32),
                        pltpu.VMEM((tt, bsz, w), F32), pltpu.VMEM((2, bsz, w), F32)],
        compiler_params=_params("arbitrary"),
        name="lru_scan",
    )(u, u, u, u, u, u, cw, wax, bax, lam)


def _odd_out_kernel(x_ref, mod_ref, gate_ref, hf_ref, hr_ref, w_ref, o_ref):
    mixed = (gate_ref[...] * (hf_ref[...] + hr_ref[...])).astype(BF16)
    o_ref[...] = x_ref[...] + mod_ref[2:3, :] * _dot(mixed, w_ref[...])


def odd_out_proj(xx, mods, gate, hf, hr, w_out, n_ctx_tiles):
    bsz, s, d = xx.shape
    t = ROW_TILE
    w = gate.shape[-1]
    row = lambda n: pl.BlockSpec((None, t, n), lambda b, i: (b, i, 0))
    tmaj = pl.BlockSpec((t, w), lambda b, i: (i, b))
    return pl.pallas_call(
        _odd_out_kernel,
        out_shape=jax.ShapeDtypeStruct((bsz, s, d), F32),
        grid=(bsz, s // t),
        in_specs=[row(d), _mods_spec(n_ctx_tiles, d), row(w), tmaj, tmaj, _resident(w_out.shape)],
        out_specs=row(d),
        compiler_params=_params("parallel", "parallel"),
        name="odd_out_proj",
    )(xx, mods, gate, hf, hr, w_out)


def _pad_heads(w, n_heads, head_dim):
    d = w.shape[0]
    w = w.reshape(d, n_heads, head_dim)
    return jnp.pad(w, ((0, 0), (0, 0), (0, LANES - head_dim))).reshape(d, n_heads * LANES)


def _rope_tables(n_ctx, n_lat):
    rows = n_lat // GRID_W
    row = jnp.repeat(jnp.arange(rows, dtype=F32), GRID_W)
    col = jnp.tile(jnp.arange(GRID_W, dtype=F32), rows)
    n_freq = A_HEAD_DIM // 4
    inv_freq = ROPE_THETA ** (-jnp.arange(n_freq, dtype=F32) / n_freq)
    ar = row[:, None] * inv_freq
    ac = col[:, None] * inv_freq
    cos = jnp.concatenate([jnp.cos(ar), jnp.cos(ar), jnp.cos(ac), jnp.cos(ac)], axis=-1)
    sin = jnp.concatenate([-jnp.sin(ar), jnp.sin(ar), -jnp.sin(ac), jnp.sin(ac)], axis=-1)
    cos = jnp.concatenate([jnp.ones((n_ctx, A_HEAD_DIM), F32), cos], axis=0)
    sin = jnp.concatenate([jnp.zeros((n_ctx, A_HEAD_DIM), F32), sin], axis=0)
    return jnp.tile(cos, (1, 2)), jnp.tile(sin, (1, 2))


def kernel(x, c, ctx, c_ctx, ada_w, ada_b, norm_mix, norm_ffn, ffn_w_up, ffn_conv, ffn_w_down, even_w_in, even_w_out, attn_q_gain, attn_k_gain, gla_gate_w_up, gla_gate_b, gla_out_gain, lru_w_in, lru_conv, lru_lambda, lru_w_a, lru_b_a, lru_w_x, lru_b_x, lru_w_out, final_gain):
    bsz, n_lat, d = x.shape
    n_ctx = ctx.shape[1]
    depth = ada_w.shape[0]
    d_ff = ffn_w_down.shape[1]
    assert n_ctx % ROW_TILE == 0 and n_lat % ROW_TILE == 0 and d_ff % FFN_CN == 0
    n_ctx_tiles = n_ctx // ROW_TILE
    s = n_ctx + n_lat

    xx = jnp.concatenate([ctx, x], axis=1)
    cond_rows = -(-(bsz + 1) // 8) * 8
    cond = jnp.zeros((cond_rows, d), F32).at[:bsz].set(c).at[bsz].set(c_ctx)
    cos, sin = _rope_tables(n_ctx, n_lat)
    pad_gain = lambda g: jnp.pad(g, (0, LANES - A_HEAD_DIM)).reshape(1, LANES)

    for l in range(depth):
        last = l == depth - 1
        j = l // 2
        table = adaln_table(cond, ada_w[l], ada_b[l])
        m_lat = table[:bsz].reshape(bsz, N_MOD, d)
        m_ctx = jnp.broadcast_to(table[bsz].reshape(1, N_MOD, d), (bsz, N_MOD, d))
        mods = jnp.stack([m_ctx, m_lat], axis=1)
        gain_mix = norm_mix[l].reshape(1, d)

        if l % 2 == 0:
            w_in = even_w_in[j]
            o0 = A_HEADS * A_HEAD_DIM
            o1 = o0 + A_KV_HEADS * A_HEAD_DIM
            o2 = o1 + A_KV_HEADS * A_HEAD_DIM
            o3 = o2 + 2 * B_K + 2 * B_V
            wq = _pad_heads(w_in[:, :o0], A_HEADS, A_HEAD_DIM).astype(BF16)
            wkv = jnp.concatenate([_pad_heads(w_in[:, o0:o1], A_KV_HEADS, A_HEAD_DIM),
                                   _pad_heads(w_in[:, o1:o2], A_KV_HEADS, A_HEAD_DIM)], axis=1).astype(BF16)
            wb = w_in[:, o2:o3].astype(BF16)
            wlr = w_in[:, o3:].astype(BF16)
            zeros = jnp.zeros((B_GATE_RANK, B_K), F32)
            wgate = jnp.concatenate([jnp.concatenate([gla_gate_w_up[j, 0], zeros], axis=1),
                                     jnp.concatenate([zeros, gla_gate_w_up[j, 1]], axis=1)], axis=0)
            bgate = gla_gate_b[j].reshape(1, 2 * B_K)
            q, k, v, qb, kb, vb, gb, laf, lab = even_in_proj(
                xx, mods, gain_mix, wq, wkv, wb, wlr, wgate, bgate,
                pad_gain(attn_q_gain[j]), pad_gain(attn_k_gain[j]), cos, sin, n_ctx_tiles)
            a = attention(q, k, v, n_ctx)
            o_f, o_r = gla_bidir(qb, kb, vb, laf, lab, n_ctx_tiles)
            w_out = even_w_out[j]
            wa = jnp.pad(w_out[:o0].reshape(A_HEADS, A_HEAD_DIM, d),
                         ((0, 0), (0, LANES - A_HEAD_DIM), (0, 0))).reshape(A_HEADS * LANES, d).astype(BF16)
            xx = even_out_proj(xx, mods, a, o_f, o_r, gb, gla_out_gain[j].reshape(1, B_DV), wa,
                               w_out[o0:].astype(BF16), n_ctx_tiles)
        else:
            w_in = lru_w_in[j]
            gate, rec = odd_in_proj(xx, mods, gain_mix, w_in[:, :LRU_WIDTH].astype(BF16),
                                    w_in[:, LRU_WIDTH:].astype(BF16), n_ctx_tiles)
            wax = jnp.concatenate([lru_w_a[j], lru_w_x[j]], axis=-1).astype(BF16)
            bax = jnp.concatenate([lru_b_a[j].reshape(2, LRU_HEADS, 1, LRU_HEAD_DIM),
                                   lru_b_x[j].reshape(2, LRU_HEADS, 1, LRU_HEAD_DIM)], axis=-1)
            hf, hr = lru_scan(rec.reshape(s, bsz, LRU_WIDTH), lru_conv[j], wax, bax,
                              lru_lambda[j].reshape(2, 1, LRU_WIDTH), n_ctx)
            xx = odd_out_proj(xx, mods, gate, hf.reshape(s, bsz * LRU_WIDTH), hr.reshape(s, bsz * LRU_WIDTH),
                              lru_w_out[j].astype(BF16), n_ctx_tiles)

        n_chunk = d_ff // FFN_CN
        wu = ffn_w_up[l].reshape(d, 2 * n_chunk, FFN_CN).transpose(1, 0, 2).astype(BF16)
        cw = ffn_conv[l].reshape(-1, 2 * n_chunk, FFN_CN).transpose(1, 0, 2)
        wd = ffn_w_down[l].reshape(n_chunk, FFN_CN, d).astype(BF16)
        xx = conv_ffn(xx, mods, norm_ffn[l].reshape(1, d), wu, cw, wd, final_gain.reshape(1, d),
                      n_ctx_tiles, latents_only=last, final_norm=last)
    return xx
```

```python
import functools

import numpy as np
import jax
import jax.numpy as jnp
from jax import lax
from jax.experimental import pallas as pl
from jax.experimental.pallas import tpu as pltpu

F32 = jnp.float32
BF16 = jnp.bfloat16

NORM_EPS = 1e-6
N_MOD = 6
GRID_W = 64
ROPE_THETA = 10000.0

A_HEADS = 8
A_KV_HEADS = 2
A_GROUP = A_HEADS // A_KV_HEADS
A_HEAD_DIM = 64

B_HEADS = 4
B_DK = 64
B_DV = 128
B_K = B_HEADS * B_DK
B_V = B_HEADS * B_DV
B_GATE_RANK = 16
B_GATE_TAU = 16.0
GLA_CHUNK = 64
GLA_SUB = 16

LRU_HEADS = 10
LRU_HEAD_DIM = 128
LRU_WIDTH = LRU_HEADS * LRU_HEAD_DIM
LRU_C = 8.0

LANES = 128
ROW_TILE = 256
ATT_TQ = 128
ATT_TK = 256
FFN_CN = 256
FFN_HALO = 8
LRU_TT = 32
VMEM_LIMIT = 56 * 1024 * 1024

LOG2E = float(np.log2(np.e))
ATT_MAX_STATIC_SHIFT = 40.0

NT_DIMS = (((1,), (1,)), ((), ()))
TN_DIMS = (((0,), (0,)), ((), ()))


def _params(*sem):
    return pltpu.CompilerParams(dimension_semantics=sem, vmem_limit_bytes=VMEM_LIMIT)


def _resident(shape):
    nd = len(shape)
    return pl.BlockSpec(shape, lambda *_: (0,) * nd, pipeline_mode=pl.Buffered(1))


def _dot(a, b):
    return jnp.dot(a, b, preferred_element_type=F32)


def _split_bf16(a):
    hi = a.astype(BF16)
    lo = (a - hi.astype(F32)).astype(BF16)
    return hi, lo


def _dot_f32(a, b):
    ah, al = _split_bf16(a)
    bh, bl = _split_bf16(b)
    return _dot(ah, bh) + _dot(ah, bl) + _dot(al, bh)


def _sigmoid(x):
    return 1.0 / (1.0 + jnp.exp(-x))


def _silu(x):
    return x * _sigmoid(x)


def _softplus(x):
    return jnp.maximum(x, 0.0) + jnp.log1p(jnp.exp(-jnp.abs(x)))


def _norm_mod(x, gain, shift, scale):
    ms = jnp.mean(x * x, axis=-1, keepdims=True)
    return (x * lax.rsqrt(ms + NORM_EPS) * gain) * (1.0 + scale) + shift


def _adaln_kernel(c_ref, w_ref, b_ref, o_ref):
    o_ref[...] = _dot_f32(_silu(c_ref[...]), w_ref[...]) + b_ref[...]


def adaln_table(cond, w, b):
    rows, d = cond.shape
    n = w.shape[1]
    tn = 768
    return pl.pallas_call(
        _adaln_kernel,
        out_shape=jax.ShapeDtypeStruct((rows, n), F32),
        grid=(n // tn,),
        in_specs=[pl.BlockSpec((rows, d), lambda j: (0, 0)),
                  pl.BlockSpec((d, tn), lambda j: (0, j)),
                  pl.BlockSpec((1, tn), lambda j: (0, j))],
        out_specs=pl.BlockSpec((rows, tn), lambda j: (0, j)),
        compiler_params=_params("arbitrary"),
        name="adaln_table",
    )(cond, w, b.reshape(1, n))


def _mods_spec(n_ctx_tiles, d):
    return pl.BlockSpec((None, None, N_MOD, d),
                        lambda b, i: (b, (i >= n_ctx_tiles).astype(jnp.int32), 0, 0))


def _even_in_kernel(x_ref, mod_ref, gain_ref, wq_ref, wkv_ref, wb_ref, wlr_ref, wgate_ref, bgate_ref,
                    qg_ref, kg_ref, qaug_ref, cos_ref, sin_ref,
                    q_ref, k_ref, v_ref, qb_ref, kb_ref, vb_ref, gb_ref, laf_ref, lab_ref):
    h = _norm_mod(x_ref[...], gain_ref[...], mod_ref[0:1, :], mod_ref[1:2, :]).astype(BF16)
    cos = cos_ref[...]
    sin = sin_ref[...]
    lane = lax.broadcasted_iota(jnp.int32, cos.shape, 1)
    first_half = (lane % 32) < 16
    k_aug = (lane == A_HEAD_DIM).astype(F32)

    def head_norm_rope(xh, g):
        ms = jnp.sum(xh * xh, axis=-1, keepdims=True) * (1.0 / A_HEAD_DIM)
        y = xh * lax.rsqrt(ms + NORM_EPS) * g
        swapped = jnp.where(first_half, pltpu.roll(y, LANES - 16, 1), pltpu.roll(y, 16, 1))
        return y * cos + swapped * sin

    qp = _dot(h, wq_ref[...])
    for hh in range(A_HEADS):
        qh = head_norm_rope(qp[:, hh * LANES:(hh + 1) * LANES], qg_ref[...])
        q_ref[hh] = (qh * (A_HEAD_DIM ** -0.5 * LOG2E) + qaug_ref[...]).astype(BF16)
    kvp = _dot(h, wkv_ref[...])
    for hh in range(A_KV_HEADS):
        k_ref[hh] = (head_norm_rope(kvp[:, hh * LANES:(hh + 1) * LANES], kg_ref[...]) + k_aug).astype(BF16)
        vh = kvp[:, (A_KV_HEADS + hh) * LANES:(A_KV_HEADS + hh + 1) * LANES]
        v_ref[hh] = jnp.where(lane >= A_HEAD_DIM, 1.0, vh).astype(BF16)

    pb = _dot(h, wb_ref[...])
    qb_ref[...] = pb[:, 0:B_K] * B_DK ** -0.5
    kb_ref[...] = pb[:, B_K:2 * B_K]
    vb_ref[...] = pb[:, 2 * B_K:2 * B_K + B_V]
    gb_ref[...] = pb[:, 2 * B_K + B_V:2 * B_K + 2 * B_V]

    lr = _dot(h, wlr_ref[...])
    z = _dot_f32(lr, wgate_ref[...]) + bgate_ref[...]
    la = -_softplus(-z) * (1.0 / B_GATE_TAU)
    laf_ref[...] = la[:, 0:B_K]
    lab_ref[...] = la[:, B_K:2 * B_K]


def even_in_proj(xx, mods, gain, wq, wkv, wb, wlr, wgate, bgate, qg, kg, qaug, cos, sin, n_ctx_tiles):
    bsz, s, d = xx.shape
    t = ROW_TILE
    row = lambda w: pl.BlockSpec((None, t, w), lambda b, i: (b, i, 0))
    heads = lambda n: pl.BlockSpec((None, n, t, LANES), lambda b, i: (b, 0, i, 0))
    tab = pl.BlockSpec((t, LANES), lambda b, i: (i, 0))
    out_shape = [
        jax.ShapeDtypeStruct((bsz, A_HEADS, s, LANES), BF16),
        jax.ShapeDtypeStruct((bsz, A_KV_HEADS, s, LANES), BF16),
        jax.ShapeDtypeStruct((bsz, A_KV_HEADS, s, LANES), BF16),
        jax.ShapeDtypeStruct((bsz, s, B_K), F32),
        jax.ShapeDtypeStruct((bsz, s, B_K), F32),
        jax.ShapeDtypeStruct((bsz, s, B_V), F32),
        jax.ShapeDtypeStruct((bsz, s, B_V), F32),
        jax.ShapeDtypeStruct((bsz, s, B_K), F32),
        jax.ShapeDtypeStruct((bsz, s, B_K), F32),
    ]
    return pl.pallas_call(
        _even_in_kernel,
        out_shape=out_shape,
        grid=(bsz, s // t),
        in_specs=[row(d), _mods_spec(n_ctx_tiles, d), _resident(gain.shape),
                  _resident(wq.shape), _resident(wkv.shape), _resident(wb.shape), _resident(wlr.shape),
                  _resident(wgate.shape), _resident(bgate.shape), _resident(qg.shape), _resident(kg.shape),
                  _resident(qaug.shape), tab, tab],
        out_specs=[heads(A_HEADS), heads(A_KV_HEADS), heads(A_KV_HEADS),
                   row(B_K), row(B_K), row(B_V), row(B_V), row(B_K), row(B_K)],
        compiler_params=_params("parallel", "parallel"),
        name="even_in_proj",
    )(xx, mods, gain, wq, wkv, wb, wlr, wgate, bgate, qg, kg, qaug, cos, sin)


def _attn_kernel(shift_ref, q_ref, k_ref, v_ref, o_ref, s_ref, p_ref, m_ref, *, n_ctx_q_tiles, n_ctx_rows):
    i = pl.program_id(2)
    tq = q_ref.shape[1]
    rows = A_GROUP * tq
    n_keys = k_ref.shape[0]

    def finish(acc):
        out = acc / acc[:, A_HEAD_DIM:A_HEAD_DIM + 1]
        for j in range(A_GROUP):
            o_ref[:, j * LANES:(j + 1) * LANES] = out[j * tq:(j + 1) * tq].astype(BF16)

    def attend_static_shift(nk):
        qs = q_ref[...].reshape(rows, LANES)
        acc = jnp.zeros((rows, LANES), F32)
        for c0 in range(0, nk, ATT_TK):
            s = lax.dot_general(qs, k_ref[c0:c0 + ATT_TK, :], NT_DIMS, preferred_element_type=F32)
            acc = acc + _dot(jnp.exp2(s).astype(BF16), v_ref[c0:c0 + ATT_TK, :])
        finish(acc)

    def attend_row_max(nk):
        qs = q_ref[...].reshape(rows, LANES)
        m_ref[...] = jnp.full(m_ref.shape, -jnp.inf, F32)
        for c0 in range(0, nk, ATT_TK):
            cols = slice(c0, c0 + ATT_TK)
            s = lax.dot_general(qs, k_ref[cols, :], NT_DIMS, preferred_element_type=F32)
            s_ref[:, cols] = s
            m_ref[...] = jnp.maximum(m_ref[...], jnp.maximum(s[:, 0:LANES], s[:, LANES:2 * LANES]))
        m = jnp.max(m_ref[...], axis=1, keepdims=True)
        for c0 in range(0, nk, ATT_TK):
            cols = slice(c0, c0 + ATT_TK)
            p_ref[:, cols] = jnp.exp2(s_ref[:, cols] - m).astype(BF16)
        finish(_dot(p_ref[:, 0:nk], v_ref[0:nk, :]))

    is_ctx = i < n_ctx_q_tiles
    static_ok = shift_ref[0] <= ATT_MAX_STATIC_SHIFT
    for ctx_tile, nk in ((True, n_ctx_rows), (False, n_keys)):
        tile_match = is_ctx if ctx_tile else jnp.logical_not(is_ctx)

        @pl.when(jnp.logical_and(tile_match, static_ok))
        def _():
            attend_static_shift(nk)

        @pl.when(jnp.logical_and(tile_match, jnp.logical_not(static_ok)))
        def _():
            attend_row_max(nk)


def attention(shift, q, k, v, n_ctx_rows):
    bsz, _, s, _ = q.shape
    tq = ATT_TQ
    assert ATT_TK == 2 * LANES and n_ctx_rows % ATT_TK == 0 and s % ATT_TK == 0
    kernel = functools.partial(_attn_kernel, n_ctx_q_tiles=n_ctx_rows // tq, n_ctx_rows=n_ctx_rows)
    kv_spec = pl.BlockSpec((None, None, s, LANES), lambda b, g, i: (b, g, 0, 0))
    return pl.pallas_call(
        kernel,
        out_shape=jax.ShapeDtypeStruct((bsz, s, A_HEADS * LANES), BF16),
        grid=(bsz, A_KV_HEADS, s // tq),
        in_specs=[pl.BlockSpec(memory_space=pltpu.SMEM),
                  pl.BlockSpec((None, A_GROUP, tq, LANES), lambda b, g, i: (b, g, i, 0)), kv_spec, kv_spec],
        out_specs=pl.BlockSpec((None, tq, A_GROUP * LANES), lambda b, g, i: (b, i, g)),
        scratch_shapes=[pltpu.VMEM((A_GROUP * tq, s), F32),
                        pltpu.VMEM((A_GROUP * tq, s), BF16),
                        pltpu.VMEM((A_GROUP * tq, LANES), F32)],
        compiler_params=_params("parallel", "parallel", "arbitrary"),
        name="gqa_attention",
    )(shift, q, k, v)


def _gla_consts(reverse):
    c, sub = GLA_CHUNK, GLA_SUB
    t = lax.broadcasted_iota(jnp.int32, (c, c), 0)
    s = lax.broadcasted_iota(jnp.int32, (c, c), 1)
    if reverse:
        tri = s >= t
        blk = s >= (t // sub + 1) * sub
    else:
        tri = s <= t
        blk = s < (t // sub) * sub
    sums = jnp.concatenate([tri, blk], axis=0).astype(BF16)
    n = B_HEADS * c
    rt = lax.broadcasted_iota(jnp.int32, (n, n), 0) % c
    cs = lax.broadcasted_iota(jnp.int32, (n, n), 1) % c
    causal = (cs >= rt) if reverse else (cs <= rt)
    lane = lax.broadcasted_iota(jnp.int32, (1, B_K), 1)
    head_masks = [(lane // B_DK == hh).astype(F32) for hh in range(B_HEADS)]
    row_sub = lax.broadcasted_iota(jnp.int32, (n, 1), 0) % c // sub
    key_row = lax.broadcasted_iota(jnp.int32, (c, 1), 0)
    return sums, causal, head_masks, row_sub, key_row


def _gla_chunk(q, k, v, la, st, consts, reverse):
    sums, causal, head_masks, row_sub, key_row = consts
    c, sub = GLA_CHUNK, GLA_SUB

    def stack_heads(x):
        return jnp.concatenate([x * hm for hm in head_masks], axis=0)

    la_hi, la_lo = _split_bf16(la)
    cr = _dot(sums, la_hi) + _dot(sums, la_lo)
    cum = cr[0:c]
    ref = cr[c:2 * c]
    edge = 0 if reverse else c - 1
    total = cum[edge:edge + 1]

    q_in = stack_heads(q * jnp.exp(cum)).astype(BF16)
    q_loc = stack_heads(q * jnp.exp(cum - ref)).astype(BF16)
    k_out = stack_heads(k * jnp.exp(total - cum)).astype(BF16)

    scores = jnp.zeros((B_HEADS * c, B_HEADS * c), F32)
    for i in range(c // sub):
        ref_i = ref[i * sub:i * sub + 1]
        valid = (key_row >= i * sub) if reverse else (key_row < (i + 1) * sub)
        k_i = stack_heads(k * jnp.exp(jnp.where(valid, ref_i - cum, -jnp.inf))).astype(BF16)
        s_i = lax.dot_general(q_loc, k_i, NT_DIMS, preferred_element_type=F32)
        scores = jnp.where(row_sub == i, s_i, scores)
    scores = jnp.where(causal, scores, 0.0).astype(BF16)

    v_st = jnp.concatenate([v[:, hh * B_DV:(hh + 1) * B_DV] for hh in range(B_HEADS)], axis=0).astype(BF16)
    o = _dot(scores, v_st) + lax.dot_general(q_in, st.astype(BF16), NT_DIMS, preferred_element_type=F32)
    st_new = st * jnp.exp(total) + lax.dot_general(v_st, k_out, TN_DIMS, preferred_element_type=F32)
    return o, st_new


def _gla_kernel(qf_ref, kf_ref, vf_ref, laf_ref, qr_ref, kr_ref, vr_ref, lar_ref,
                of_ref, or_ref, stf_ref, str_ref):
    @pl.when(pl.program_id(1) == 0)
    def _():
        stf_ref[...] = jnp.zeros_like(stf_ref)
        str_ref[...] = jnp.zeros_like(str_ref)

    c = GLA_CHUNK
    n_chunks = qf_ref.shape[0] // c
    for reverse, (q_ref, k_ref, v_ref, la_ref, o_ref, st_ref) in (
            (False, (qf_ref, kf_ref, vf_ref, laf_ref, of_ref, stf_ref)),
            (True, (qr_ref, kr_ref, vr_ref, lar_ref, or_ref, str_ref))):
        consts = _gla_consts(reverse)
        st = st_ref[...]
        order = range(n_chunks - 1, -1, -1) if reverse else range(n_chunks)
        for n in order:
            rows = slice(n * c, (n + 1) * c)
            o, st = _gla_chunk(q_ref[rows, :], k_ref[rows, :], v_ref[rows, :], la_ref[rows, :], st, consts, reverse)
            for hh in range(B_HEADS):
                o_ref[rows, hh * B_DV:(hh + 1) * B_DV] = o[hh * c:(hh + 1) * c]
        st_ref[...] = st


def gla_bidir(qb, kb, vb, laf, lab, n_ctx_tiles):
    bsz, s, _ = qb.shape
    t = ROW_TILE
    nt = s // t

    def rev_tile(j):
        return jnp.where(j < n_ctx_tiles, n_ctx_tiles - 1 - j, nt - 1 - (j - n_ctx_tiles))

    fwd = lambda w: pl.BlockSpec((None, t, w), lambda b, j: (b, j, 0))
    rev = lambda w: pl.BlockSpec((None, t, w), lambda b, j: (b, rev_tile(j), 0))
    return pl.pallas_call(
        _gla_kernel,
        out_shape=[jax.ShapeDtypeStruct((bsz, s, B_V), F32)] * 2,
        grid=(bsz, nt),
        in_specs=[fwd(B_K), fwd(B_K), fwd(B_V), fwd(B_K), rev(B_K), rev(B_K), rev(B_V), rev(B_K)],
        out_specs=[fwd(B_V), rev(B_V)],
        scratch_shapes=[pltpu.VMEM((B_DV, B_K), F32)] * 2,
        compiler_params=_params("parallel", "arbitrary"),
        name="gla_bidir",
    )(qb, kb, vb, laf, qb, kb, vb, lab)


def _even_out_kernel(x_ref, mod_ref, a_ref, of_ref, or_ref, g_ref, og_ref, wa_ref, wb_ref, o_ref):
    o = of_ref[...] + or_ref[...]
    g = g_ref[...]
    parts = []
    for hh in range(B_HEADS):
        oh = o[:, hh * B_DV:(hh + 1) * B_DV]
        ms = jnp.mean(oh * oh, axis=-1, keepdims=True)
        y = oh * lax.rsqrt(ms + NORM_EPS) * og_ref[...]
        parts.append((y * _silu(g[:, hh * B_DV:(hh + 1) * B_DV])).astype(BF16))
    gla = jnp.concatenate(parts, axis=1)
    y = _dot(a_ref[...], wa_ref[...]) + _dot(gla, wb_ref[...])
    o_ref[...] = x_ref[...] + mod_ref[2:3, :] * y


def even_out_proj(xx, mods, a, o_f, o_r, gb, o_gain, wa, wb, n_ctx_tiles):
    bsz, s, d = xx.shape
    t = ROW_TILE
    row = lambda w: pl.BlockSpec((None, t, w), lambda b, i: (b, i, 0))
    return pl.pallas_call(
        _even_out_kernel,
        out_shape=jax.ShapeDtypeStruct((bsz, s, d), F32),
        grid=(bsz, s // t),
        in_specs=[row(d), _mods_spec(n_ctx_tiles, d), row(a.shape[-1]), row(B_V), row(B_V), row(B_V),
                  _resident(o_gain.shape), _resident(wa.shape), _resident(wb.shape)],
        out_specs=row(d),
        compiler_params=_params("parallel", "parallel"),
        name="even_out_proj",
    )(xx, mods, a, o_f, o_r, gb, o_gain, wa, wb)


def _ffn_kernel(x_ref, xp_ref, xn_ref, mod_ref, gain_ref, wu_ref, cw_ref, wd_ref, fg_ref, o_ref,
                h_ref, ug_ref, uv_ref, acc_ref, *, tile0, n_ctx_tiles, n_tiles, final_norm):
    ti = pl.program_id(1) + tile0
    t = x_ref.shape[0]
    n_chunk = wd_ref.shape[0]
    gain = gain_ref[...]
    shift = mod_ref[3:4, :]
    scale = mod_ref[4:5, :]
    x = x_ref[...]
    has_prev = jnp.logical_and(ti != 0, ti != n_ctx_tiles)
    has_next = jnp.logical_and(ti != n_ctx_tiles - 1, ti != n_tiles - 1)
    hp = jnp.where(has_prev, _norm_mod(xp_ref[...], gain, shift, scale), 0.0)
    hn = jnp.where(has_next, _norm_mod(xn_ref[...], gain, shift, scale), 0.0)
    h_ref[...] = jnp.concatenate([hp, _norm_mod(x, gain, shift, scale), hn], axis=0).astype(BF16)
    acc_ref[...] = jnp.zeros_like(acc_ref)
    lo = FFN_HALO - 1

    def chunk(c, carry):
        h = h_ref[...]
        ug_ref[...] = _dot(h, wu_ref[c])
        uv_ref[...] = _dot(h, wu_ref[c + n_chunk])
        cg = cw_ref[c]
        cv = cw_ref[c + n_chunk]
        yg = cg[0:1] * ug_ref[lo:lo + t, :] + cg[1:2] * ug_ref[lo + 1:lo + 1 + t, :] + cg[2:3] * ug_ref[lo + 2:lo + 2 + t, :]
        yv = cv[0:1] * uv_ref[lo:lo + t, :] + cv[1:2] * uv_ref[lo + 1:lo + 1 + t, :] + cv[2:3] * uv_ref[lo + 2:lo + 2 + t, :]
        act = (_silu(yg) * yv).astype(BF16)
        acc_ref[...] += _dot(act, wd_ref[c])
        return carry

    lax.fori_loop(0, n_chunk, chunk, 0)
    y = x + mod_ref[5:6, :] * acc_ref[...]
    if final_norm:
        ms = jnp.mean(y * y, axis=-1, keepdims=True)
        y = y * lax.rsqrt(ms + NORM_EPS) * fg_ref[...]
    o_ref[...] = y


def conv_ffn(xx, mods, gain, wu, cw, wd, final_gain, n_ctx_tiles, latents_only, final_norm):
    bsz, s, d = xx.shape
    t = ROW_TILE
    nt = s // t
    tile0 = n_ctx_tiles if latents_only else 0
    hb = t // FFN_HALO
    last_hb = s // FFN_HALO - 1
    kernel = functools.partial(_ffn_kernel, tile0=tile0, n_ctx_tiles=n_ctx_tiles, n_tiles=nt, final_norm=final_norm)
    return pl.pallas_call(
        kernel,
        out_shape=jax.ShapeDtypeStruct((bsz, s - tile0 * t, d), F32),
        grid=(bsz, nt - tile0),
        in_specs=[pl.BlockSpec((None, t, d), lambda b, i: (b, i + tile0, 0)),
                  pl.BlockSpec((None, FFN_HALO, d), lambda b, i: (b, jnp.maximum((i + tile0) * hb - 1, 0), 0)),
                  pl.BlockSpec((None, FFN_HALO, d), lambda b, i: (b, jnp.minimum((i + tile0 + 1) * hb, last_hb), 0)),
                  pl.BlockSpec((None, None, N_MOD, d),
                               lambda b, i: (b, (i + tile0 >= n_ctx_tiles).astype(jnp.int32), 0, 0)),
                  _resident(gain.shape), _resident(wu.shape), _resident(cw.shape), _resident(wd.shape),
                  _resident(final_gain.shape)],
        out_specs=pl.BlockSpec((None, t, d), lambda b, i: (b, i, 0)),
        scratch_shapes=[pltpu.VMEM((t + 2 * FFN_HALO, d), BF16),
                        pltpu.VMEM((t + 2 * FFN_HALO, FFN_CN), F32),
                        pltpu.VMEM((t + 2 * FFN_HALO, FFN_CN), F32),
                        pltpu.VMEM((t, d), F32)],
        compiler_params=_params("parallel", "parallel"),
        name="conv_ffn_final" if final_norm else "conv_ffn",
    )(xx, xx, xx, mods, gain, wu, cw, wd, final_gain)


def _odd_in_kernel(x_ref, mod_ref, gain_ref, wg_ref, wr_ref, gate_ref, rec_ref):
    h = _norm_mod(x_ref[...], gain_ref[...], mod_ref[0:1, :], mod_ref[1:2, :]).astype(BF16)
    gate_ref[...] = jax.nn.gelu(_dot(h, wg_ref[...]), approximate=True)
    rec_ref[...] = _dot(h, wr_ref[...])


def odd_in_proj(xx, mods, gain, wg, wr, n_ctx_tiles):
    bsz, s, d = xx.shape
    t = ROW_TILE
    w = wg.shape[1]
    return pl.pallas_call(
        _odd_in_kernel,
        out_shape=[jax.ShapeDtypeStruct((bsz, s, w), F32), jax.ShapeDtypeStruct((s, bsz * w), F32)],
        grid=(bsz, s // t),
        in_specs=[pl.BlockSpec((None, t, d), lambda b, i: (b, i, 0)), _mods_spec(n_ctx_tiles, d),
                  _resident(gain.shape), _resident(wg.shape), _resident(wr.shape)],
        out_specs=[pl.BlockSpec((None, t, w), lambda b, i: (b, i, 0)),
                   pl.BlockSpec((t, w), lambda b, i: (i, b))],
        compiler_params=_params("parallel", "parallel"),
        name="odd_in_proj",
    )(xx, mods, gain, wg, wr)


def _lru_kernel(uf_ref, ufp_ref, ufn_ref, ur_ref, urp_ref, urn_ref, cw_ref, wax_ref, bax_ref, lam_ref,
                hf_ref, hr_ref, ue_ref, a_ref, x_ref, hst_ref, *, n_ctx_blocks, n_blocks):
    j = pl.program_id(0)
    tt, bsz, w = uf_ref.shape

    @pl.when(j == 0)
    def _():
        hst_ref[...] = jnp.zeros_like(hst_ref)

    rev_blk = jnp.where(j < n_ctx_blocks, n_ctx_blocks - 1 - j, n_blocks - 1 - (j - n_ctx_blocks))
    for d, (blk, u_ref, up_ref, un_ref, o_ref) in enumerate(
            ((j, uf_ref, ufp_ref, ufn_ref, hf_ref), (rev_blk, ur_ref, urp_ref, urn_ref, hr_ref))):
        has_prev = jnp.logical_and(blk != 0, blk != n_ctx_blocks)
        has_next = jnp.logical_and(blk != n_ctx_blocks - 1, blk != n_blocks - 1)
        ue_ref[0:1] = jnp.where(has_prev, up_ref[...], 0.0)
        ue_ref[1:tt + 1] = u_ref[...]
        ue_ref[tt + 1:tt + 3] = jnp.where(has_next, un_ref[...], 0.0)
        cw = cw_ref[...]
        uc = cw[0:1] * ue_ref[0:tt] + cw[1:2] * ue_ref[1:tt + 1] + cw[2:3] * ue_ref[2:tt + 2] + cw[3:4] * ue_ref[3:tt + 3]
        uc = uc.reshape(tt * bsz, w)
        neg_c_softplus = -LRU_C * _softplus(-lam_ref[d])
        for hh in range(LRU_HEADS):
            cols = slice(hh * LRU_HEAD_DIM, (hh + 1) * LRU_HEAD_DIM)
            uh = uc[:, cols]
            pre = _dot(uh.astype(BF16), wax_ref[d, hh]) + bax_ref[d, hh]
            r = _sigmoid(pre[:, 0:LRU_HEAD_DIM])
            ig = _sigmoid(pre[:, LRU_HEAD_DIM:2 * LRU_HEAD_DIM])
            log_a = r * neg_c_softplus[:, cols]
            a = jnp.exp(log_a)
            mult = jnp.sqrt(-jnp.tanh(log_a) * (1.0 + a * a))
            a_ref[:, :, cols] = a.reshape(tt, bsz, LRU_HEAD_DIM)
            x_ref[:, :, cols] = (mult * ig * uh).reshape(tt, bsz, LRU_HEAD_DIM)
        h = hst_ref[d]
        for step in range(tt):
            tcur = tt - 1 - step if d == 1 else step
            h = a_ref[tcur] * h + x_ref[tcur]
            o_ref[tcur] = h
        hst_ref[d] = h


def lru_scan(u, cw, wax, bax, lam, n_ctx_rows):
    s, bsz, w = u.shape
    tt = LRU_TT
    nb = s // tt
    ncb = n_ctx_rows // tt

    def rev_blk(j):
        return jnp.where(j < ncb, ncb - 1 - j, nb - 1 - (j - ncb))

    def specs(blk):
        return [pl.BlockSpec((tt, bsz, w), lambda j: (blk(j), 0, 0)),
                pl.BlockSpec((1, bsz, w), lambda j: (jnp.maximum(blk(j) * tt - 1, 0), 0, 0)),
                pl.BlockSpec((2, bsz, w), lambda j: (jnp.minimum((blk(j) + 1) * (tt // 2), s // 2 - 1), 0, 0))]

    fwd_blk = lambda j: j
    kernel = functools.partial(_lru_kernel, n_ctx_blocks=ncb, n_blocks=nb)
    return pl.pallas_call(
        kernel,
        out_shape=[jax.ShapeDtypeStruct((s, bsz, w), F32)] * 2,
        grid=(nb,),
        in_specs=specs(fwd_blk) + specs(rev_blk) + [_resident(cw.shape), _resident(wax.shape),
                                                     _resident(bax.shape), _resident(lam.shape)],
        out_specs=[pl.BlockSpec((tt, bsz, w), lambda j: (j, 0, 0)),
                   pl.BlockSpec((tt, bsz, w), lambda j: (rev_blk(j), 0, 0))],
        scratch_shapes=[pltpu.VMEM((tt + 3, bsz, w), F32), pltpu.VMEM((tt, bsz, w), F32),
                        pltpu.VMEM((tt, bsz, w), F32), pltpu.VMEM((2, bsz, w), F32)],
        compiler_params=_params("arbitrary"),
        name="lru_scan",
    )(u, u, u, u, u, u, cw, wax, bax, lam)


def _odd_out_kernel(x_ref, mod_ref, gate_ref, hf_ref, hr_ref, w_ref, o_ref):
    mixed = (gate_ref[...] * (hf_ref[...] + hr_ref[...])).astype(BF16)
    o_ref[...] = x_ref[...] + mod_ref[2:3, :] * _dot(mixed, w_ref[...])


def odd_out_proj(xx, mods, gate, hf, hr, w_out, n_ctx_tiles):
    bsz, s, d = xx.shape
    t = ROW_TILE
    w = gate.shape[-1]
    row = lambda n: pl.BlockSpec((None, t, n), lambda b, i: (b, i, 0))
    tmaj = pl.BlockSpec((t, w), lambda b, i: (i, b))
    return pl.pallas_call(
        _odd_out_kernel,
        out_shape=jax.ShapeDtypeStruct((bsz, s, d), F32),
        grid=(bsz, s // t),
        in_specs=[row(d), _mods_spec(n_ctx_tiles, d), row(w), tmaj, tmaj, _resident(w_out.shape)],
        out_specs=row(d),
        compiler_params=_params("parallel", "parallel"),
        name="odd_out_proj",
    )(xx, mods, gate, hf, hr, w_out)


def _pad_heads(w, n_heads, head_dim):
    d = w.shape[0]
    w = w.reshape(d, n_heads, head_dim)
    return jnp.pad(w, ((0, 0), (0, 0), (0, LANES - head_dim))).reshape(d, n_heads * LANES)


def _rope_tables(n_ctx, n_lat):
    rows = n_lat // GRID_W
    row = jnp.repeat(jnp.arange(rows, dtype=F32), GRID_W)
    col = jnp.tile(jnp.arange(GRID_W, dtype=F32), rows)
    n_freq = A_HEAD_DIM // 4
    inv_freq = ROPE_THETA ** (-jnp.arange(n_freq, dtype=F32) / n_freq)
    ar = row[:, None] * inv_freq
    ac = col[:, None] * inv_freq
    cos = jnp.concatenate([jnp.cos(ar), jnp.cos(ar), jnp.cos(ac), jnp.cos(ac)], axis=-1)
    sin = jnp.concatenate([-jnp.sin(ar), jnp.sin(ar), -jnp.sin(ac), jnp.sin(ac)], axis=-1)
    cos = jnp.concatenate([jnp.ones((n_ctx, A_HEAD_DIM), F32), cos], axis=0)
    sin = jnp.concatenate([jnp.zeros((n_ctx, A_HEAD_DIM), F32), sin], axis=0)
    return jnp.tile(cos, (1, 2)), jnp.tile(sin, (1, 2))


def kernel(x, c, ctx, c_ctx, ada_w, ada_b, norm_mix, norm_ffn, ffn_w_up, ffn_conv, ffn_w_down, even_w_in, even_w_out, attn_q_gain, attn_k_gain, gla_gate_w_up, gla_gate_b, gla_out_gain, lru_w_in, lru_conv, lru_lambda, lru_w_a, lru_b_a, lru_w_x, lru_b_x, lru_w_out, final_gain):
    bsz, n_lat, d = x.shape
    n_ctx = ctx.shape[1]
    depth = ada_w.shape[0]
    d_ff = ffn_w_down.shape[1]
    assert n_ctx % ROW_TILE == 0 and n_lat % ROW_TILE == 0 and d_ff % FFN_CN == 0
    n_ctx_tiles = n_ctx // ROW_TILE
    s = n_ctx + n_lat

    xx = jnp.concatenate([ctx, x], axis=1)
    cond_rows = -(-(bsz + 1) // 8) * 8
    cond = jnp.zeros((cond_rows, d), F32).at[:bsz].set(c).at[bsz].set(c_ctx)
    cos, sin = _rope_tables(n_ctx, n_lat)
    pad_gain = lambda g: jnp.pad(g, (0, LANES - A_HEAD_DIM)).reshape(1, LANES)

    for l in range(depth):
        last = l == depth - 1
        j = l // 2
        table = adaln_table(cond, ada_w[l], ada_b[l])
        m_lat = table[:bsz].reshape(bsz, N_MOD, d)
        m_ctx = jnp.broadcast_to(table[bsz].reshape(1, N_MOD, d), (bsz, N_MOD, d))
        mods = jnp.stack([m_ctx, m_lat], axis=1)
        gain_mix = norm_mix[l].reshape(1, d)

        if l % 2 == 0:
            w_in = even_w_in[j]
            o0 = A_HEADS * A_HEAD_DIM
            o1 = o0 + A_KV_HEADS * A_HEAD_DIM
            o2 = o1 + A_KV_HEADS * A_HEAD_DIM
            o3 = o2 + 2 * B_K + 2 * B_V
            wq = _pad_heads(w_in[:, :o0], A_HEADS, A_HEAD_DIM).astype(BF16)
            wkv = jnp.concatenate([_pad_heads(w_in[:, o0:o1], A_KV_HEADS, A_HEAD_DIM),
                                   _pad_heads(w_in[:, o1:o2], A_KV_HEADS, A_HEAD_DIM)], axis=1).astype(BF16)
            wb = w_in[:, o2:o3].astype(BF16)
            wlr = w_in[:, o3:].astype(BF16)
            zeros = jnp.zeros((B_GATE_RANK, B_K), F32)
            wgate = jnp.concatenate([jnp.concatenate([gla_gate_w_up[j, 0], zeros], axis=1),
                                     jnp.concatenate([zeros, gla_gate_w_up[j, 1]], axis=1)], axis=0)
            bgate = gla_gate_b[j].reshape(1, 2 * B_K)
            s_bound = A_HEAD_DIM ** 0.5 * jnp.max(jnp.abs(attn_q_gain[j])) * jnp.max(jnp.abs(attn_k_gain[j]))
            qaug = jnp.zeros((1, LANES), F32).at[0, A_HEAD_DIM].set(-s_bound * LOG2E)
            q, k, v, qb, kb, vb, gb, laf, lab = even_in_proj(
                xx, mods, gain_mix, wq, wkv, wb, wlr, wgate, bgate,
                pad_gain(attn_q_gain[j]), pad_gain(attn_k_gain[j]), qaug, cos, sin, n_ctx_tiles)
            a = attention(s_bound.reshape(1), q, k, v, n_ctx)
            o_f, o_r = gla_bidir(qb, kb, vb, laf, lab, n_ctx_tiles)
            w_out = even_w_out[j]
            wa = jnp.pad(w_out[:o0].reshape(A_HEADS, A_HEAD_DIM, d),
                         ((0, 0), (0, LANES - A_HEAD_DIM), (0, 0))).reshape(A_HEADS * LANES, d).astype(BF16)
            xx = even_out_proj(xx, mods, a, o_f, o_r, gb, gla_out_gain[j].reshape(1, B_DV), wa,
                               w_out[o0:].astype(BF16), n_ctx_tiles)
        else:
            w_in = lru_w_in[j]
            gate, rec = odd_in_proj(xx, mods, gain_mix, w_in[:, :LRU_WIDTH].astype(BF16),
                                    w_in[:, LRU_WIDTH:].astype(BF16), n_ctx_tiles)
            wax = jnp.concatenate([lru_w_a[j], lru_w_x[j]], axis=-1).astype(BF16)
            bax = jnp.concatenate([lru_b_a[j].reshape(2, LRU_HEADS, 1, LRU_HEAD_DIM),
                                   lru_b_x[j].reshape(2, LRU_HEADS, 1, LRU_HEAD_DIM)], axis=-1)
            hf, hr = lru_scan(rec.reshape(s, bsz, LRU_WIDTH), lru_conv[j], wax, bax,
                              lru_lambda[j].reshape(2, 1, LRU_WIDTH), n_ctx)
            xx = odd_out_proj(xx, mods, gate, hf.reshape(s, bsz * LRU_WIDTH), hr.reshape(s, bsz * LRU_WIDTH),
                              lru_w_out[j].astype(BF16), n_ctx_tiles)

        n_chunk = d_ff // FFN_CN
        wu = ffn_w_up[l].reshape(d, 2 * n_chunk, FFN_CN).transpose(1, 0, 2).astype(BF16)
        cw = ffn_conv[l].reshape(-1, 2 * n_chunk, FFN_CN).transpose(1, 0, 2)
        wd = ffn_w_down[l].reshape(n_chunk, FFN_CN, d).astype(BF16)
        xx = conv_ffn(xx, mods, norm_ffn[l].reshape(1, d), wu, cw, wd, final_gain.reshape(1, d),
                      n_ctx_tiles, latents_only=last, final_norm=last)
    return xx
```

```python
import functools

import numpy as np
import jax
import jax.numpy as jnp
from jax import lax
from jax.experimental import pallas as pl
from jax.experimental.pallas import tpu as pltpu

F32 = jnp.float32
BF16 = jnp.bfloat16

NORM_EPS = 1e-6
N_MOD = 6
GRID_W = 64
ROPE_THETA = 10000.0

A_HEADS = 8
A_KV_HEADS = 2
A_GROUP = A_HEADS // A_KV_HEADS
A_HEAD_DIM = 64

B_HEADS = 4
B_DK = 64
B_DV = 128
B_K = B_HEADS * B_DK
B_V = B_HEADS * B_DV
B_GATE_RANK = 16
B_GATE_TAU = 16.0
GLA_CHUNK = 64
GLA_SUB = 16

LRU_HEADS = 10
LRU_HEAD_DIM = 128
LRU_WIDTH = LRU_HEADS * LRU_HEAD_DIM
LRU_C = 8.0

LANES = 128
ROW_TILE = 256
ATT_TQ = 128
ATT_TK = 256
FFN_CN = 256
FFN_HALO = 8
LRU_TT = 32
VMEM_LIMIT = 56 * 1024 * 1024

LOG2E = float(np.log2(np.e))
ATT_MAX_STATIC_SHIFT = 40.0

NT_DIMS = (((1,), (1,)), ((), ()))
TN_DIMS = (((0,), (0,)), ((), ()))


def _params(*sem):
    return pltpu.CompilerParams(dimension_semantics=sem, vmem_limit_bytes=VMEM_LIMIT)


def _resident(shape):
    nd = len(shape)
    return pl.BlockSpec(shape, lambda *_: (0,) * nd, pipeline_mode=pl.Buffered(1))


def _dot(a, b):
    return jnp.dot(a, b, preferred_element_type=F32)


def _split_bf16(a):
    hi = a.astype(BF16)
    lo = (a - hi.astype(F32)).astype(BF16)
    return hi, lo


def _dot_f32(a, b):
    ah, al = _split_bf16(a)
    bh, bl = _split_bf16(b)
    return _dot(ah, bh) + _dot(ah, bl) + _dot(al, bh)


def _sigmoid(x):
    return 0.5 * jnp.tanh(0.5 * x) + 0.5


def _silu(x):
    return x * _sigmoid(x)


def _softplus(x):
    return jnp.maximum(x, 0.0) + jnp.log1p(jnp.exp(-jnp.abs(x)))


def _norm_mod(x, gain, shift, scale):
    ms = jnp.mean(x * x, axis=-1, keepdims=True)
    return (x * lax.rsqrt(ms + NORM_EPS) * gain) * (1.0 + scale) + shift


def _adaln_kernel(c_ref, w_ref, b_ref, o_ref):
    o_ref[...] = _dot_f32(_silu(c_ref[...]), w_ref[...]) + b_ref[...]


def adaln_table(cond, w, b):
    rows, d = cond.shape
    n = w.shape[1]
    tn = 768
    return pl.pallas_call(
        _adaln_kernel,
        out_shape=jax.ShapeDtypeStruct((rows, n), F32),
        grid=(n // tn,),
        in_specs=[pl.BlockSpec((rows, d), lambda j: (0, 0)),
                  pl.BlockSpec((d, tn), lambda j: (0, j)),
                  pl.BlockSpec((1, tn), lambda j: (0, j))],
        out_specs=pl.BlockSpec((rows, tn), lambda j: (0, j)),
        compiler_params=_params("arbitrary"),
        name="adaln_table",
    )(cond, w, b.reshape(1, n))


def _mods_spec(n_ctx_tiles, d):
    return pl.BlockSpec((None, None, N_MOD, d),
                        lambda b, i: (b, (i >= n_ctx_tiles).astype(jnp.int32), 0, 0))


def _even_in_kernel(x_ref, mod_ref, gain_ref, wq_ref, wkv_ref, wb_ref, wlr_ref, wgate_ref, bgate_ref,
                    qg_ref, kg_ref, qaug_ref, cos_ref, sin_ref,
                    q_ref, k_ref, v_ref, qb_ref, kb_ref, vb_ref, gb_ref, laf_ref, lab_ref):
    h = _norm_mod(x_ref[...], gain_ref[...], mod_ref[0:1, :], mod_ref[1:2, :]).astype(BF16)
    cos = cos_ref[...]
    sin = sin_ref[...]
    lane = lax.broadcasted_iota(jnp.int32, cos.shape, 1)
    first_half = (lane % 32) < 16
    k_aug = (lane == A_HEAD_DIM).astype(F32)

    def head_norm_rope(xh, g):
        ms = jnp.sum(xh * xh, axis=-1, keepdims=True) * (1.0 / A_HEAD_DIM)
        y = xh * lax.rsqrt(ms + NORM_EPS) * g
        swapped = jnp.where(first_half, pltpu.roll(y, LANES - 16, 1), pltpu.roll(y, 16, 1))
        return y * cos + swapped * sin

    qp = _dot(h, wq_ref[...])
    for hh in range(A_HEADS):
        qh = head_norm_rope(qp[:, hh * LANES:(hh + 1) * LANES], qg_ref[...])
        q_ref[hh] = (qh * (A_HEAD_DIM ** -0.5 * LOG2E) + qaug_ref[...]).astype(BF16)
    kvp = _dot(h, wkv_ref[...])
    for hh in range(A_KV_HEADS):
        k_ref[hh] = (head_norm_rope(kvp[:, hh * LANES:(hh + 1) * LANES], kg_ref[...]) + k_aug).astype(BF16)
        vh = kvp[:, (A_KV_HEADS + hh) * LANES:(A_KV_HEADS + hh + 1) * LANES]
        v_ref[hh] = jnp.where(lane >= A_HEAD_DIM, 1.0, vh).astype(BF16)

    pb = _dot(h, wb_ref[...])
    qb_ref[...] = pb[:, 0:B_K] * B_DK ** -0.5
    kb_ref[...] = pb[:, B_K:2 * B_K]
    vb_ref[...] = pb[:, 2 * B_K:2 * B_K + B_V]
    gb_ref[...] = pb[:, 2 * B_K + B_V:2 * B_K + 2 * B_V].astype(BF16)

    lr = _dot(h, wlr_ref[...])
    z = _dot_f32(lr, wgate_ref[...]) + bgate_ref[...]
    la = -_softplus(-z) * (1.0 / B_GATE_TAU)
    laf_ref[...] = la[:, 0:B_K]
    lab_ref[...] = la[:, B_K:2 * B_K]


def even_in_proj(xx, mods, gain, wq, wkv, wb, wlr, wgate, bgate, qg, kg, qaug, cos, sin, n_ctx_tiles):
    bsz, s, d = xx.shape
    t = ROW_TILE
    row = lambda w: pl.BlockSpec((None, t, w), lambda b, i: (b, i, 0))
    heads = lambda n: pl.BlockSpec((None, n, t, LANES), lambda b, i: (b, 0, i, 0))
    tab = pl.BlockSpec((t, LANES), lambda b, i: (i, 0))
    out_shape = [
        jax.ShapeDtypeStruct((bsz, A_HEADS, s, LANES), BF16),
        jax.ShapeDtypeStruct((bsz, A_KV_HEADS, s, LANES), BF16),
        jax.ShapeDtypeStruct((bsz, A_KV_HEADS, s, LANES), BF16),
        jax.ShapeDtypeStruct((bsz, s, B_K), F32),
        jax.ShapeDtypeStruct((bsz, s, B_K), F32),
        jax.ShapeDtypeStruct((bsz, s, B_V), F32),
        jax.ShapeDtypeStruct((bsz, s, B_V), BF16),
        jax.ShapeDtypeStruct((bsz, s, B_K), F32),
        jax.ShapeDtypeStruct((bsz, s, B_K), F32),
    ]
    return pl.pallas_call(
        _even_in_kernel,
        out_shape=out_shape,
        grid=(bsz, s // t),
        in_specs=[row(d), _mods_spec(n_ctx_tiles, d), _resident(gain.shape),
                  _resident(wq.shape), _resident(wkv.shape), _resident(wb.shape), _resident(wlr.shape),
                  _resident(wgate.shape), _resident(bgate.shape), _resident(qg.shape), _resident(kg.shape),
                  _resident(qaug.shape), tab, tab],
        out_specs=[heads(A_HEADS), heads(A_KV_HEADS), heads(A_KV_HEADS),
                   row(B_K), row(B_K), row(B_V), row(B_V), row(B_K), row(B_K)],
        compiler_params=_params("parallel", "parallel"),
        name="even_in_proj",
    )(xx, mods, gain, wq, wkv, wb, wlr, wgate, bgate, qg, kg, qaug, cos, sin)


def _attn_kernel(shift_ref, q_ref, k_ref, v_ref, o_ref, s_ref, p_ref, m_ref, *, n_ctx_q_tiles, n_ctx_rows):
    i = pl.program_id(2)
    tq = q_ref.shape[1]
    rows = A_GROUP * tq
    n_keys = k_ref.shape[0]

    def finish(acc):
        out = acc / acc[:, A_HEAD_DIM:A_HEAD_DIM + 1]
        for j in range(A_GROUP):
            o_ref[:, j * LANES:(j + 1) * LANES] = out[j * tq:(j + 1) * tq].astype(BF16)

    def attend_static_shift(nk):
        qs = q_ref[...].reshape(rows, LANES)
        acc = jnp.zeros((rows, LANES), F32)
        for c0 in range(0, nk, ATT_TK):
            s = lax.dot_general(qs, k_ref[c0:c0 + ATT_TK, :], NT_DIMS, preferred_element_type=F32)
            acc = acc + _dot(jnp.exp2(s).astype(BF16), v_ref[c0:c0 + ATT_TK, :])
        finish(acc)

    def attend_row_max(nk):
        qs = q_ref[...].reshape(rows, LANES)
        m_ref[...] = jnp.full(m_ref.shape, -jnp.inf, F32)
        for c0 in range(0, nk, ATT_TK):
            cols = slice(c0, c0 + ATT_TK)
            s = lax.dot_general(qs, k_ref[cols, :], NT_DIMS, preferred_element_type=F32)
            s_ref[:, cols] = s
            m_ref[...] = jnp.maximum(m_ref[...], jnp.maximum(s[:, 0:LANES], s[:, LANES:2 * LANES]))
        m = jnp.max(m_ref[...], axis=1, keepdims=True)
        for c0 in range(0, nk, ATT_TK):
            cols = slice(c0, c0 + ATT_TK)
            p_ref[:, cols] = jnp.exp2(s_ref[:, cols] - m).astype(BF16)
        finish(_dot(p_ref[:, 0:nk], v_ref[0:nk, :]))

    is_ctx = i < n_ctx_q_tiles
    static_ok = shift_ref[0] <= ATT_MAX_STATIC_SHIFT
    for ctx_tile, nk in ((True, n_ctx_rows), (False, n_keys)):
        tile_match = is_ctx if ctx_tile else jnp.logical_not(is_ctx)

        @pl.when(jnp.logical_and(tile_match, static_ok))
        def _():
            attend_static_shift(nk)

        @pl.when(jnp.logical_and(tile_match, jnp.logical_not(static_ok)))
        def _():
            attend_row_max(nk)


def attention(shift, q, k, v, n_ctx_rows):
    bsz, _, s, _ = q.shape
    tq = ATT_TQ
    assert ATT_TK == 2 * LANES and n_ctx_rows % ATT_TK == 0 and s % ATT_TK == 0
    kernel = functools.partial(_attn_kernel, n_ctx_q_tiles=n_ctx_rows // tq, n_ctx_rows=n_ctx_rows)
    kv_spec = pl.BlockSpec((None, None, s, LANES), lambda b, g, i: (b, g, 0, 0))
    return pl.pallas_call(
        kernel,
        out_shape=jax.ShapeDtypeStruct((bsz, s, A_HEADS * LANES), BF16),
        grid=(bsz, A_KV_HEADS, s // tq),
        in_specs=[pl.BlockSpec(memory_space=pltpu.SMEM),
                  pl.BlockSpec((None, A_GROUP, tq, LANES), lambda b, g, i: (b, g, i, 0)), kv_spec, kv_spec],
        out_specs=pl.BlockSpec((None, tq, A_GROUP * LANES), lambda b, g, i: (b, i, g)),
        scratch_shapes=[pltpu.VMEM((A_GROUP * tq, s), F32),
                        pltpu.VMEM((A_GROUP * tq, s), BF16),
                        pltpu.VMEM((A_GROUP * tq, LANES), F32)],
        compiler_params=_params("parallel", "parallel", "arbitrary"),
        name="gqa_attention",
    )(shift, q, k, v)


def _gla_consts(reverse):
    c, sub = GLA_CHUNK, GLA_SUB
    t = lax.broadcasted_iota(jnp.int32, (c, c), 0)
    s = lax.broadcasted_iota(jnp.int32, (c, c), 1)
    if reverse:
        tri = s >= t
        blk = s >= (t // sub + 1) * sub
    else:
        tri = s <= t
        blk = s < (t // sub) * sub
    sums = jnp.concatenate([tri, blk], axis=0).astype(BF16)
    n = B_HEADS * c
    rt = lax.broadcasted_iota(jnp.int32, (n, n), 0) % c
    cs = lax.broadcasted_iota(jnp.int32, (n, n), 1) % c
    causal = (cs >= rt) if reverse else (cs <= rt)
    lane = lax.broadcasted_iota(jnp.int32, (1, B_K), 1)
    head_masks = [(lane // B_DK == hh).astype(F32) for hh in range(B_HEADS)]
    row_sub = lax.broadcasted_iota(jnp.int32, (n, 1), 0) % c // sub
    key_row = lax.broadcasted_iota(jnp.int32, (c, 1), 0)
    return sums, causal, head_masks, row_sub, key_row


def _gla_chunk(q, k, v, la, st, consts, reverse):
    sums, causal, head_masks, row_sub, key_row = consts
    c, sub = GLA_CHUNK, GLA_SUB

    def stack_heads(x):
        return jnp.concatenate([x * hm for hm in head_masks], axis=0)

    la_hi, la_lo = _split_bf16(la)
    cr = _dot(sums, la_hi) + _dot(sums, la_lo)
    cum = cr[0:c]
    ref = cr[c:2 * c]
    edge = 0 if reverse else c - 1
    total = cum[edge:edge + 1]

    q_in = stack_heads(q * jnp.exp(cum)).astype(BF16)
    q_loc = stack_heads(q * jnp.exp(cum - ref)).astype(BF16)
    k_out = stack_heads(k * jnp.exp(total - cum)).astype(BF16)

    scores = jnp.zeros((B_HEADS * c, B_HEADS * c), F32)
    for i in range(c // sub):
        ref_i = ref[i * sub:i * sub + 1]
        valid = (key_row >= i * sub) if reverse else (key_row < (i + 1) * sub)
        k_i = stack_heads(k * jnp.exp(jnp.where(valid, ref_i - cum, -jnp.inf))).astype(BF16)
        s_i = lax.dot_general(q_loc, k_i, NT_DIMS, preferred_element_type=F32)
        scores = jnp.where(row_sub == i, s_i, scores)
    scores = jnp.where(causal, scores, 0.0).astype(BF16)

    v_st = jnp.concatenate([v[:, hh * B_DV:(hh + 1) * B_DV] for hh in range(B_HEADS)], axis=0).astype(BF16)
    o = _dot(scores, v_st) + lax.dot_general(q_in, st.astype(BF16), NT_DIMS, preferred_element_type=F32)
    st_new = st * jnp.exp(total) + lax.dot_general(v_st, k_out, TN_DIMS, preferred_element_type=F32)
    return o, st_new


def _gla_kernel(qf_ref, kf_ref, vf_ref, laf_ref, qr_ref, kr_ref, vr_ref, lar_ref,
                of_ref, or_ref, stf_ref, str_ref):
    @pl.when(pl.program_id(1) == 0)
    def _():
        stf_ref[...] = jnp.zeros_like(stf_ref)
        str_ref[...] = jnp.zeros_like(str_ref)

    c = GLA_CHUNK
    n_chunks = qf_ref.shape[0] // c
    for reverse, (q_ref, k_ref, v_ref, la_ref, o_ref, st_ref) in (
            (False, (qf_ref, kf_ref, vf_ref, laf_ref, of_ref, stf_ref)),
            (True, (qr_ref, kr_ref, vr_ref, lar_ref, or_ref, str_ref))):
        consts = _gla_consts(reverse)
        st = st_ref[...]
        order = range(n_chunks - 1, -1, -1) if reverse else range(n_chunks)
        for n in order:
            rows = slice(n * c, (n + 1) * c)
            o, st = _gla_chunk(q_ref[rows, :], k_ref[rows, :], v_ref[rows, :], la_ref[rows, :], st, consts, reverse)
            for hh in range(B_HEADS):
                o_ref[rows, hh * B_DV:(hh + 1) * B_DV] = o[hh * c:(hh + 1) * c].astype(BF16)
        st_ref[...] = st


def gla_bidir(qb, kb, vb, laf, lab, n_ctx_tiles):
    bsz, s, _ = qb.shape
    t = ROW_TILE
    nt = s // t

    def rev_tile(j):
        return jnp.where(j < n_ctx_tiles, n_ctx_tiles - 1 - j, nt - 1 - (j - n_ctx_tiles))

    fwd = lambda w: pl.BlockSpec((None, t, w), lambda b, j: (b, j, 0))
    rev = lambda w: pl.BlockSpec((None, t, w), lambda b, j: (b, rev_tile(j), 0))
    return pl.pallas_call(
        _gla_kernel,
        out_shape=[jax.ShapeDtypeStruct((bsz, s, B_V), BF16)] * 2,
        grid=(bsz, nt),
        in_specs=[fwd(B_K), fwd(B_K), fwd(B_V), fwd(B_K), rev(B_K), rev(B_K), rev(B_V), rev(B_K)],
        out_specs=[fwd(B_V), rev(B_V)],
        scratch_shapes=[pltpu.VMEM((B_DV, B_K), F32)] * 2,
        compiler_params=_params("parallel", "arbitrary"),
        name="gla_bidir",
    )(qb, kb, vb, laf, qb, kb, vb, lab)


def _even_out_kernel(x_ref, mod_ref, a_ref, of_ref, or_ref, g_ref, og_ref, wa_ref, wb_ref, o_ref):
    o = of_ref[...].astype(F32) + or_ref[...].astype(F32)
    g = g_ref[...].astype(F32)
    parts = []
    for hh in range(B_HEADS):
        oh = o[:, hh * B_DV:(hh + 1) * B_DV]
        ms = jnp.mean(oh * oh, axis=-1, keepdims=True)
        y = oh * lax.rsqrt(ms + NORM_EPS) * og_ref[...]
        parts.append((y * _silu(g[:, hh * B_DV:(hh + 1) * B_DV])).astype(BF16))
    gla = jnp.concatenate(parts, axis=1)
    y = _dot(a_ref[...], wa_ref[...]) + _dot(gla, wb_ref[...])
    o_ref[...] = x_ref[...] + mod_ref[2:3, :] * y


def even_out_proj(xx, mods, a, o_f, o_r, gb, o_gain, wa, wb, n_ctx_tiles):
    bsz, s, d = xx.shape
    t = ROW_TILE
    row = lambda w: pl.BlockSpec((None, t, w), lambda b, i: (b, i, 0))
    return pl.pallas_call(
        _even_out_kernel,
        out_shape=jax.ShapeDtypeStruct((bsz, s, d), F32),
        grid=(bsz, s // t),
        in_specs=[row(d), _mods_spec(n_ctx_tiles, d), row(a.shape[-1]), row(B_V), row(B_V), row(B_V),
                  _resident(o_gain.shape), _resident(wa.shape), _resident(wb.shape)],
        out_specs=row(d),
        compiler_params=_params("parallel", "parallel"),
        name="even_out_proj",
    )(xx, mods, a, o_f, o_r, gb, o_gain, wa, wb)


def _ffn_kernel(x_ref, xp_ref, xn_ref, mod_ref, gain_ref, wu_ref, cw_ref, wd_ref, fg_ref, o_ref,
                h_ref, u_ref, act_ref, *, tile0, n_ctx_tiles, n_tiles, final_norm):
    ti = pl.program_id(1) + tile0
    t = x_ref.shape[0]
    d_ff = wd_ref.shape[0]
    gain = gain_ref[...]
    shift = mod_ref[3:4, :]
    scale = mod_ref[4:5, :]
    x = x_ref[...]
    has_prev = jnp.logical_and(ti != 0, ti != n_ctx_tiles)
    has_next = jnp.logical_and(ti != n_ctx_tiles - 1, ti != n_tiles - 1)
    hp = jnp.where(has_prev, _norm_mod(xp_ref[...], gain, shift, scale), 0.0)
    hn = jnp.where(has_next, _norm_mod(xn_ref[...], gain, shift, scale), 0.0)
    h_ref[...] = jnp.concatenate([hp, _norm_mod(x, gain, shift, scale), hn], axis=0).astype(BF16)
    lo = FFN_HALO - 1

    def conv(cols):
        cw = cw_ref[:, cols]
        return (cw[0:1] * u_ref[lo:lo + t, cols] + cw[1:2] * u_ref[lo + 1:lo + 1 + t, cols]
                + cw[2:3] * u_ref[lo + 2:lo + 2 + t, cols])

    for c0 in range(0, d_ff, FFN_CN):
        gate_cols = slice(c0, c0 + FFN_CN)
        val_cols = slice(d_ff + c0, d_ff + c0 + FFN_CN)
        h = h_ref[...]
        u_ref[:, gate_cols] = _dot(h, wu_ref[:, gate_cols])
        u_ref[:, val_cols] = _dot(h, wu_ref[:, val_cols])
        act_ref[:, gate_cols] = (_silu(conv(gate_cols)) * conv(val_cols)).astype(BF16)
    y = x + mod_ref[5:6, :] * _dot(act_ref[...], wd_ref[...])
    if final_norm:
        ms = jnp.mean(y * y, axis=-1, keepdims=True)
        y = y * lax.rsqrt(ms + NORM_EPS) * fg_ref[...]
    o_ref[...] = y


def conv_ffn(xx, mods, gain, wu, cw, wd, final_gain, n_ctx_tiles, latents_only, final_norm):
    bsz, s, d = xx.shape
    t = ROW_TILE
    nt = s // t
    tile0 = n_ctx_tiles if latents_only else 0
    hb = t // FFN_HALO
    last_hb = s // FFN_HALO - 1
    kernel = functools.partial(_ffn_kernel, tile0=tile0, n_ctx_tiles=n_ctx_tiles, n_tiles=nt, final_norm=final_norm)
    return pl.pallas_call(
        kernel,
        out_shape=jax.ShapeDtypeStruct((bsz, s - tile0 * t, d), F32),
        grid=(bsz, nt - tile0),
        in_specs=[pl.BlockSpec((None, t, d), lambda b, i: (b, i + tile0, 0)),
                  pl.BlockSpec((None, FFN_HALO, d), lambda b, i: (b, jnp.maximum((i + tile0) * hb - 1, 0), 0)),
                  pl.BlockSpec((None, FFN_HALO, d), lambda b, i: (b, jnp.minimum((i + tile0 + 1) * hb, last_hb), 0)),
                  pl.BlockSpec((None, None, N_MOD, d),
                               lambda b, i: (b, (i + tile0 >= n_ctx_tiles).astype(jnp.int32), 0, 0)),
                  _resident(gain.shape), _resident(wu.shape), _resident(cw.shape), _resident(wd.shape),
                  _resident(final_gain.shape)],
        out_specs=pl.BlockSpec((None, t, d), lambda b, i: (b, i, 0)),
        scratch_shapes=[pltpu.VMEM((t + 2 * FFN_HALO, d), BF16),
                        pltpu.VMEM((t + 2 * FFN_HALO, wu.shape[1]), F32),
                        pltpu.VMEM((t, wd.shape[0]), BF16)],
        compiler_params=_params("parallel", "parallel"),
        name="conv_ffn_final" if final_norm else "conv_ffn",
    )(xx, xx, xx, mods, gain, wu, cw, wd, final_gain)


def _odd_in_kernel(x_ref, mod_ref, gain_ref, wg_ref, wr_ref, gate_ref, rec_ref):
    h = _norm_mod(x_ref[...], gain_ref[...], mod_ref[0:1, :], mod_ref[1:2, :]).astype(BF16)
    gate_ref[...] = jax.nn.gelu(_dot(h, wg_ref[...]), approximate=True).astype(BF16)
    rec_ref[...] = _dot(h, wr_ref[...])


def odd_in_proj(xx, mods, gain, wg, wr, n_ctx_tiles):
    bsz, s, d = xx.shape
    t = ROW_TILE
    w = wg.shape[1]
    return pl.pallas_call(
        _odd_in_kernel,
        out_shape=[jax.ShapeDtypeStruct((bsz, s, w), BF16), jax.ShapeDtypeStruct((s, bsz * w), F32)],
        grid=(bsz, s // t),
        in_specs=[pl.BlockSpec((None, t, d), lambda b, i: (b, i, 0)), _mods_spec(n_ctx_tiles, d),
                  _resident(gain.shape), _resident(wg.shape), _resident(wr.shape)],
        out_specs=[pl.BlockSpec((None, t, w), lambda b, i: (b, i, 0)),
                   pl.BlockSpec((t, w), lambda b, i: (i, b))],
        compiler_params=_params("parallel", "parallel"),
        name="odd_in_proj",
    )(xx, mods, gain, wg, wr)


def _lru_kernel(uf_ref, ufp_ref, ufn_ref, ur_ref, urp_ref, urn_ref, cw_ref, wax_ref, bax_ref, lam_ref,
                hf_ref, hr_ref, ue_ref, a_ref, x_ref, hst_ref, *, n_ctx_blocks, n_blocks):
    j = pl.program_id(0)
    tt, bsz, w = uf_ref.shape

    @pl.when(j == 0)
    def _():
        hst_ref[...] = jnp.zeros_like(hst_ref)

    rev_blk = jnp.where(j < n_ctx_blocks, n_ctx_blocks - 1 - j, n_blocks - 1 - (j - n_ctx_blocks))
    for d, (blk, u_ref, up_ref, un_ref, o_ref) in enumerate(
            ((j, uf_ref, ufp_ref, ufn_ref, hf_ref), (rev_blk, ur_ref, urp_ref, urn_ref, hr_ref))):
        has_prev = jnp.logical_and(blk != 0, blk != n_ctx_blocks)
        has_next = jnp.logical_and(blk != n_ctx_blocks - 1, blk != n_blocks - 1)
        ue_ref[0:1] = jnp.where(has_prev, up_ref[...], 0.0)
        ue_ref[1:tt + 1] = u_ref[...]
        ue_ref[tt + 1:tt + 3] = jnp.where(has_next, un_ref[...], 0.0)
        cw = cw_ref[...]
        uc = cw[0:1] * ue_ref[0:tt] + cw[1:2] * ue_ref[1:tt + 1] + cw[2:3] * ue_ref[2:tt + 2] + cw[3:4] * ue_ref[3:tt + 3]
        uc = uc.reshape(tt * bsz, w)
        half_c = (-0.5 * LRU_C) * _softplus(-lam_ref[d])
        for hh in range(LRU_HEADS):
            cols = slice(hh * LRU_HEAD_DIM, (hh + 1) * LRU_HEAD_DIM)
            uh = uc[:, cols]
            th = jnp.tanh(_dot(uh.astype(BF16), wax_ref[d, hh]) + bax_ref[d, hh])
            log_a = th[:, 0:LRU_HEAD_DIM] * half_c[:, cols] + half_c[:, cols]
            ig = 0.5 * th[:, LRU_HEAD_DIM:2 * LRU_HEAD_DIM] + 0.5
            a = jnp.exp(log_a)
            m2 = jnp.tanh(log_a) * (-1.0 - a * a)
            mult = jnp.where(m2 > 0.0, m2 * lax.rsqrt(m2), 0.0)
            a_ref[:, :, cols] = a.reshape(tt, bsz, LRU_HEAD_DIM)
            x_ref[:, :, cols] = (mult * ig * uh).reshape(tt, bsz, LRU_HEAD_DIM)
        h = hst_ref[d]
        for step in range(tt):
            tcur = tt - 1 - step if d == 1 else step
            h = a_ref[tcur] * h + x_ref[tcur]
            o_ref[tcur] = h
        hst_ref[d] = h


def lru_scan(u, cw, wax, bax, lam, n_ctx_rows):
    s, bsz, w = u.shape
    tt = LRU_TT
    nb = s // tt
    ncb = n_ctx_rows // tt

    def rev_blk(j):
        return jnp.where(j < ncb, ncb - 1 - j, nb - 1 - (j - ncb))

    def specs(blk):
        return [pl.BlockSpec((tt, bsz, w), lambda j: (blk(j), 0, 0)),
                pl.BlockSpec((1, bsz, w), lambda j: (jnp.maximum(blk(j) * tt - 1, 0), 0, 0)),
                pl.BlockSpec((2, bsz, w), lambda j: (jnp.minimum((blk(j) + 1) * (tt // 2), s // 2 - 1), 0, 0))]

    fwd_blk = lambda j: j
    kernel = functools.partial(_lru_kernel, n_ctx_blocks=ncb, n_blocks=nb)
    return pl.pallas_call(
        kernel,
        out_shape=[jax.ShapeDtypeStruct((s, bsz, w), F32)] * 2,
        grid=(nb,),
        in_specs=specs(fwd_blk) + specs(rev_blk) + [_resident(cw.shape), _resident(wax.shape),
                                                     _resident(bax.shape), _resident(lam.shape)],
        out_specs=[pl.BlockSpec((tt, bsz, w), lambda j: (j, 0, 0)),
                   pl.BlockSpec((tt, bsz, w), lambda j: (rev_blk(j), 0, 0))],
        scratch_shapes=[pltpu.VMEM((tt + 3, bsz, w), F32), pltpu.VMEM((tt, bsz, w), F32),
                        pltpu.VMEM((tt, bsz, w), F32), pltpu.VMEM((2, bsz, w), F32)],
        compiler_params=_params("arbitrary"),
        name="lru_scan",
    )(u, u, u, u, u, u, cw, wax, bax, lam)


def _odd_out_kernel(x_ref, mod_ref, gate_ref, hf_ref, hr_ref, w_ref, o_ref):
    mixed = (gate_ref[...] * (hf_ref[...] + hr_ref[...])).astype(BF16)
    o_ref[...] = x_ref[...] + mod_ref[2:3, :] * _dot(mixed, w_ref[...])


def odd_out_proj(xx, mods, gate, hf, hr, w_out, n_ctx_tiles):
    bsz, s, d = xx.shape
    t = ROW_TILE
    w = gate.shape[-1]
    row = lambda n: pl.BlockSpec((None, t, n), lambda b, i: (b, i, 0))
    tmaj = pl.BlockSpec((t, w), lambda b, i: (i, b))
    return pl.pallas_call(
        _odd_out_kernel,
        out_shape=jax.ShapeDtypeStruct((bsz, s, d), F32),
        grid=(bsz, s // t),
        in_specs=[row(d), _mods_spec(n_ctx_tiles, d), row(w), tmaj, tmaj, _resident(w_out.shape)],
        out_specs=row(d),
        compiler_params=_params("parallel", "parallel"),
        name="odd_out_proj",
    )(xx, mods, gate, hf, hr, w_out)


def _pad_heads(w, n_heads, head_dim):
    d = w.shape[0]
    w = w.reshape(d, n_heads, head_dim)
    return jnp.pad(w, ((0, 0), (0, 0), (0, LANES - head_dim))).reshape(d, n_heads * LANES)


def _rope_tables(n_ctx, n_lat):
    rows = n_lat // GRID_W
    row = jnp.repeat(jnp.arange(rows, dtype=F32), GRID_W)
    col = jnp.tile(jnp.arange(GRID_W, dtype=F32), rows)
    n_freq = A_HEAD_DIM // 4
    inv_freq = ROPE_THETA ** (-jnp.arange(n_freq, dtype=F32) / n_freq)
    ar = row[:, None] * inv_freq
    ac = col[:, None] * inv_freq
    cos = jnp.concatenate([jnp.cos(ar), jnp.cos(ar), jnp.cos(ac), jnp.cos(ac)], axis=-1)
    sin = jnp.concatenate([-jnp.sin(ar), jnp.sin(ar), -jnp.sin(ac), jnp.sin(ac)], axis=-1)
    cos = jnp.concatenate([jnp.ones((n_ctx, A_HEAD_DIM), F32), cos], axis=0)
    sin = jnp.concatenate([jnp.zeros((n_ctx, A_HEAD_DIM), F32), sin], axis=0)
    return jnp.tile(cos, (1, 2)), jnp.tile(sin, (1, 2))


def kernel(x, c, ctx, c_ctx, ada_w, ada_b, norm_mix, norm_ffn, ffn_w_up, ffn_conv, ffn_w_down, even_w_in, even_w_out, attn_q_gain, attn_k_gain, gla_gate_w_up, gla_gate_b, gla_out_gain, lru_w_in, lru_conv, lru_lambda, lru_w_a, lru_b_a, lru_w_x, lru_b_x, lru_w_out, final_gain):
    bsz, n_lat, d = x.shape
    n_ctx = ctx.shape[1]
    depth = ada_w.shape[0]
    d_ff = ffn_w_down.shape[1]
    assert n_ctx % ROW_TILE == 0 and n_lat % ROW_TILE == 0 and d_ff % FFN_CN == 0
    n_ctx_tiles = n_ctx // ROW_TILE
    s = n_ctx + n_lat

    xx = jnp.concatenate([ctx, x], axis=1)
    cond_rows = -(-(bsz + 1) // 8) * 8
    cond = jnp.zeros((cond_rows, d), F32).at[:bsz].set(c).at[bsz].set(c_ctx)
    cos, sin = _rope_tables(n_ctx, n_lat)
    pad_gain = lambda g: jnp.pad(g, (0, LANES - A_HEAD_DIM)).reshape(1, LANES)

    for l in range(depth):
        last = l == depth - 1
        j = l // 2
        table = adaln_table(cond, ada_w[l], ada_b[l])
        m_lat = table[:bsz].reshape(bsz, N_MOD, d)
        m_ctx = jnp.broadcast_to(table[bsz].reshape(1, N_MOD, d), (bsz, N_MOD, d))
        mods = jnp.stack([m_ctx, m_lat], axis=1)
        gain_mix = norm_mix[l].reshape(1, d)

        if l % 2 == 0:
            w_in = even_w_in[j]
            o0 = A_HEADS * A_HEAD_DIM
            o1 = o0 + A_KV_HEADS * A_HEAD_DIM
            o2 = o1 + A_KV_HEADS * A_HEAD_DIM
            o3 = o2 + 2 * B_K + 2 * B_V
            wq = _pad_heads(w_in[:, :o0], A_HEADS, A_HEAD_DIM).astype(BF16)
            wkv = jnp.concatenate([_pad_heads(w_in[:, o0:o1], A_KV_HEADS, A_HEAD_DIM),
                                   _pad_heads(w_in[:, o1:o2], A_KV_HEADS, A_HEAD_DIM)], axis=1).astype(BF16)
            wb = w_in[:, o2:o3].astype(BF16)
            wlr = w_in[:, o3:].astype(BF16)
            zeros = jnp.zeros((B_GATE_RANK, B_K), F32)
            wgate = jnp.concatenate([jnp.concatenate([gla_gate_w_up[j, 0], zeros], axis=1),
                                     jnp.concatenate([zeros, gla_gate_w_up[j, 1]], axis=1)], axis=0)
            bgate = gla_gate_b[j].reshape(1, 2 * B_K)
            s_bound = A_HEAD_DIM ** 0.5 * jnp.max(jnp.abs(attn_q_gain[j])) * jnp.max(jnp.abs(attn_k_gain[j]))
            qaug = jnp.zeros((1, LANES), F32).at[0, A_HEAD_DIM].set(-s_bound * LOG2E)
            q, k, v, qb, kb, vb, gb, laf, lab = even_in_proj(
                xx, mods, gain_mix, wq, wkv, wb, wlr, wgate, bgate,
                pad_gain(attn_q_gain[j]), pad_gain(attn_k_gain[j]), qaug, cos, sin, n_ctx_tiles)
            a = attention(s_bound.reshape(1), q, k, v, n_ctx)
            o_f, o_r = gla_bidir(qb, kb, vb, laf, lab, n_ctx_tiles)
            w_out = even_w_out[j]
            wa = jnp.pad(w_out[:o0].reshape(A_HEADS, A_HEAD_DIM, d),
                         ((0, 0), (0, LANES - A_HEAD_DIM), (0, 0))).reshape(A_HEADS * LANES, d).astype(BF16)
            xx = even_out_proj(xx, mods, a, o_f, o_r, gb, gla_out_gain[j].reshape(1, B_DV), wa,
                               w_out[o0:].astype(BF16), n_ctx_tiles)
        else:
            w_in = lru_w_in[j]
            gate, rec = odd_in_proj(xx, mods, gain_mix, w_in[:, :LRU_WIDTH].astype(BF16),
                                    w_in[:, LRU_WIDTH:].astype(BF16), n_ctx_tiles)
            wax = (0.5 * jnp.concatenate([lru_w_a[j], lru_w_x[j]], axis=-1)).astype(BF16)
            bax = 0.5 * jnp.concatenate([lru_b_a[j].reshape(2, LRU_HEADS, 1, LRU_HEAD_DIM),
                                         lru_b_x[j].reshape(2, LRU_HEADS, 1, LRU_HEAD_DIM)], axis=-1)
            hf, hr = lru_scan(rec.reshape(s, bsz, LRU_WIDTH), lru_conv[j], wax, bax,
                              lru_lambda[j].reshape(2, 1, LRU_WIDTH), n_ctx)
            xx = odd_out_proj(xx, mods, gate, hf.reshape(s, bsz * LRU_WIDTH), hr.reshape(s, bsz * LRU_WIDTH),
                              lru_w_out[j].astype(BF16), n_ctx_tiles)

        xx = conv_ffn(xx, mods, norm_ffn[l].reshape(1, d), ffn_w_up[l].astype(BF16), ffn_conv[l],
                      ffn_w_down[l].astype(BF16), final_gain.reshape(1, d),
                      n_ctx_tiles, latents_only=last, final_norm=last)
    return xx
```

```python
import functools

import numpy as np
import jax
import jax.numpy as jnp
from jax import lax
from jax.experimental import pallas as pl
from jax.experimental.pallas import tpu as pltpu

F32 = jnp.float32
BF16 = jnp.bfloat16

NORM_EPS = 1e-6
N_MOD = 6
GRID_W = 64
ROPE_THETA = 10000.0

A_HEADS = 8
A_KV_HEADS = 2
A_GROUP = A_HEADS // A_KV_HEADS
A_HEAD_DIM = 64

B_HEADS = 4
B_DK = 64
B_DV = 128
B_K = B_HEADS * B_DK
B_V = B_HEADS * B_DV
B_GATE_RANK = 16
B_GATE_TAU = 16.0
GLA_CHUNK = 64
GLA_SUB = 16

LRU_HEADS = 10
LRU_HEAD_DIM = 128
LRU_WIDTH = LRU_HEADS * LRU_HEAD_DIM
LRU_C = 8.0

LANES = 128
ROW_TILE = 256
ATT_TQ = 256
ATT_TK = 256
FFN_CN = 256
FFN_HALO = 8
LRU_TT = 32
TM_PITCH = 40
VMEM_LIMIT = 56 * 1024 * 1024

LOG2E = float(np.log2(np.e))
ATT_MAX_STATIC_SHIFT = 40.0

NT_DIMS = (((1,), (1,)), ((), ()))
TN_DIMS = (((0,), (0,)), ((), ()))


def _params(*sem):
    return pltpu.CompilerParams(dimension_semantics=sem, vmem_limit_bytes=VMEM_LIMIT)


def _resident(shape):
    nd = len(shape)
    return pl.BlockSpec(shape, lambda *_: (0,) * nd, pipeline_mode=pl.Buffered(1))


def _dot(a, b):
    return jnp.dot(a, b, preferred_element_type=F32)


def _split_bf16(a):
    hi = a.astype(BF16)
    lo = (a - hi.astype(F32)).astype(BF16)
    return hi, lo


def _dot_f32(a, b):
    ah, al = _split_bf16(a)
    bh, bl = _split_bf16(b)
    return _dot(ah, bh) + _dot(ah, bl) + _dot(al, bh)


def _sigmoid(x):
    return 0.5 * jnp.tanh(0.5 * x) + 0.5


def _silu(x):
    return x * _sigmoid(x)


def _softplus(x):
    return jnp.maximum(x, 0.0) + jnp.log1p(jnp.exp(-jnp.abs(x)))


def _norm_mod(x, gain, shift, scale):
    ms = jnp.mean(x * x, axis=-1, keepdims=True)
    return (x * lax.rsqrt(ms + NORM_EPS) * gain) * (1.0 + scale) + shift


def _adaln_kernel(c_ref, w_ref, b_ref, o_ref):
    o_ref[...] = _dot_f32(_silu(c_ref[...]), w_ref[...]) + b_ref[...]


def adaln_table(cond, w, b, layer):
    rows, d = cond.shape
    n = w.shape[2]
    tn = 768
    return pl.pallas_call(
        _adaln_kernel,
        out_shape=jax.ShapeDtypeStruct((rows, n), F32),
        grid=(n // tn,),
        in_specs=[pl.BlockSpec((rows, d), lambda j: (0, 0)),
                  pl.BlockSpec((None, d, tn), lambda j: (layer, 0, j)),
                  pl.BlockSpec((None, 1, tn), lambda j: (layer, 0, j))],
        out_specs=pl.BlockSpec((rows, tn), lambda j: (0, j)),
        compiler_params=_params("arbitrary"),
        name="adaln_table",
    )(cond, w, b)


def _mods_spec(n_ctx_tiles, d):
    return pl.BlockSpec((None, None, N_MOD, d),
                        lambda b, i: (b, (i >= n_ctx_tiles).astype(jnp.int32), 0, 0))


def _stream_specs(stream, n_ctx_tiles, t):
    if isinstance(stream, tuple):
        ctx_arr, lat_arr = stream
        lat_off = 0
    else:
        ctx_arr = lat_arr = stream
        lat_off = n_ctx_tiles
    d = ctx_arr.shape[-1]
    ctx_spec = pl.BlockSpec((None, t, d), lambda b, i: (b, jnp.minimum(i, n_ctx_tiles - 1), 0))
    lat_spec = pl.BlockSpec((None, t, d), lambda b, i: (b, jnp.maximum(i - n_ctx_tiles, 0) + lat_off, 0))
    return (ctx_arr, lat_arr), [ctx_spec, lat_spec]


def _stream_tile(xc_ref, xl_ref, n_ctx_tiles):
    return jnp.where(pl.program_id(1) < n_ctx_tiles, xc_ref[...], xl_ref[...])


def _even_in_kernel(xc_ref, xl_ref, mod_ref, gain_ref, wq_ref, wkv_ref, wb_ref, wlr_ref, wgate_ref, bgate_ref,
                    qg_ref, kg_ref, qaug_ref, cos_ref, sin_ref,
                    q_ref, k_ref, v_ref, qb_ref, kb_ref, vb_ref, gb_ref, laf_ref, lab_ref, *, n_ctx_tiles):
    x = _stream_tile(xc_ref, xl_ref, n_ctx_tiles)
    h = _norm_mod(x, gain_ref[...], mod_ref[0:1, :], mod_ref[1:2, :]).astype(BF16)
    cos = cos_ref[...]
    sin = sin_ref[...]
    lane = lax.broadcasted_iota(jnp.int32, cos.shape, 1)
    first_half = (lane % 32) < 16
    k_aug = (lane == A_HEAD_DIM).astype(F32)

    def head_norm_rope(xh, g):
        ms = jnp.sum(xh * xh, axis=-1, keepdims=True) * (1.0 / A_HEAD_DIM)
        y = xh * lax.rsqrt(ms + NORM_EPS) * g
        swapped = jnp.where(first_half, pltpu.roll(y, LANES - 16, 1), pltpu.roll(y, 16, 1))
        return y * cos + swapped * sin

    qp = _dot(h, wq_ref[...])
    for hh in range(A_HEADS):
        qh = head_norm_rope(qp[:, hh * LANES:(hh + 1) * LANES], qg_ref[...])
        q_ref[hh] = (qh * (A_HEAD_DIM ** -0.5 * LOG2E) + qaug_ref[...]).astype(BF16)
    kvp = _dot(h, wkv_ref[...])
    for hh in range(A_KV_HEADS):
        k_ref[hh] = (head_norm_rope(kvp[:, hh * LANES:(hh + 1) * LANES], kg_ref[...]) + k_aug).astype(BF16)
        vh = kvp[:, (A_KV_HEADS + hh) * LANES:(A_KV_HEADS + hh + 1) * LANES]
        v_ref[hh] = jnp.where(lane >= A_HEAD_DIM, 1.0, vh).astype(BF16)

    pb = _dot(h, wb_ref[...])
    qb_ref[...] = pb[:, 0:B_K] * B_DK ** -0.5
    kb_ref[...] = pb[:, B_K:2 * B_K]
    vb_ref[...] = pb[:, 2 * B_K:2 * B_K + B_V]
    gb_ref[...] = pb[:, 2 * B_K + B_V:2 * B_K + 2 * B_V].astype(BF16)

    lr = _dot(h, wlr_ref[...])
    z = _dot_f32(lr, wgate_ref[...]) + bgate_ref[...]
    la = -_softplus(-z) * (1.0 / B_GATE_TAU)
    laf_ref[...] = la[:, 0:B_K]
    lab_ref[...] = la[:, B_K:2 * B_K]


def even_in_proj(stream, mods, gain, wq, wkv, wb, wlr, wgate, bgate, qg, kg, qaug, cos, sin, n_ctx_tiles):
    t = ROW_TILE
    (xc, xl), x_specs = _stream_specs(stream, n_ctx_tiles, t)
    bsz, d = xc.shape[0], xc.shape[-1]
    s = cos.shape[0]
    row = lambda w: pl.BlockSpec((None, t, w), lambda b, i: (b, i, 0))
    heads = lambda n: pl.BlockSpec((None, n, t, LANES), lambda b, i: (b, 0, i, 0))
    tab = pl.BlockSpec((t, LANES), lambda b, i: (i, 0))
    out_shape = [
        jax.ShapeDtypeStruct((bsz, A_HEADS, s, LANES), BF16),
        jax.ShapeDtypeStruct((bsz, A_KV_HEADS, s, LANES), BF16),
        jax.ShapeDtypeStruct((bsz, A_KV_HEADS, s, LANES), BF16),
        jax.ShapeDtypeStruct((bsz, s, B_K), F32),
        jax.ShapeDtypeStruct((bsz, s, B_K), F32),
        jax.ShapeDtypeStruct((bsz, s, B_V), F32),
        jax.ShapeDtypeStruct((bsz, s, B_V), BF16),
        jax.ShapeDtypeStruct((bsz, s, B_K), F32),
        jax.ShapeDtypeStruct((bsz, s, B_K), F32),
    ]
    return pl.pallas_call(
        functools.partial(_even_in_kernel, n_ctx_tiles=n_ctx_tiles),
        out_shape=out_shape,
        grid=(bsz, s // t),
        in_specs=x_specs + [_mods_spec(n_ctx_tiles, d), _resident(gain.shape),
                  _resident(wq.shape), _resident(wkv.shape), _resident(wb.shape), _resident(wlr.shape),
                  _resident(wgate.shape), _resident(bgate.shape), _resident(qg.shape), _resident(kg.shape),
                  _resident(qaug.shape), tab, tab],
        out_specs=[heads(A_HEADS), heads(A_KV_HEADS), heads(A_KV_HEADS),
                   row(B_K), row(B_K), row(B_V), row(B_V), row(B_K), row(B_K)],
        compiler_params=_params("parallel", "parallel"),
        name="even_in_proj",
    )(xc, xl, mods, gain, wq, wkv, wb, wlr, wgate, bgate, qg, kg, qaug, cos, sin)


def _attn_kernel(shift_ref, q_ref, k_ref, v_ref, o_ref, s_ref, p_ref, m_ref, *, n_ctx_q_tiles, n_ctx_rows):
    i = pl.program_id(2)
    tq = q_ref.shape[1]
    rows = A_GROUP * tq
    n_keys = k_ref.shape[0]

    def finish(acc):
        out = acc / acc[:, A_HEAD_DIM:A_HEAD_DIM + 1]
        for j in range(A_GROUP):
            o_ref[:, j * LANES:(j + 1) * LANES] = out[j * tq:(j + 1) * tq].astype(BF16)

    def attend_static_shift(nk):
        qs = q_ref[...].reshape(rows, LANES)
        acc = jnp.zeros((rows, LANES), F32)
        for c0 in range(0, nk, ATT_TK):
            s = lax.dot_general(qs, k_ref[c0:c0 + ATT_TK, :], NT_DIMS, preferred_element_type=F32)
            acc = acc + _dot(jnp.exp2(s).astype(BF16), v_ref[c0:c0 + ATT_TK, :])
        finish(acc)

    def attend_row_max(nk):
        qs = q_ref[...].reshape(rows, LANES)
        m_ref[...] = jnp.full(m_ref.shape, -jnp.inf, F32)
        for c0 in range(0, nk, ATT_TK):
            cols = slice(c0, c0 + ATT_TK)
            s = lax.dot_general(qs, k_ref[cols, :], NT_DIMS, preferred_element_type=F32)
            s_ref[:, cols] = s
            m_ref[...] = jnp.maximum(m_ref[...], jnp.maximum(s[:, 0:LANES], s[:, LANES:2 * LANES]))
        m = jnp.max(m_ref[...], axis=1, keepdims=True)
        for c0 in range(0, nk, ATT_TK):
            cols = slice(c0, c0 + ATT_TK)
            p_ref[:, cols] = jnp.exp2(s_ref[:, cols] - m).astype(BF16)
        finish(_dot(p_ref[:, 0:nk], v_ref[0:nk, :]))

    is_ctx = i < n_ctx_q_tiles
    static_ok = shift_ref[0] <= ATT_MAX_STATIC_SHIFT
    for ctx_tile, nk in ((True, n_ctx_rows), (False, n_keys)):
        tile_match = is_ctx if ctx_tile else jnp.logical_not(is_ctx)

        @pl.when(jnp.logical_and(tile_match, static_ok))
        def _():
            attend_static_shift(nk)

        @pl.when(jnp.logical_and(tile_match, jnp.logical_not(static_ok)))
        def _():
            attend_row_max(nk)


def attention(shift, q, k, v, n_ctx_rows):
    bsz, _, s, _ = q.shape
    tq = ATT_TQ
    assert ATT_TK == 2 * LANES and n_ctx_rows % ATT_TK == 0 and s % ATT_TK == 0
    kernel = functools.partial(_attn_kernel, n_ctx_q_tiles=n_ctx_rows // tq, n_ctx_rows=n_ctx_rows)
    kv_spec = pl.BlockSpec((None, None, s, LANES), lambda b, g, i: (b, g, 0, 0))
    return pl.pallas_call(
        kernel,
        out_shape=jax.ShapeDtypeStruct((bsz, s, A_HEADS * LANES), BF16),
        grid=(bsz, A_KV_HEADS, s // tq),
        in_specs=[pl.BlockSpec(memory_space=pltpu.SMEM),
                  pl.BlockSpec((None, A_GROUP, tq, LANES), lambda b, g, i: (b, g, i, 0)), kv_spec, kv_spec],
        out_specs=pl.BlockSpec((None, tq, A_GROUP * LANES), lambda b, g, i: (b, i, g)),
        scratch_shapes=[pltpu.VMEM((A_GROUP * tq, s), F32),
                        pltpu.VMEM((A_GROUP * tq, s), BF16),
                        pltpu.VMEM((A_GROUP * tq, LANES), F32)],
        compiler_params=_params("parallel", "parallel", "arbitrary"),
        name="gqa_attention",
    )(shift, q, k, v)


def _gla_consts(reverse):
    c, sub = GLA_CHUNK, GLA_SUB
    t = lax.broadcasted_iota(jnp.int32, (c, c), 0)
    s = lax.broadcasted_iota(jnp.int32, (c, c), 1)
    if reverse:
        tri = s >= t
        blk = s >= (t // sub + 1) * sub
    else:
        tri = s <= t
        blk = s < (t // sub) * sub
    sums = jnp.concatenate([tri, blk], axis=0).astype(BF16)
    n = B_HEADS * c
    rt = lax.broadcasted_iota(jnp.int32, (n, n), 0) % c
    cs = lax.broadcasted_iota(jnp.int32, (n, n), 1) % c
    causal = (cs >= rt) if reverse else (cs <= rt)
    lane = lax.broadcasted_iota(jnp.int32, (1, B_K), 1)
    head_masks = [(lane // B_DK == hh).astype(F32) for hh in range(B_HEADS)]
    row_sub = lax.broadcasted_iota(jnp.int32, (n, 1), 0) % c // sub
    key_row = lax.broadcasted_iota(jnp.int32, (c, 1), 0)
    return sums, causal, head_masks, row_sub, key_row


def _gla_chunk(q, k, v, la, st, consts, reverse):
    sums, causal, head_masks, row_sub, key_row = consts
    c, sub = GLA_CHUNK, GLA_SUB

    def stack_heads(x):
        return jnp.concatenate([x * hm for hm in head_masks], axis=0)

    la_hi, la_lo = _split_bf16(la)
    cr = _dot(sums, la_hi) + _dot(sums, la_lo)
    cum = cr[0:c]
    ref = cr[c:2 * c]
    edge = 0 if reverse else c - 1
    total = cum[edge:edge + 1]

    q_in = stack_heads(q * jnp.exp(cum)).astype(BF16)
    q_loc = stack_heads(q * jnp.exp(cum - ref)).astype(BF16)
    k_out = stack_heads(k * jnp.exp(total - cum)).astype(BF16)

    scores = jnp.zeros((B_HEADS * c, B_HEADS * c), F32)
    for i in range(c // sub):
        ref_i = ref[i * sub:i * sub + 1]
        valid = (key_row >= i * sub) if reverse else (key_row < (i + 1) * sub)
        k_i = stack_heads(k * jnp.exp(jnp.where(valid, ref_i - cum, -jnp.inf))).astype(BF16)
        s_i = lax.dot_general(q_loc, k_i, NT_DIMS, preferred_element_type=F32)
        scores = jnp.where(row_sub == i, s_i, scores)
    scores = jnp.where(causal, scores, 0.0).astype(BF16)

    v_st = jnp.concatenate([v[:, hh * B_DV:(hh + 1) * B_DV] for hh in range(B_HEADS)], axis=0).astype(BF16)
    o = _dot(scores, v_st) + lax.dot_general(q_in, st.astype(BF16), NT_DIMS, preferred_element_type=F32)
    st_new = st * jnp.exp(total) + lax.dot_general(v_st, k_out, TN_DIMS, preferred_element_type=F32)
    return o, st_new


def _gla_kernel(qf_ref, kf_ref, vf_ref, laf_ref, qr_ref, kr_ref, vr_ref, lar_ref,
                of_ref, or_ref, stf_ref, str_ref):
    @pl.when(pl.program_id(1) == 0)
    def _():
        stf_ref[...] = jnp.zeros_like(stf_ref)
        str_ref[...] = jnp.zeros_like(str_ref)

    c = GLA_CHUNK
    n_chunks = qf_ref.shape[0] // c
    for reverse, (q_ref, k_ref, v_ref, la_ref, o_ref, st_ref) in (
            (False, (qf_ref, kf_ref, vf_ref, laf_ref, of_ref, stf_ref)),
            (True, (qr_ref, kr_ref, vr_ref, lar_ref, or_ref, str_ref))):
        consts = _gla_consts(reverse)
        st = st_ref[...]
        order = range(n_chunks - 1, -1, -1) if reverse else range(n_chunks)
        for n in order:
            rows = slice(n * c, (n + 1) * c)
            o, st = _gla_chunk(q_ref[rows, :], k_ref[rows, :], v_ref[rows, :], la_ref[rows, :], st, consts, reverse)
            for hh in range(B_HEADS):
                o_ref[rows, hh * B_DV:(hh + 1) * B_DV] = o[hh * c:(hh + 1) * c].astype(BF16)
        st_ref[...] = st


def gla_bidir(qb, kb, vb, laf, lab, n_ctx_tiles):
    bsz, s, _ = qb.shape
    t = ROW_TILE
    nt = s // t

    def rev_tile(j):
        return jnp.where(j < n_ctx_tiles, n_ctx_tiles - 1 - j, nt - 1 - (j - n_ctx_tiles))

    fwd = lambda w: pl.BlockSpec((None, t, w), lambda b, j: (b, j, 0))
    rev = lambda w: pl.BlockSpec((None, t, w), lambda b, j: (b, rev_tile(j), 0))
    return pl.pallas_call(
        _gla_kernel,
        out_shape=[jax.ShapeDtypeStruct((bsz, s, B_V), BF16)] * 2,
        grid=(bsz, nt),
        in_specs=[fwd(B_K), fwd(B_K), fwd(B_V), fwd(B_K), rev(B_K), rev(B_K), rev(B_V), rev(B_K)],
        out_specs=[fwd(B_V), rev(B_V)],
        scratch_shapes=[pltpu.VMEM((B_DV, B_K), F32)] * 2,
        compiler_params=_params("parallel", "arbitrary"),
        name="gla_bidir",
    )(qb, kb, vb, laf, qb, kb, vb, lab)


def _even_out_kernel(xc_ref, xl_ref, mod_ref, a_ref, of_ref, or_ref, g_ref, og_ref, wa_ref, wb_ref, o_ref, *,
                     n_ctx_tiles):
    o = of_ref[...].astype(F32) + or_ref[...].astype(F32)
    g = g_ref[...].astype(F32)
    parts = []
    for hh in range(B_HEADS):
        oh = o[:, hh * B_DV:(hh + 1) * B_DV]
        ms = jnp.mean(oh * oh, axis=-1, keepdims=True)
        y = oh * lax.rsqrt(ms + NORM_EPS) * og_ref[...]
        parts.append((y * _silu(g[:, hh * B_DV:(hh + 1) * B_DV])).astype(BF16))
    gla = jnp.concatenate(parts, axis=1)
    y = _dot(a_ref[...], wa_ref[...]) + _dot(gla, wb_ref[...])
    o_ref[...] = _stream_tile(xc_ref, xl_ref, n_ctx_tiles) + mod_ref[2:3, :] * y


def even_out_proj(stream, mods, a, o_f, o_r, gb, o_gain, wa, wb, n_ctx_tiles):
    t = ROW_TILE
    (xc, xl), x_specs = _stream_specs(stream, n_ctx_tiles, t)
    bsz, s, d = a.shape[0], a.shape[1], xc.shape[-1]
    row = lambda w: pl.BlockSpec((None, t, w), lambda b, i: (b, i, 0))
    return pl.pallas_call(
        functools.partial(_even_out_kernel, n_ctx_tiles=n_ctx_tiles),
        out_shape=jax.ShapeDtypeStruct((bsz, s, d), F32),
        grid=(bsz, s // t),
        in_specs=x_specs + [_mods_spec(n_ctx_tiles, d), row(a.shape[-1]), row(B_V), row(B_V), row(B_V),
                            _resident(o_gain.shape), _resident(wa.shape), _resident(wb.shape)],
        out_specs=row(d),
        compiler_params=_params("parallel", "parallel"),
        name="even_out_proj",
    )(xc, xl, mods, a, o_f, o_r, gb, o_gain, wa, wb)


def _ffn_kernel(x_ref, xp_ref, xn_ref, mod_ref, gain_ref, wu_ref, cw_ref, wd_ref, fg_ref, o_ref,
                h_ref, u_ref, act_ref, *, tile0, n_ctx_tiles, n_tiles, final_norm):
    ti = pl.program_id(1) + tile0
    t = x_ref.shape[0]
    d_ff = wd_ref.shape[0]
    gain = gain_ref[...]
    shift = mod_ref[3:4, :]
    scale = mod_ref[4:5, :]
    x = x_ref[...]
    has_prev = jnp.logical_and(ti != 0, ti != n_ctx_tiles)
    has_next = jnp.logical_and(ti != n_ctx_tiles - 1, ti != n_tiles - 1)
    hp = jnp.where(has_prev, _norm_mod(xp_ref[...], gain, shift, scale), 0.0)
    hn = jnp.where(has_next, _norm_mod(xn_ref[...], gain, shift, scale), 0.0)
    h_ref[...] = jnp.concatenate([hp, _norm_mod(x, gain, shift, scale), hn], axis=0).astype(BF16)
    lo = FFN_HALO - 1

    def conv(cols):
        cw = cw_ref[:, cols]
        return (cw[0:1] * u_ref[lo:lo + t, cols] + cw[1:2] * u_ref[lo + 1:lo + 1 + t, cols]
                + cw[2:3] * u_ref[lo + 2:lo + 2 + t, cols])

    for c0 in range(0, d_ff, FFN_CN):
        gate_cols = slice(c0, c0 + FFN_CN)
        val_cols = slice(d_ff + c0, d_ff + c0 + FFN_CN)
        h = h_ref[...]
        u_ref[:, gate_cols] = _dot(h, wu_ref[:, gate_cols])
        u_ref[:, val_cols] = _dot(h, wu_ref[:, val_cols])
        act_ref[:, gate_cols] = (_silu(conv(gate_cols)) * conv(val_cols)).astype(BF16)
    y = x + mod_ref[5:6, :] * _dot(act_ref[...], wd_ref[...])
    if final_norm:
        ms = jnp.mean(y * y, axis=-1, keepdims=True)
        y = y * lax.rsqrt(ms + NORM_EPS) * fg_ref[...]
    o_ref[...] = y


def conv_ffn(xx, mods, gain, wu, cw, wd, final_gain, n_ctx_tiles, latents_only, final_norm):
    bsz, s, d = xx.shape
    t = ROW_TILE
    nt = s // t
    tile0 = n_ctx_tiles if latents_only else 0
    hb = t // FFN_HALO
    last_hb = s // FFN_HALO - 1
    kernel = functools.partial(_ffn_kernel, tile0=tile0, n_ctx_tiles=n_ctx_tiles, n_tiles=nt, final_norm=final_norm)
    return pl.pallas_call(
        kernel,
        out_shape=jax.ShapeDtypeStruct((bsz, s - tile0 * t, d), F32),
        grid=(bsz, nt - tile0),
        in_specs=[pl.BlockSpec((None, t, d), lambda b, i: (b, i + tile0, 0)),
                  pl.BlockSpec((None, FFN_HALO, d), lambda b, i: (b, jnp.maximum((i + tile0) * hb - 1, 0), 0)),
                  pl.BlockSpec((None, FFN_HALO, d), lambda b, i: (b, jnp.minimum((i + tile0 + 1) * hb, last_hb), 0)),
                  pl.BlockSpec((None, None, N_MOD, d),
                               lambda b, i: (b, (i + tile0 >= n_ctx_tiles).astype(jnp.int32), 0, 0)),
                  _resident(gain.shape), _resident(wu.shape), _resident(cw.shape), _resident(wd.shape),
                  _resident(final_gain.shape)],
        out_specs=pl.BlockSpec((None, t, d), lambda b, i: (b, i, 0)),
        scratch_shapes=[pltpu.VMEM((t + 2 * FFN_HALO, d), BF16),
                        pltpu.VMEM((t + 2 * FFN_HALO, wu.shape[1]), F32),
                        pltpu.VMEM((t, wd.shape[0]), BF16)],
        compiler_params=_params("parallel", "parallel"),
        name="conv_ffn_final" if final_norm else "conv_ffn",
    )(xx, xx, xx, mods, gain, wu, cw, wd, final_gain)


def _odd_mods_spec(bsz, n_ctx_blocks, d):
    return pl.BlockSpec((bsz, None, N_MOD, d), lambda j: (0, (j >= n_ctx_blocks).astype(jnp.int32), 0, 0))


def _odd_in_kernel(x_ref, mod_ref, gain_ref, wg_ref, wr_ref, gate_ref, rec_ref, slab_ref):
    bsz, tt, d = x_ref.shape
    w = wr_ref.shape[1]
    h = _norm_mod(x_ref[...], gain_ref[...], mod_ref[:, 0:1, :], mod_ref[:, 1:2, :])
    h = h.reshape(bsz * tt, d).astype(BF16)
    gate_ref[...] = jax.nn.gelu(_dot(h, wg_ref[...]), approximate=True).astype(BF16).reshape(bsz, tt, w)
    rec = _dot(h, wr_ref[...])
    for b in range(bsz):
        for sl in range(w // LANES):
            slab_ref[sl, b * TM_PITCH:b * TM_PITCH + tt, :] = rec[b * tt:(b + 1) * tt, sl * LANES:(sl + 1) * LANES]
    for t in range(tt):
        for sl in range(w // LANES):
            rec_ref[t, :, sl * LANES:(sl + 1) * LANES] = slab_ref[sl, pl.ds(t, bsz, stride=TM_PITCH), :]


def odd_in_proj(xx, mods, gain, wg, wr, n_ctx_rows):
    bsz, s, d = xx.shape
    tt = LRU_TT
    w = wg.shape[1]
    assert tt <= TM_PITCH and TM_PITCH % 8 == 0
    return pl.pallas_call(
        _odd_in_kernel,
        out_shape=[jax.ShapeDtypeStruct((bsz, s, w), BF16), jax.ShapeDtypeStruct((s, bsz, w), F32)],
        grid=(s // tt,),
        in_specs=[pl.BlockSpec((bsz, tt, d), lambda j: (0, j, 0)), _odd_mods_spec(bsz, n_ctx_rows // tt, d),
                  _resident(gain.shape), _resident(wg.shape), _resident(wr.shape)],
        out_specs=[pl.BlockSpec((bsz, tt, w), lambda j: (0, j, 0)),
                   pl.BlockSpec((tt, bsz, w), lambda j: (j, 0, 0))],
        scratch_shapes=[pltpu.VMEM((w // LANES, bsz * TM_PITCH, LANES), F32)],
        compiler_params=_params("parallel"),
        name="odd_in_proj",
    )(xx, mods, gain, wg, wr)


def _lru_kernel(uf_ref, ufp_ref, ufn_ref, ur_ref, urp_ref, urn_ref, cw_ref, wax_ref, bax_ref, lam_ref,
                hf_ref, hr_ref, ue_ref, a_ref, x_ref, hst_ref, *, n_ctx_blocks, n_blocks):
    j = pl.program_id(0)
    tt, bsz, w = uf_ref.shape

    @pl.when(j == 0)
    def _():
        hst_ref[...] = jnp.zeros_like(hst_ref)

    rev_blk = jnp.where(j < n_ctx_blocks, n_ctx_blocks - 1 - j, n_blocks - 1 - (j - n_ctx_blocks))
    for d, (blk, u_ref, up_ref, un_ref, o_ref) in enumerate(
            ((j, uf_ref, ufp_ref, ufn_ref, hf_ref), (rev_blk, ur_ref, urp_ref, urn_ref, hr_ref))):
        has_prev = jnp.logical_and(blk != 0, blk != n_ctx_blocks)
        has_next = jnp.logical_and(blk != n_ctx_blocks - 1, blk != n_blocks - 1)
        ue_ref[0:1] = jnp.where(has_prev, up_ref[...], 0.0)
        ue_ref[1:tt + 1] = u_ref[...]
        ue_ref[tt + 1:tt + 3] = jnp.where(has_next, un_ref[...], 0.0)
        cw = cw_ref[...]
        uc = cw[0:1] * ue_ref[0:tt] + cw[1:2] * ue_ref[1:tt + 1] + cw[2:3] * ue_ref[2:tt + 2] + cw[3:4] * ue_ref[3:tt + 3]
        uc = uc.reshape(tt * bsz, w)
        half_c = (-0.5 * LRU_C) * _softplus(-lam_ref[d])
        for hh in range(LRU_HEADS):
            cols = slice(hh * LRU_HEAD_DIM, (hh + 1) * LRU_HEAD_DIM)
            uh = uc[:, cols]
            th = jnp.tanh(_dot(uh.astype(BF16), wax_ref[d, hh]) + bax_ref[d, hh])
            log_a = th[:, 0:LRU_HEAD_DIM] * half_c[:, cols] + half_c[:, cols]
            ig = 0.5 * th[:, LRU_HEAD_DIM:2 * LRU_HEAD_DIM] + 0.5
            a = jnp.exp(log_a)
            m2 = jnp.tanh(log_a) * (-1.0 - a * a)
            mult = jnp.where(m2 > 0.0, m2 * lax.rsqrt(m2), 0.0)
            a_ref[:, :, cols] = a.reshape(tt, bsz, LRU_HEAD_DIM)
            x_ref[:, :, cols] = (mult * ig * uh).reshape(tt, bsz, LRU_HEAD_DIM)
        h = hst_ref[d]
        for step in range(tt):
            tcur = tt - 1 - step if d == 1 else step
            h = a_ref[tcur] * h + x_ref[tcur]
            o_ref[tcur] = h
        hst_ref[d] = h


def lru_scan(u, cw, wax, bax, lam, n_ctx_rows):
    s, bsz, w = u.shape
    tt = LRU_TT
    nb = s // tt
    ncb = n_ctx_rows // tt

    def rev_blk(j):
        return jnp.where(j < ncb, ncb - 1 - j, nb - 1 - (j - ncb))

    def specs(blk):
        return [pl.BlockSpec((tt, bsz, w), lambda j: (blk(j), 0, 0)),
                pl.BlockSpec((1, bsz, w), lambda j: (jnp.maximum(blk(j) * tt - 1, 0), 0, 0)),
                pl.BlockSpec((2, bsz, w), lambda j: (jnp.minimum((blk(j) + 1) * (tt // 2), s // 2 - 1), 0, 0))]

    fwd_blk = lambda j: j
    kernel = functools.partial(_lru_kernel, n_ctx_blocks=ncb, n_blocks=nb)
    return pl.pallas_call(
        kernel,
        out_shape=[jax.ShapeDtypeStruct((s, bsz, w), F32)] * 2,
        grid=(nb,),
        in_specs=specs(fwd_blk) + specs(rev_blk) + [_resident(cw.shape), _resident(wax.shape),
                                                     _resident(bax.shape), _resident(lam.shape)],
        out_specs=[pl.BlockSpec((tt, bsz, w), lambda j: (j, 0, 0)),
                   pl.BlockSpec((tt, bsz, w), lambda j: (rev_blk(j), 0, 0))],
        scratch_shapes=[pltpu.VMEM((tt + 3, bsz, w), F32), pltpu.VMEM((tt, bsz, w), F32),
                        pltpu.VMEM((tt, bsz, w), F32), pltpu.VMEM((2, bsz, w), F32)],
        compiler_params=_params("arbitrary"),
        name="lru_scan",
    )(u, u, u, u, u, u, cw, wax, bax, lam)


def _odd_out_kernel(x_ref, mod_ref, gate_ref, hf_ref, hr_ref, w_ref, o_ref, slab_ref):
    bsz, tt, d = x_ref.shape
    w = w_ref.shape[0]
    for t in range(tt):
        hs = hf_ref[t] + hr_ref[t]
        for sl in range(w // LANES):
            slab_ref[sl, pl.ds(t, bsz, stride=TM_PITCH), :] = hs[:, sl * LANES:(sl + 1) * LANES]
    rec = jnp.concatenate(
        [jnp.concatenate([slab_ref[sl, b * TM_PITCH:b * TM_PITCH + tt, :] for sl in range(w // LANES)], axis=1)
         for b in range(bsz)], axis=0)
    mixed = (gate_ref[...].reshape(bsz * tt, w) * rec).astype(BF16)
    y = _dot(mixed, w_ref[...]).reshape(bsz, tt, d)
    o_ref[...] = x_ref[...] + mod_ref[:, 2:3, :] * y


def odd_out_proj(xx, mods, gate, hf, hr, w_out, n_ctx_rows):
    bsz, s, d = xx.shape
    tt = LRU_TT
    w = gate.shape[-1]
    bmaj = lambda n: pl.BlockSpec((bsz, tt, n), lambda j: (0, j, 0))
    tmaj = pl.BlockSpec((tt, bsz, w), lambda j: (j, 0, 0))
    return pl.pallas_call(
        _odd_out_kernel,
        out_shape=jax.ShapeDtypeStruct((bsz, s, d), F32),
        grid=(s // tt,),
        in_specs=[bmaj(d), _odd_mods_spec(bsz, n_ctx_rows // tt, d), bmaj(w), tmaj, tmaj, _resident(w_out.shape)],
        out_specs=bmaj(d),
        scratch_shapes=[pltpu.VMEM((w // LANES, bsz * TM_PITCH, LANES), F32)],
        compiler_params=_params("parallel"),
        name="odd_out_proj",
    )(xx, mods, gate, hf, hr, w_out)


def _pad_heads(w, n_heads, head_dim):
    d = w.shape[0]
    w = w.reshape(d, n_heads, head_dim)
    return jnp.pad(w, ((0, 0), (0, 0), (0, LANES - head_dim))).reshape(d, n_heads * LANES)


def _rope_tables(n_ctx, n_lat):
    rows = n_lat // GRID_W
    row = jnp.repeat(jnp.arange(rows, dtype=F32), GRID_W)
    col = jnp.tile(jnp.arange(GRID_W, dtype=F32), rows)
    n_freq = A_HEAD_DIM // 4
    inv_freq = ROPE_THETA ** (-jnp.arange(n_freq, dtype=F32) / n_freq)
    ar = row[:, None] * inv_freq
    ac = col[:, None] * inv_freq
    cos = jnp.concatenate([jnp.cos(ar), jnp.cos(ar), jnp.cos(ac), jnp.cos(ac)], axis=-1)
    sin = jnp.concatenate([-jnp.sin(ar), jnp.sin(ar), -jnp.sin(ac), jnp.sin(ac)], axis=-1)
    cos = jnp.concatenate([jnp.ones((n_ctx, A_HEAD_DIM), F32), cos], axis=0)
    sin = jnp.concatenate([jnp.zeros((n_ctx, A_HEAD_DIM), F32), sin], axis=0)
    return jnp.tile(cos, (1, 2)), jnp.tile(sin, (1, 2))


def kernel(x, c, ctx, c_ctx, ada_w, ada_b, norm_mix, norm_ffn, ffn_w_up, ffn_conv, ffn_w_down, even_w_in, even_w_out, attn_q_gain, attn_k_gain, gla_gate_w_up, gla_gate_b, gla_out_gain, lru_w_in, lru_conv, lru_lambda, lru_w_a, lru_b_a, lru_w_x, lru_b_x, lru_w_out, final_gain):
    bsz, n_lat, d = x.shape
    n_ctx = ctx.shape[1]
    depth = ada_w.shape[0]
    d_ff = ffn_w_down.shape[1]
    assert n_ctx % ROW_TILE == 0 and n_lat % ROW_TILE == 0 and d_ff % FFN_CN == 0
    n_ctx_tiles = n_ctx // ROW_TILE
    s = n_ctx + n_lat

    xx = (ctx, x)
    cond_rows = -(-(bsz + 1) // 8) * 8
    cond = jnp.zeros((cond_rows, d), F32).at[:bsz].set(c).at[bsz].set(c_ctx)
    cos, sin = _rope_tables(n_ctx, n_lat)
    pad_gain = lambda g: jnp.pad(g, (0, LANES - A_HEAD_DIM)).reshape(1, LANES)

    for l in range(depth):
        last = l == depth - 1
        j = l // 2
        table = adaln_table(cond, ada_w, ada_b.reshape(depth, 1, N_MOD * d), l)
        m_lat = table[:bsz].reshape(bsz, N_MOD, d)
        m_ctx = jnp.broadcast_to(table[bsz].reshape(1, N_MOD, d), (bsz, N_MOD, d))
        mods = jnp.stack([m_ctx, m_lat], axis=1)
        gain_mix = norm_mix[l].reshape(1, d)

        if l % 2 == 0:
            w_in = even_w_in[j]
            o0 = A_HEADS * A_HEAD_DIM
            o1 = o0 + A_KV_HEADS * A_HEAD_DIM
            o2 = o1 + A_KV_HEADS * A_HEAD_DIM
            o3 = o2 + 2 * B_K + 2 * B_V
            wq = _pad_heads(w_in[:, :o0], A_HEADS, A_HEAD_DIM).astype(BF16)
            wkv = jnp.concatenate([_pad_heads(w_in[:, o0:o1], A_KV_HEADS, A_HEAD_DIM),
                                   _pad_heads(w_in[:, o1:o2], A_KV_HEADS, A_HEAD_DIM)], axis=1).astype(BF16)
            wb = w_in[:, o2:o3].astype(BF16)
            wlr = w_in[:, o3:].astype(BF16)
            zeros = jnp.zeros((B_GATE_RANK, B_K), F32)
            wgate = jnp.concatenate([jnp.concatenate([gla_gate_w_up[j, 0], zeros], axis=1),
                                     jnp.concatenate([zeros, gla_gate_w_up[j, 1]], axis=1)], axis=0)
            bgate = gla_gate_b[j].reshape(1, 2 * B_K)
            s_bound = A_HEAD_DIM ** 0.5 * jnp.max(jnp.abs(attn_q_gain[j])) * jnp.max(jnp.abs(attn_k_gain[j]))
            qaug = jnp.zeros((1, LANES), F32).at[0, A_HEAD_DIM].set(-s_bound * LOG2E)
            q, k, v, qb, kb, vb, gb, laf, lab = even_in_proj(
                xx, mods, gain_mix, wq, wkv, wb, wlr, wgate, bgate,
                pad_gain(attn_q_gain[j]), pad_gain(attn_k_gain[j]), qaug, cos, sin, n_ctx_tiles)
            a = attention(s_bound.reshape(1), q, k, v, n_ctx)
            o_f, o_r = gla_bidir(qb, kb, vb, laf, lab, n_ctx_tiles)
            w_out = even_w_out[j]
            wa = jnp.pad(w_out[:o0].reshape(A_HEADS, A_HEAD_DIM, d),
                         ((0, 0), (0, LANES - A_HEAD_DIM), (0, 0))).reshape(A_HEADS * LANES, d).astype(BF16)
            xx = even_out_proj(xx, mods, a, o_f, o_r, gb, gla_out_gain[j].reshape(1, B_DV), wa,
                               w_out[o0:].astype(BF16), n_ctx_tiles)
        else:
            w_in = lru_w_in[j]
            gate, rec = odd_in_proj(xx, mods, gain_mix, w_in[:, :LRU_WIDTH].astype(BF16),
                                    w_in[:, LRU_WIDTH:].astype(BF16), n_ctx)
            wax = (0.5 * jnp.concatenate([lru_w_a[j], lru_w_x[j]], axis=-1)).astype(BF16)
            bax = 0.5 * jnp.concatenate([lru_b_a[j].reshape(2, LRU_HEADS, 1, LRU_HEAD_DIM),
                                         lru_b_x[j].reshape(2, LRU_HEADS, 1, LRU_HEAD_DIM)], axis=-1)
            hf, hr = lru_scan(rec, lru_conv[j], wax, bax, lru_lambda[j].reshape(2, 1, LRU_WIDTH), n_ctx)
            xx = odd_out_proj(xx, mods, gate, hf, hr, lru_w_out[j].astype(BF16), n_ctx)

        xx = conv_ffn(xx, mods, norm_ffn[l].reshape(1, d), ffn_w_up[l].astype(BF16), ffn_conv[l],
                      ffn_w_down[l].astype(BF16), final_gain.reshape(1, d),
                      n_ctx_tiles, latents_only=last, final_norm=last)
    return xx
```

```python
import functools

import numpy as np
import jax
import jax.numpy as jnp
from jax import lax
from jax.experimental import pallas as pl
from jax.experimental.pallas import tpu as pltpu

F32 = jnp.float32
BF16 = jnp.bfloat16

NORM_EPS = 1e-6
N_MOD = 6
GRID_W = 64
ROPE_THETA = 10000.0

A_HEADS = 8
A_KV_HEADS = 2
A_GROUP = A_HEADS // A_KV_HEADS
A_HEAD_DIM = 64

B_HEADS = 4
B_DK = 64
B_DV = 128
B_K = B_HEADS * B_DK
B_V = B_HEADS * B_DV
B_GATE_RANK = 16
B_GATE_TAU = 16.0
GLA_CHUNK = 64
GLA_SUB = 16

LRU_HEADS = 10
LRU_HEAD_DIM = 128
LRU_WIDTH = LRU_HEADS * LRU_HEAD_DIM
LRU_C = 8.0

LANES = 128
ROW_TILE = 256
EVEN_IN_SPLIT = 1
ATT_TQ = 256
ATT_TK = 256
FFN_CN = 256
FFN_HALO = 8
LRU_TT = 32
TM_PITCH = 40
VMEM_LIMIT = 56 * 1024 * 1024

LOG2E = float(np.log2(np.e))
ATT_MAX_STATIC_SHIFT = 40.0

NT_DIMS = (((1,), (1,)), ((), ()))
TN_DIMS = (((0,), (0,)), ((), ()))


def _params(*sem):
    return pltpu.CompilerParams(dimension_semantics=sem, vmem_limit_bytes=VMEM_LIMIT)


def _resident(shape):
    nd = len(shape)
    return pl.BlockSpec(shape, lambda *_: (0,) * nd, pipeline_mode=pl.Buffered(1))


def _dot(a, b):
    return jnp.dot(a, b, preferred_element_type=F32)


def _split_bf16(a):
    hi = a.astype(BF16)
    lo = (a - hi.astype(F32)).astype(BF16)
    return hi, lo


def _dot_f32(a, b):
    ah, al = _split_bf16(a)
    bh, bl = _split_bf16(b)
    return _dot(ah, bh) + _dot(ah, bl) + _dot(al, bh)


def _sigmoid(x):
    return 0.5 * jnp.tanh(0.5 * x) + 0.5


def _silu(x):
    return x * _sigmoid(x)


def _softplus(x):
    return jnp.maximum(x, 0.0) + jnp.log1p(jnp.exp(-jnp.abs(x)))


def _norm_mod(x, gain, shift, scale):
    ms = jnp.mean(x * x, axis=-1, keepdims=True)
    return (x * lax.rsqrt(ms + NORM_EPS) * gain) * (1.0 + scale) + shift


def _adaln_kernel(c_ref, w_ref, b_ref, o_ref):
    o_ref[...] = _dot_f32(_silu(c_ref[...]), w_ref[...]) + b_ref[...]


def adaln_table(cond, w, b, layer):
    rows, d = cond.shape
    n = w.shape[2]
    tn = 768
    return pl.pallas_call(
        _adaln_kernel,
        out_shape=jax.ShapeDtypeStruct((rows, n), F32),
        grid=(n // tn,),
        in_specs=[pl.BlockSpec((rows, d), lambda j: (0, 0)),
                  pl.BlockSpec((None, d, tn), lambda j: (layer, 0, j)),
                  pl.BlockSpec((None, 1, tn), lambda j: (layer, 0, j))],
        out_specs=pl.BlockSpec((rows, tn), lambda j: (0, j)),
        compiler_params=_params("arbitrary"),
        name="adaln_table",
    )(cond, w, b)


def _mods_spec(n_ctx_tiles, d):
    return pl.BlockSpec((None, None, N_MOD, d),
                        lambda b, i: (b, (i >= n_ctx_tiles).astype(jnp.int32), 0, 0))


def _stream_specs(stream, n_ctx_tiles, t):
    if isinstance(stream, tuple):
        ctx_arr, lat_arr = stream
        lat_off = 0
    else:
        ctx_arr = lat_arr = stream
        lat_off = n_ctx_tiles
    d = ctx_arr.shape[-1]
    ctx_spec = pl.BlockSpec((None, t, d), lambda b, i: (b, jnp.minimum(i, n_ctx_tiles - 1), 0))
    lat_spec = pl.BlockSpec((None, t, d), lambda b, i: (b, jnp.maximum(i - n_ctx_tiles, 0) + lat_off, 0))
    return (ctx_arr, lat_arr), [ctx_spec, lat_spec]


def _stream_tile(xc_ref, xl_ref, n_ctx_tiles):
    return jnp.where(pl.program_id(1) < n_ctx_tiles, xc_ref[...], xl_ref[...])


def _even_in_kernel(xc_ref, xl_ref, mod_ref, gain_ref, wqkv_ref, wb_ref, wlr_ref, wgate_ref, bgate_ref,
                    qg_ref, kg_ref, qaug_ref, cos_ref, sin_ref,
                    q_ref, k_ref, v_ref, qb_ref, kb_ref, vb_ref, gb_ref, laf_ref, lab_ref, *, n_ctx_tiles):
    t = cos_ref.shape[0]
    tr = t // EVEN_IN_SPLIT
    lane = lax.broadcasted_iota(jnp.int32, (tr, LANES), 1)
    first_half = (lane % 32) < 16
    k_aug = (lane == A_HEAD_DIM).astype(F32)
    is_ctx = pl.program_id(1) < n_ctx_tiles

    for r0 in range(0, t, tr):
        rows = slice(r0, r0 + tr)
        x = jnp.where(is_ctx, xc_ref[rows, :], xl_ref[rows, :])
        h = _norm_mod(x, gain_ref[...], mod_ref[0:1, :], mod_ref[1:2, :]).astype(BF16)
        cos = cos_ref[rows, :]
        sin = sin_ref[rows, :]

        def head_norm_rope(xh, g):
            ms = jnp.sum(xh * xh, axis=-1, keepdims=True) * (1.0 / A_HEAD_DIM)
            y = xh * lax.rsqrt(ms + NORM_EPS) * g
            swapped = jnp.where(first_half, pltpu.roll(y, LANES - 16, 1), pltpu.roll(y, 16, 1))
            return y * cos + swapped * sin

        qkv = _dot(h, wqkv_ref[...])
        slab = lambda hh: qkv[:, hh * LANES:(hh + 1) * LANES]
        for hh in range(A_HEADS):
            qh = head_norm_rope(slab(hh), qg_ref[...])
            q_ref[hh, rows, :] = (qh * (A_HEAD_DIM ** -0.5 * LOG2E) + qaug_ref[...]).astype(BF16)
        for hh in range(A_KV_HEADS):
            k_ref[hh, rows, :] = (head_norm_rope(slab(A_HEADS + hh), kg_ref[...]) + k_aug).astype(BF16)
            v_ref[hh, rows, :] = jnp.where(lane >= A_HEAD_DIM, 1.0, slab(A_HEADS + A_KV_HEADS + hh)).astype(BF16)

        pb = _dot(h, wb_ref[...])
        qb_ref[rows, :] = pb[:, 0:B_K] * B_DK ** -0.5
        kb_ref[rows, :] = pb[:, B_K:2 * B_K]
        vb_ref[rows, :] = pb[:, 2 * B_K:2 * B_K + B_V]
        gb_ref[rows, :] = pb[:, 2 * B_K + B_V:2 * B_K + 2 * B_V].astype(BF16)

        lr = _dot(h, wlr_ref[...])
        z = _dot_f32(lr, wgate_ref[...]) + bgate_ref[...]
        la = -_softplus(-z) * (1.0 / B_GATE_TAU)
        laf_ref[rows, :] = la[:, 0:B_K]
        lab_ref[rows, :] = la[:, B_K:2 * B_K]


def even_in_proj(stream, mods, gain, wqkv, wb, wlr, wgate, bgate, qg, kg, qaug, cos, sin, n_ctx_tiles):
    t = ROW_TILE
    (xc, xl), x_specs = _stream_specs(stream, n_ctx_tiles, t)
    bsz, d = xc.shape[0], xc.shape[-1]
    s = cos.shape[0]
    row = lambda w: pl.BlockSpec((None, t, w), lambda b, i: (b, i, 0))
    heads = lambda n: pl.BlockSpec((None, n, t, LANES), lambda b, i: (b, 0, i, 0))
    tab = pl.BlockSpec((t, LANES), lambda b, i: (i, 0))
    out_shape = [
        jax.ShapeDtypeStruct((bsz, A_HEADS, s, LANES), BF16),
        jax.ShapeDtypeStruct((bsz, A_KV_HEADS, s, LANES), BF16),
        jax.ShapeDtypeStruct((bsz, A_KV_HEADS, s, LANES), BF16),
        jax.ShapeDtypeStruct((bsz, s, B_K), F32),
        jax.ShapeDtypeStruct((bsz, s, B_K), F32),
        jax.ShapeDtypeStruct((bsz, s, B_V), F32),
        jax.ShapeDtypeStruct((bsz, s, B_V), BF16),
        jax.ShapeDtypeStruct((bsz, s, B_K), F32),
        jax.ShapeDtypeStruct((bsz, s, B_K), F32),
    ]
    return pl.pallas_call(
        functools.partial(_even_in_kernel, n_ctx_tiles=n_ctx_tiles),
        out_shape=out_shape,
        grid=(bsz, s // t),
        in_specs=x_specs + [_mods_spec(n_ctx_tiles, d), _resident(gain.shape),
                  _resident(wqkv.shape), _resident(wb.shape), _resident(wlr.shape),
                  _resident(wgate.shape), _resident(bgate.shape), _resident(qg.shape), _resident(kg.shape),
                  _resident(qaug.shape), tab, tab],
        out_specs=[heads(A_HEADS), heads(A_KV_HEADS), heads(A_KV_HEADS),
                   row(B_K), row(B_K), row(B_V), row(B_V), row(B_K), row(B_K)],
        compiler_params=_params("parallel", "parallel"),
        name="even_in_proj",
    )(xc, xl, mods, gain, wqkv, wb, wlr, wgate, bgate, qg, kg, qaug, cos, sin)


def _attn_kernel(shift_ref, q_ref, k_ref, v_ref, o_ref, s_ref, p_ref, m_ref, *, n_ctx_q_tiles, n_ctx_rows):
    i = pl.program_id(2)
    tq = q_ref.shape[1]
    rows = A_GROUP * tq
    n_keys = k_ref.shape[0]

    def finish(acc):
        out = acc / acc[:, A_HEAD_DIM:A_HEAD_DIM + 1]
        low = lax.broadcasted_iota(jnp.int32, (tq, LANES), 1) < A_HEAD_DIM
        for j in range(0, A_GROUP, 2):
            even = out[j * tq:(j + 1) * tq]
            odd = pltpu.roll(out[(j + 1) * tq:(j + 2) * tq], A_HEAD_DIM, 1)
            o_ref[:, (j // 2) * LANES:(j // 2 + 1) * LANES] = jnp.where(low, even, odd).astype(BF16)

    def attend_static_shift(nk):
        qs = q_ref[...].reshape(rows, LANES)
        acc = jnp.zeros((rows, LANES), F32)
        for c0 in range(0, nk, ATT_TK):
            s = lax.dot_general(qs, k_ref[c0:c0 + ATT_TK, :], NT_DIMS, preferred_element_type=F32)
            acc = acc + _dot(jnp.exp2(s).astype(BF16), v_ref[c0:c0 + ATT_TK, :])
        finish(acc)

    def attend_row_max(nk):
        qs = q_ref[...].reshape(rows, LANES)
        m_ref[...] = jnp.full(m_ref.shape, -jnp.inf, F32)
        for c0 in range(0, nk, ATT_TK):
            cols = slice(c0, c0 + ATT_TK)
            s = lax.dot_general(qs, k_ref[cols, :], NT_DIMS, preferred_element_type=F32)
            s_ref[:, cols] = s
            m_ref[...] = jnp.maximum(m_ref[...], jnp.maximum(s[:, 0:LANES], s[:, LANES:2 * LANES]))
        m = jnp.max(m_ref[...], axis=1, keepdims=True)
        for c0 in range(0, nk, ATT_TK):
            cols = slice(c0, c0 + ATT_TK)
            p_ref[:, cols] = jnp.exp2(s_ref[:, cols] - m).astype(BF16)
        finish(_dot(p_ref[:, 0:nk], v_ref[0:nk, :]))

    is_ctx = i < n_ctx_q_tiles
    static_ok = shift_ref[0] <= ATT_MAX_STATIC_SHIFT
    for ctx_tile, nk in ((True, n_ctx_rows), (False, n_keys)):
        tile_match = is_ctx if ctx_tile else jnp.logical_not(is_ctx)

        @pl.when(jnp.logical_and(tile_match, static_ok))
        def _():
            attend_static_shift(nk)

        @pl.when(jnp.logical_and(tile_match, jnp.logical_not(static_ok)))
        def _():
            attend_row_max(nk)


def attention(shift, q, k, v, n_ctx_rows):
    bsz, _, s, _ = q.shape
    tq = ATT_TQ
    assert ATT_TK == 2 * LANES and n_ctx_rows % ATT_TK == 0 and s % ATT_TK == 0
    kernel = functools.partial(_attn_kernel, n_ctx_q_tiles=n_ctx_rows // tq, n_ctx_rows=n_ctx_rows)
    kv_spec = pl.BlockSpec((None, None, s, LANES), lambda b, g, i: (b, g, 0, 0))
    return pl.pallas_call(
        kernel,
        out_shape=jax.ShapeDtypeStruct((bsz, s, A_HEADS * A_HEAD_DIM), BF16),
        grid=(bsz, A_KV_HEADS, s // tq),
        in_specs=[pl.BlockSpec(memory_space=pltpu.SMEM),
                  pl.BlockSpec((None, A_GROUP, tq, LANES), lambda b, g, i: (b, g, i, 0)), kv_spec, kv_spec],
        out_specs=pl.BlockSpec((None, tq, A_GROUP * A_HEAD_DIM), lambda b, g, i: (b, i, g)),
        scratch_shapes=[pltpu.VMEM((A_GROUP * tq, s), F32),
                        pltpu.VMEM((A_GROUP * tq, s), BF16),
                        pltpu.VMEM((A_GROUP * tq, LANES), F32)],
        compiler_params=_params("parallel", "parallel", "arbitrary"),
        name="gqa_attention",
    )(shift, q, k, v)


def _gla_consts(reverse):
    c, sub = GLA_CHUNK, GLA_SUB
    t = lax.broadcasted_iota(jnp.int32, (c, c), 0)
    s = lax.broadcasted_iota(jnp.int32, (c, c), 1)
    if reverse:
        tri = s >= t
        blk = s >= (t // sub + 1) * sub
    else:
        tri = s <= t
        blk = s < (t // sub) * sub
    sums = jnp.concatenate([tri, blk], axis=0).astype(BF16)
    sums = jnp.concatenate([sums, sums], axis=1)
    n_rows, n_cols = B_HEADS * sub, B_HEADS * c
    row = lax.broadcasted_iota(jnp.int32, (n_rows, n_cols), 0)
    col = lax.broadcasted_iota(jnp.int32, (n_rows, n_cols), 1)
    same_head = row // sub == col // c
    sub_masks = []
    for i in range(c // sub):
        rt = i * sub + row % sub
        causal = (col % c >= rt) if reverse else (col % c <= rt)
        sub_masks.append(jnp.logical_and(same_head, causal))
    lane = lax.broadcasted_iota(jnp.int32, (1, B_K), 1)
    head_masks = [(lane // B_DK == hh).astype(F32) for hh in range(B_HEADS)]
    key_row = lax.broadcasted_iota(jnp.int32, (c, 1), 0)
    return sums, sub_masks, head_masks, key_row


def _gla_decays(la, consts, reverse):
    sums = consts[0]
    c = GLA_CHUNK
    cr = _dot(sums, jnp.concatenate(_split_bf16(la), axis=0))
    cum = cr[0:c]
    ref = cr[c:2 * c]
    edge = 0 if reverse else c - 1
    return cum, ref, cum[edge:edge + 1]


def _gla_operands(q, k, v, cum, ref, total, consts, reverse):
    _, _, head_masks, key_row = consts
    c, sub = GLA_CHUNK, GLA_SUB

    def stack_heads(x):
        return jnp.concatenate([x * hm for hm in head_masks], axis=0)

    q_in = stack_heads(q * jnp.exp(cum)).astype(BF16)
    q_loc = stack_heads(q * jnp.exp(cum - ref)).astype(BF16)
    k_out = stack_heads(k * jnp.exp(total - cum)).astype(BF16)
    k_sub = []
    for i in range(c // sub):
        ref_i = ref[i * sub:i * sub + 1]
        valid = (key_row >= i * sub) if reverse else (key_row < (i + 1) * sub)
        k_sub.append((k * jnp.exp(jnp.where(valid, ref_i - cum, -jnp.inf))).astype(BF16))
    v_st = jnp.concatenate([v[:, hh * B_DV:(hh + 1) * B_DV] for hh in range(B_HEADS)], axis=0).astype(BF16)
    return q_in, q_loc, k_out, k_sub, v_st


def _gla_local(q_loc, k_out, k_sub, v_st, consts):
    sub_masks = consts[1]
    c, sub = GLA_CHUNK, GLA_SUB
    pieces = [[None] * (c // sub) for _ in range(B_HEADS)]
    for i in range(c // sub):
        q_i = jnp.concatenate([q_loc[hh * c + i * sub:hh * c + (i + 1) * sub] for hh in range(B_HEADS)], axis=0)
        s_i = lax.dot_general(q_i, jnp.concatenate([k_sub[i]] * B_HEADS, axis=0), NT_DIMS,
                              preferred_element_type=F32)
        s_i = jnp.where(sub_masks[i], s_i, 0.0)
        for hh in range(B_HEADS):
            pieces[hh][i] = s_i[hh * sub:(hh + 1) * sub]
    scores = jnp.concatenate([p for head in pieces for p in head], axis=0).astype(BF16)
    o_local = _dot(scores, v_st)
    st_inc = lax.dot_general(v_st, k_out, TN_DIMS, preferred_element_type=F32)
    return o_local, st_inc


def _gla_kernel(qf_ref, kf_ref, vf_ref, laf_ref, qr_ref, kr_ref, vr_ref, lar_ref,
                of_ref, or_ref, stf_ref, str_ref):
    @pl.when(pl.program_id(1) == 0)
    def _():
        stf_ref[...] = jnp.zeros_like(stf_ref)
        str_ref[...] = jnp.zeros_like(str_ref)

    c = GLA_CHUNK
    n_chunks = qf_ref.shape[0] // c
    dirs = ((False, qf_ref, kf_ref, vf_ref, laf_ref, of_ref, stf_ref),
            (True, qr_ref, kr_ref, vr_ref, lar_ref, or_ref, str_ref))
    consts = {rev: _gla_consts(rev) for rev in (False, True)}
    units = [(d, n) for n in range(n_chunks) for d in range(2)]
    rows = lambda n: slice(n * c, (n + 1) * c)

    decays = {}
    for d, n in units:
        rev, la_ref = dirs[d][0], dirs[d][4]
        decays[d, n] = _gla_decays(la_ref[rows(n), :], consts[rev], rev)
    operands = {}
    for d, n in units:
        rev, q_ref, k_ref, v_ref = dirs[d][:4]
        operands[d, n] = _gla_operands(q_ref[rows(n), :], k_ref[rows(n), :], v_ref[rows(n), :],
                                       *decays[d, n], consts[rev], rev)
    local = {}
    for d, n in units:
        _, q_loc, k_out, k_sub, v_st = operands[d, n]
        local[d, n] = _gla_local(q_loc, k_out, k_sub, v_st, consts[dirs[d][0]])

    for d in range(2):
        rev, o_ref, st_ref = dirs[d][0], dirs[d][5], dirs[d][6]
        st = st_ref[...]
        for n in (range(n_chunks - 1, -1, -1) if rev else range(n_chunks)):
            o_local, st_inc = local[d, n]
            q_in = operands[d, n][0]
            o = o_local + lax.dot_general(q_in, st.astype(BF16), NT_DIMS, preferred_element_type=F32)
            st = st * jnp.exp(decays[d, n][2]) + st_inc
            for hh in range(B_HEADS):
                o_ref[rows(n), hh * B_DV:(hh + 1) * B_DV] = o[hh * c:(hh + 1) * c].astype(BF16)
        st_ref[...] = st


def gla_bidir(qb, kb, vb, laf, lab, n_ctx_tiles):
    bsz, s, _ = qb.shape
    t = ROW_TILE
    nt = s // t

    def rev_tile(j):
        return jnp.where(j < n_ctx_tiles, n_ctx_tiles - 1 - j, nt - 1 - (j - n_ctx_tiles))

    fwd = lambda w: pl.BlockSpec((None, t, w), lambda b, j: (b, j, 0))
    rev = lambda w: pl.BlockSpec((None, t, w), lambda b, j: (b, rev_tile(j), 0))
    return pl.pallas_call(
        _gla_kernel,
        out_shape=[jax.ShapeDtypeStruct((bsz, s, B_V), BF16)] * 2,
        grid=(bsz, nt),
        in_specs=[fwd(B_K), fwd(B_K), fwd(B_V), fwd(B_K), rev(B_K), rev(B_K), rev(B_V), rev(B_K)],
        out_specs=[fwd(B_V), rev(B_V)],
        scratch_shapes=[pltpu.VMEM((B_DV, B_K), F32)] * 2,
        compiler_params=_params("parallel", "arbitrary"),
        name="gla_bidir",
    )(qb, kb, vb, laf, qb, kb, vb, lab)


def _even_out_kernel(xc_ref, xl_ref, mod_ref, a_ref, of_ref, or_ref, g_ref, og_ref, wa_ref, wb_ref, o_ref, *,
                     n_ctx_tiles):
    o = of_ref[...].astype(F32) + or_ref[...].astype(F32)
    g = g_ref[...].astype(F32)
    parts = []
    for hh in range(B_HEADS):
        oh = o[:, hh * B_DV:(hh + 1) * B_DV]
        ms = jnp.mean(oh * oh, axis=-1, keepdims=True)
        y = oh * lax.rsqrt(ms + NORM_EPS) * og_ref[...]
        parts.append((y * _silu(g[:, hh * B_DV:(hh + 1) * B_DV])).astype(BF16))
    gla = jnp.concatenate(parts, axis=1)
    y = _dot(a_ref[...], wa_ref[...]) + _dot(gla, wb_ref[...])
    o_ref[...] = _stream_tile(xc_ref, xl_ref, n_ctx_tiles) + mod_ref[2:3, :] * y


def even_out_proj(stream, mods, a, o_f, o_r, gb, o_gain, wa, wb, n_ctx_tiles):
    t = ROW_TILE
    (xc, xl), x_specs = _stream_specs(stream, n_ctx_tiles, t)
    bsz, s, d = a.shape[0], a.shape[1], xc.shape[-1]
    row = lambda w: pl.BlockSpec((None, t, w), lambda b, i: (b, i, 0))
    return pl.pallas_call(
        functools.partial(_even_out_kernel, n_ctx_tiles=n_ctx_tiles),
        out_shape=jax.ShapeDtypeStruct((bsz, s, d), F32),
        grid=(bsz, s // t),
        in_specs=x_specs + [_mods_spec(n_ctx_tiles, d), row(a.shape[-1]), row(B_V), row(B_V), row(B_V),
                            _resident(o_gain.shape), _resident(wa.shape), _resident(wb.shape)],
        out_specs=row(d),
        compiler_params=_params("parallel", "parallel"),
        name="even_out_proj",
    )(xc, xl, mods, a, o_f, o_r, gb, o_gain, wa, wb)


def _ffn_kernel(x_ref, xp_ref, xn_ref, mod_ref, gain_ref, wu_ref, cw_ref, wd_ref, fg_ref, o_ref,
                h_ref, u_ref, act_ref, *, tile0, n_ctx_tiles, n_tiles, final_norm):
    ti = pl.program_id(1) + tile0
    t = x_ref.shape[0]
    d_ff = wd_ref.shape[0]
    gain = gain_ref[...]
    shift = mod_ref[3:4, :]
    scale = mod_ref[4:5, :]
    x = x_ref[...]
    has_prev = jnp.logical_and(ti != 0, ti != n_ctx_tiles)
    has_next = jnp.logical_and(ti != n_ctx_tiles - 1, ti != n_tiles - 1)
    hp = jnp.where(has_prev, _norm_mod(xp_ref[...], gain, shift, scale), 0.0)
    hn = jnp.where(has_next, _norm_mod(xn_ref[...], gain, shift, scale), 0.0)
    h_ref[...] = jnp.concatenate([hp, _norm_mod(x, gain, shift, scale), hn], axis=0).astype(BF16)
    lo = FFN_HALO - 1

    def conv(cols):
        cw = cw_ref[:, cols]
        return (cw[0:1] * u_ref[lo:lo + t, cols] + cw[1:2] * u_ref[lo + 1:lo + 1 + t, cols]
                + cw[2:3] * u_ref[lo + 2:lo + 2 + t, cols])

    for c0 in range(0, d_ff, FFN_CN):
        gate_cols = slice(c0, c0 + FFN_CN)
        val_cols = slice(d_ff + c0, d_ff + c0 + FFN_CN)
        h = h_ref[...]
        u_ref[:, gate_cols] = _dot(h, wu_ref[:, gate_cols])
        u_ref[:, val_cols] = _dot(h, wu_ref[:, val_cols])
        act_ref[:, gate_cols] = (_silu(conv(gate_cols)) * conv(val_cols)).astype(BF16)
    y = x + mod_ref[5:6, :] * _dot(act_ref[...], wd_ref[...])
    if final_norm:
        ms = jnp.mean(y * y, axis=-1, keepdims=True)
        y = y * lax.rsqrt(ms + NORM_EPS) * fg_ref[...]
    o_ref[...] = y


def conv_ffn(xx, mods, gain, wu, cw, wd, final_gain, n_ctx_tiles, latents_only, final_norm):
    bsz, s, d = xx.shape
    t = ROW_TILE
    nt = s // t
    tile0 = n_ctx_tiles if latents_only else 0
    hb = t // FFN_HALO
    last_hb = s // FFN_HALO - 1
    kernel = functools.partial(_ffn_kernel, tile0=tile0, n_ctx_tiles=n_ctx_tiles, n_tiles=nt, final_norm=final_norm)
    return pl.pallas_call(
        kernel,
        out_shape=jax.ShapeDtypeStruct((bsz, s - tile0 * t, d), F32),
        grid=(bsz, nt - tile0),
        in_specs=[pl.BlockSpec((None, t, d), lambda b, i: (b, i + tile0, 0)),
                  pl.BlockSpec((None, FFN_HALO, d), lambda b, i: (b, jnp.maximum((i + tile0) * hb - 1, 0), 0)),
                  pl.BlockSpec((None, FFN_HALO, d), lambda b, i: (b, jnp.minimum((i + tile0 + 1) * hb, last_hb), 0)),
                  pl.BlockSpec((None, None, N_MOD, d),
                               lambda b, i: (b, (i + tile0 >= n_ctx_tiles).astype(jnp.int32), 0, 0)),
                  _resident(gain.shape), _resident(wu.shape), _resident(cw.shape), _resident(wd.shape),
                  _resident(final_gain.shape)],
        out_specs=pl.BlockSpec((None, t, d), lambda b, i: (b, i, 0)),
        scratch_shapes=[pltpu.VMEM((t + 2 * FFN_HALO, d), BF16),
                        pltpu.VMEM((t + 2 * FFN_HALO, wu.shape[1]), F32),
                        pltpu.VMEM((t, wd.shape[0]), BF16)],
        compiler_params=_params("parallel", "parallel"),
        name="conv_ffn_final" if final_norm else "conv_ffn",
    )(xx, xx, xx, mods, gain, wu, cw, wd, final_gain)


def _odd_mods_spec(bsz, n_ctx_blocks, d):
    return pl.BlockSpec((bsz, None, N_MOD, d), lambda j: (0, (j >= n_ctx_blocks).astype(jnp.int32), 0, 0))


def _odd_in_kernel(x_ref, mod_ref, gain_ref, wg_ref, wr_ref, gate_ref, rec_ref, slab_ref):
    bsz, tt, d = x_ref.shape
    w = wr_ref.shape[1]
    h = _norm_mod(x_ref[...], gain_ref[...], mod_ref[:, 0:1, :], mod_ref[:, 1:2, :])
    h = h.reshape(bsz * tt, d).astype(BF16)
    gate_ref[...] = jax.nn.gelu(_dot(h, wg_ref[...]), approximate=True).astype(BF16).reshape(bsz, tt, w)
    rec = _dot(h, wr_ref[...])
    for b in range(bsz):
        for sl in range(w // LANES):
            slab_ref[sl, b * TM_PITCH:b * TM_PITCH + tt, :] = rec[b * tt:(b + 1) * tt, sl * LANES:(sl + 1) * LANES]
    for t in range(tt):
        for sl in range(w // LANES):
            rec_ref[t, :, sl * LANES:(sl + 1) * LANES] = slab_ref[sl, pl.ds(t, bsz, stride=TM_PITCH), :]


def odd_in_proj(xx, mods, gain, wg, wr, n_ctx_rows):
    bsz, s, d = xx.shape
    tt = LRU_TT
    w = wg.shape[1]
    assert tt <= TM_PITCH and TM_PITCH % 8 == 0
    return pl.pallas_call(
        _odd_in_kernel,
        out_shape=[jax.ShapeDtypeStruct((bsz, s, w), BF16), jax.ShapeDtypeStruct((s, bsz, w), F32)],
        grid=(s // tt,),
        in_specs=[pl.BlockSpec((bsz, tt, d), lambda j: (0, j, 0)), _odd_mods_spec(bsz, n_ctx_rows // tt, d),
                  _resident(gain.shape), _resident(wg.shape), _resident(wr.shape)],
        out_specs=[pl.BlockSpec((bsz, tt, w), lambda j: (0, j, 0)),
                   pl.BlockSpec((tt, bsz, w), lambda j: (j, 0, 0))],
        scratch_shapes=[pltpu.VMEM((w // LANES, bsz * TM_PITCH, LANES), F32)],
        compiler_params=_params("parallel"),
        name="odd_in_proj",
    )(xx, mods, gain, wg, wr)


def _lru_kernel(uf_ref, ufp_ref, ufn_ref, ur_ref, urp_ref, urn_ref, cw_ref, wax_ref, bax_ref, lam_ref,
                hf_ref, hr_ref, ue_ref, a_ref, x_ref, hst_ref, *, n_ctx_blocks, n_blocks):
    j = pl.program_id(0)
    tt, bsz, w = uf_ref.shape

    @pl.when(j == 0)
    def _():
        hst_ref[...] = jnp.zeros_like(hst_ref)

    rev_blk = jnp.where(j < n_ctx_blocks, n_ctx_blocks - 1 - j, n_blocks - 1 - (j - n_ctx_blocks))
    for d, (blk, u_ref, up_ref, un_ref, o_ref) in enumerate(
            ((j, uf_ref, ufp_ref, ufn_ref, hf_ref), (rev_blk, ur_ref, urp_ref, urn_ref, hr_ref))):
        has_prev = jnp.logical_and(blk != 0, blk != n_ctx_blocks)
        has_next = jnp.logical_and(blk != n_ctx_blocks - 1, blk != n_blocks - 1)
        ue_ref[0:1] = jnp.where(has_prev, up_ref[...], 0.0)
        ue_ref[1:tt + 1] = u_ref[...]
        ue_ref[tt + 1:tt + 3] = jnp.where(has_next, un_ref[...], 0.0)
        cw = cw_ref[...]
        uc = cw[0:1] * ue_ref[0:tt] + cw[1:2] * ue_ref[1:tt + 1] + cw[2:3] * ue_ref[2:tt + 2] + cw[3:4] * ue_ref[3:tt + 3]
        uc = uc.reshape(tt * bsz, w)
        half_c = (-0.5 * LRU_C) * _softplus(-lam_ref[d])
        for hh in range(LRU_HEADS):
            cols = slice(hh * LRU_HEAD_DIM, (hh + 1) * LRU_HEAD_DIM)
            uh = uc[:, cols]
            th = jnp.tanh(_dot(uh.astype(BF16), wax_ref[d, hh]) + bax_ref[d, hh])
            log_a = th[:, 0:LRU_HEAD_DIM] * half_c[:, cols] + half_c[:, cols]
            ig = 0.5 * th[:, LRU_HEAD_DIM:2 * LRU_HEAD_DIM] + 0.5
            a = jnp.exp(log_a)
            m2 = jnp.tanh(log_a) * (-1.0 - a * a)
            mult = jnp.where(m2 > 0.0, m2 * lax.rsqrt(m2), 0.0)
            a_ref[:, :, cols] = a.reshape(tt, bsz, LRU_HEAD_DIM)
            x_ref[:, :, cols] = (mult * ig * uh).reshape(tt, bsz, LRU_HEAD_DIM)
        h = hst_ref[d]
        for step in range(tt):
            tcur = tt - 1 - step if d == 1 else step
            h = a_ref[tcur] * h + x_ref[tcur]
            o_ref[tcur] = h
        hst_ref[d] = h


def lru_scan(u, cw, wax, bax, lam, n_ctx_rows):
    s, bsz, w = u.shape
    tt = LRU_TT
    nb = s // tt
    ncb = n_ctx_rows // tt

    def rev_blk(j):
        return jnp.where(j < ncb, ncb - 1 - j, nb - 1 - (j - ncb))

    def specs(blk):
        return [pl.BlockSpec((tt, bsz, w), lambda j: (blk(j), 0, 0)),
                pl.BlockSpec((1, bsz, w), lambda j: (jnp.maximum(blk(j) * tt - 1, 0), 0, 0)),
                pl.BlockSpec((2, bsz, w), lambda j: (jnp.minimum((blk(j) + 1) * (tt // 2), s // 2 - 1), 0, 0))]

    fwd_blk = lambda j: j
    kernel = functools.partial(_lru_kernel, n_ctx_blocks=ncb, n_blocks=nb)
    return pl.pallas_call(
        kernel,
        out_shape=[jax.ShapeDtypeStruct((s, bsz, w), F32)] * 2,
        grid=(nb,),
        in_specs=specs(fwd_blk) + specs(rev_blk) + [_resident(cw.shape), _resident(wax.shape),
                                                     _resident(bax.shape), _resident(lam.shape)],
        out_specs=[pl.BlockSpec((tt, bsz, w), lambda j: (j, 0, 0)),
                   pl.BlockSpec((tt, bsz, w), lambda j: (rev_blk(j), 0, 0))],
        scratch_shapes=[pltpu.VMEM((tt + 3, bsz, w), F32), pltpu.VMEM((tt, bsz, w), F32),
                        pltpu.VMEM((tt, bsz, w), F32), pltpu.VMEM((2, bsz, w), F32)],
        compiler_params=_params("arbitrary"),
        name="lru_scan",
    )(u, u, u, u, u, u, cw, wax, bax, lam)


def _odd_out_kernel(x_ref, mod_ref, gate_ref, hf_ref, hr_ref, w_ref, o_ref, slab_ref):
    bsz, tt, d = x_ref.shape
    w = w_ref.shape[0]
    for t in range(tt):
        hs = hf_ref[t] + hr_ref[t]
        for sl in range(w // LANES):
            slab_ref[sl, pl.ds(t, bsz, stride=TM_PITCH), :] = hs[:, sl * LANES:(sl + 1) * LANES]
    rec = jnp.concatenate(
        [jnp.concatenate([slab_ref[sl, b * TM_PITCH:b * TM_PITCH + tt, :] for sl in range(w // LANES)], axis=1)
         for b in range(bsz)], axis=0)
    mixed = (gate_ref[...].reshape(bsz * tt, w) * rec).astype(BF16)
    y = _dot(mixed, w_ref[...]).reshape(bsz, tt, d)
    o_ref[...] = x_ref[...] + mod_ref[:, 2:3, :] * y


def odd_out_proj(xx, mods, gate, hf, hr, w_out, n_ctx_rows):
    bsz, s, d = xx.shape
    tt = LRU_TT
    w = gate.shape[-1]
    bmaj = lambda n: pl.BlockSpec((bsz, tt, n), lambda j: (0, j, 0))
    tmaj = pl.BlockSpec((tt, bsz, w), lambda j: (j, 0, 0))
    return pl.pallas_call(
        _odd_out_kernel,
        out_shape=jax.ShapeDtypeStruct((bsz, s, d), F32),
        grid=(s // tt,),
        in_specs=[bmaj(d), _odd_mods_spec(bsz, n_ctx_rows // tt, d), bmaj(w), tmaj, tmaj, _resident(w_out.shape)],
        out_specs=bmaj(d),
        scratch_shapes=[pltpu.VMEM((w // LANES, bsz * TM_PITCH, LANES), F32)],
        compiler_params=_params("parallel"),
        name="odd_out_proj",
    )(xx, mods, gate, hf, hr, w_out)


def _pad_heads(w, n_heads, head_dim):
    d = w.shape[0]
    w = w.reshape(d, n_heads, head_dim)
    return jnp.pad(w, ((0, 0), (0, 0), (0, LANES - head_dim))).reshape(d, n_heads * LANES)


def _rope_tables(n_ctx, n_lat):
    rows = n_lat // GRID_W
    row = jnp.repeat(jnp.arange(rows, dtype=F32), GRID_W)
    col = jnp.tile(jnp.arange(GRID_W, dtype=F32), rows)
    n_freq = A_HEAD_DIM // 4
    inv_freq = ROPE_THETA ** (-jnp.arange(n_freq, dtype=F32) / n_freq)
    ar = row[:, None] * inv_freq
    ac = col[:, None] * inv_freq
    cos = jnp.concatenate([jnp.cos(ar), jnp.cos(ar), jnp.cos(ac), jnp.cos(ac)], axis=-1)
    sin = jnp.concatenate([-jnp.sin(ar), jnp.sin(ar), -jnp.sin(ac), jnp.sin(ac)], axis=-1)
    cos = jnp.concatenate([jnp.ones((n_ctx, A_HEAD_DIM), F32), cos], axis=0)
    sin = jnp.concatenate([jnp.zeros((n_ctx, A_HEAD_DIM), F32), sin], axis=0)
    return jnp.tile(cos, (1, 2)), jnp.tile(sin, (1, 2))


def kernel(x, c, ctx, c_ctx, ada_w, ada_b, norm_mix, norm_ffn, ffn_w_up, ffn_conv, ffn_w_down, even_w_in, even_w_out, attn_q_gain, attn_k_gain, gla_gate_w_up, gla_gate_b, gla_out_gain, lru_w_in, lru_conv, lru_lambda, lru_w_a, lru_b_a, lru_w_x, lru_b_x, lru_w_out, final_gain):
    bsz, n_lat, d = x.shape
    n_ctx = ctx.shape[1]
    depth = ada_w.shape[0]
    d_ff = ffn_w_down.shape[1]
    assert n_ctx % ROW_TILE == 0 and n_lat % ROW_TILE == 0 and d_ff % FFN_CN == 0
    n_ctx_tiles = n_ctx // ROW_TILE
    s = n_ctx + n_lat

    xx = (ctx, x)
    cond_rows = -(-(bsz + 1) // 8) * 8
    cond = jnp.zeros((cond_rows, d), F32).at[:bsz].set(c).at[bsz].set(c_ctx)
    cos, sin = _rope_tables(n_ctx, n_lat)
    pad_gain = lambda g: jnp.pad(g, (0, LANES - A_HEAD_DIM)).reshape(1, LANES)

    for l in range(depth):
        last = l == depth - 1
        j = l // 2
        table = adaln_table(cond, ada_w, ada_b.reshape(depth, 1, N_MOD * d), l)
        m_lat = table[:bsz].reshape(bsz, N_MOD, d)
        m_ctx = jnp.broadcast_to(table[bsz].reshape(1, N_MOD, d), (bsz, N_MOD, d))
        mods = jnp.stack([m_ctx, m_lat], axis=1)
        gain_mix = norm_mix[l].reshape(1, d)

        if l % 2 == 0:
            w_in = even_w_in[j]
            o0 = A_HEADS * A_HEAD_DIM
            o1 = o0 + A_KV_HEADS * A_HEAD_DIM
            o2 = o1 + A_KV_HEADS * A_HEAD_DIM
            o3 = o2 + 2 * B_K + 2 * B_V
            wqkv = _pad_heads(w_in[:, :o2], A_HEADS + 2 * A_KV_HEADS, A_HEAD_DIM).astype(BF16)
            wb = w_in[:, o2:o3].astype(BF16)
            wlr = w_in[:, o3:].astype(BF16)
            zeros = jnp.zeros((B_GATE_RANK, B_K), F32)
            wgate = jnp.concatenate([jnp.concatenate([gla_gate_w_up[j, 0], zeros], axis=1),
                                     jnp.concatenate([zeros, gla_gate_w_up[j, 1]], axis=1)], axis=0)
            bgate = gla_gate_b[j].reshape(1, 2 * B_K)
            s_bound = A_HEAD_DIM ** 0.5 * jnp.max(jnp.abs(attn_q_gain[j])) * jnp.max(jnp.abs(attn_k_gain[j]))
            qaug = jnp.zeros((1, LANES), F32).at[0, A_HEAD_DIM].set(-s_bound * LOG2E)
            q, k, v, qb, kb, vb, gb, laf, lab = even_in_proj(
                xx, mods, gain_mix, wqkv, wb, wlr, wgate, bgate,
                pad_gain(attn_q_gain[j]), pad_gain(attn_k_gain[j]), qaug, cos, sin, n_ctx_tiles)
            a = attention(s_bound.reshape(1), q, k, v, n_ctx)
            o_f, o_r = gla_bidir(qb, kb, vb, laf, lab, n_ctx_tiles)
            w_out = even_w_out[j]
            xx = even_out_proj(xx, mods, a, o_f, o_r, gb, gla_out_gain[j].reshape(1, B_DV),
                               w_out[:o0].astype(BF16), w_out[o0:].astype(BF16), n_ctx_tiles)
        else:
            w_in = lru_w_in[j]
            gate, rec = odd_in_proj(xx, mods, gain_mix, w_in[:, :LRU_WIDTH].astype(BF16),
                                    w_in[:, LRU_WIDTH:].astype(BF16), n_ctx)
            wax = (0.5 * jnp.concatenate([lru_w_a[j], lru_w_x[j]], axis=-1)).astype(BF16)
            bax = 0.5 * jnp.concatenate([lru_b_a[j].reshape(2, LRU_HEADS, 1, LRU_HEAD_DIM),
                                         lru_b_x[j].reshape(2, LRU_HEADS, 1, LRU_HEAD_DIM)], axis=-1)
            hf, hr = lru_scan(rec, lru_conv[j], wax, bax, lru_lambda[j].reshape(2, 1, LRU_WIDTH), n_ctx)
            xx = odd_out_proj(xx, mods, gate, hf, hr, lru_w_out[j].astype(BF16), n_ctx)

        xx = conv_ffn(xx, mods, norm_ffn[l].reshape(1, d), ffn_w_up[l].astype(BF16), ffn_conv[l],
                      ffn_w_down[l].astype(BF16), final_gain.reshape(1, d),
                      n_ctx_tiles, latents_only=last, final_norm=last)
    return xx
```

```python
import functools

import numpy as np
import jax
import jax.numpy as jnp
from jax import lax
from jax.experimental import pallas as pl
from jax.experimental.pallas import tpu as pltpu

F32 = jnp.float32
BF16 = jnp.bfloat16

NORM_EPS = 1e-6
N_MOD = 6
GRID_W = 64
ROPE_THETA = 10000.0

A_HEADS = 8
A_KV_HEADS = 2
A_GROUP = A_HEADS // A_KV_HEADS
A_HEAD_DIM = 64

B_HEADS = 4
B_DK = 64
B_DV = 128
B_K = B_HEADS * B_DK
B_V = B_HEADS * B_DV
B_GATE_RANK = 16
B_GATE_TAU = 16.0
GLA_CHUNK = 64
GLA_SUB = 16

LRU_HEADS = 10
LRU_HEAD_DIM = 128
LRU_WIDTH = LRU_HEADS * LRU_HEAD_DIM
LRU_C = 8.0

LANES = 128
ROW_TILE = 256
ATT_TQ = 256
ATT_TK = 256
FFN_CN = 256
FFN_HALO = 8
FFN_LATENT_TILE = 512
LRU_TT = 32
TM_PITCH = 40
VMEM_LIMIT = 56 * 1024 * 1024

LOG2E = float(np.log2(np.e))
ATT_MAX_STATIC_SHIFT = 40.0
ATT_AUG_LANES = (32, 0)

NT_DIMS = (((1,), (1,)), ((), ()))
TN_DIMS = (((0,), (0,)), ((), ()))


def _params(*sem):
    return pltpu.CompilerParams(dimension_semantics=sem, vmem_limit_bytes=VMEM_LIMIT)


def _resident(shape):
    nd = len(shape)
    return pl.BlockSpec(shape, lambda *_: (0,) * nd, pipeline_mode=pl.Buffered(1))


def _dot(a, b):
    return jnp.dot(a, b, preferred_element_type=F32)


def _split_bf16(a):
    hi = a.astype(BF16)
    lo = (a - hi.astype(F32)).astype(BF16)
    return hi, lo


def _dot_f32(a, b):
    ah, al = _split_bf16(a)
    bh, bl = _split_bf16(b)
    return _dot(ah, bh) + _dot(ah, bl) + _dot(al, bh)


def _sigmoid(x):
    return 0.5 * jnp.tanh(0.5 * x) + 0.5


def _silu(x):
    return x * _sigmoid(x)


def _softplus(x):
    return jnp.maximum(x, 0.0) + jnp.log1p(jnp.exp(-jnp.abs(x)))


def _norm_mod(x, gain, shift, scale):
    ms = jnp.mean(x * x, axis=-1, keepdims=True)
    return (x * lax.rsqrt(ms + NORM_EPS) * gain) * (1.0 + scale) + shift


def _adaln_kernel(c_ref, w_ref, b_ref, o_ref):
    o_ref[...] = _dot_f32(_silu(c_ref[...]), w_ref[...]) + b_ref[...]


def adaln_table(cond, w, b, layer):
    rows, d = cond.shape
    n = w.shape[2]
    tn = 768
    return pl.pallas_call(
        _adaln_kernel,
        out_shape=jax.ShapeDtypeStruct((rows, n), F32),
        grid=(n // tn,),
        in_specs=[pl.BlockSpec((rows, d), lambda j: (0, 0)),
                  pl.BlockSpec((None, d, tn), lambda j: (layer, 0, j)),
                  pl.BlockSpec((None, 1, tn), lambda j: (layer, 0, j))],
        out_specs=pl.BlockSpec((rows, tn), lambda j: (0, j)),
        compiler_params=_params("arbitrary"),
        name="adaln_table",
    )(cond, w, b)


def _mods_spec(n_ctx_tiles, d):
    return pl.BlockSpec((None, None, N_MOD, d),
                        lambda b, i: (b, (i >= n_ctx_tiles).astype(jnp.int32), 0, 0))


def _stream_specs(stream, n_ctx_tiles, t):
    if isinstance(stream, tuple):
        ctx_arr, lat_arr = stream
        lat_off = 0
    else:
        ctx_arr = lat_arr = stream
        lat_off = n_ctx_tiles
    d = ctx_arr.shape[-1]
    ctx_spec = pl.BlockSpec((None, t, d), lambda b, i: (b, jnp.minimum(i, n_ctx_tiles - 1), 0))
    lat_spec = pl.BlockSpec((None, t, d), lambda b, i: (b, jnp.maximum(i - n_ctx_tiles, 0) + lat_off, 0))
    return (ctx_arr, lat_arr), [ctx_spec, lat_spec]


def _stream_tile(xc_ref, xl_ref, n_ctx_tiles):
    return jnp.where(pl.program_id(1) < n_ctx_tiles, xc_ref[...], xl_ref[...])


def _even_in_kernel(xc_ref, xl_ref, mod_ref, gain_ref, wqkv_ref, wb_ref, wlr_ref, wgate_ref, bgate_ref,
                    qg_ref, kg_ref, qaug_ref, cos_ref, sin_ref,
                    q_ref, k_ref, v_ref, qb_ref, kb_ref, vb_ref, gb_ref, laf_ref, lab_ref, *, n_ctx_tiles):
    x = _stream_tile(xc_ref, xl_ref, n_ctx_tiles)
    h = _norm_mod(x, gain_ref[...], mod_ref[0:1, :], mod_ref[1:2, :]).astype(BF16)
    cos = cos_ref[...]
    sin = sin_ref[...]
    lane = lax.broadcasted_iota(jnp.int32, cos.shape, 1)
    first = lane % (LANES // 2) < LANES // 4
    low = lane < A_HEAD_DIM

    def pair_norm_rope(xp, g):
        sq = xp * xp
        both = jnp.sum(sq, axis=-1, keepdims=True)
        ms_first = jnp.sum(jnp.where(first, sq, 0.0), axis=-1, keepdims=True)
        inv = jnp.where(first, lax.rsqrt(ms_first * (1.0 / A_HEAD_DIM) + NORM_EPS),
                        lax.rsqrt((both - ms_first) * (1.0 / A_HEAD_DIM) + NORM_EPS))
        y = xp * inv * g
        return y * cos + pltpu.roll(y, LANES // 2, 1) * sin

    qkv = _dot(h, wqkv_ref[...])
    slab = lambda i: qkv[:, i * LANES:(i + 1) * LANES]
    n_pairs = A_HEADS // 2
    for m in range(n_pairs):
        qp = pair_norm_rope(slab(m), qg_ref[...]) * (A_HEAD_DIM ** -0.5 * LOG2E)
        q_ref[m] = (jnp.where(first, qp, 0.0) + qaug_ref[0:1, :]).astype(BF16)
        q_ref[m + n_pairs] = (jnp.where(first, 0.0, qp) + qaug_ref[1:2, :]).astype(BF16)
    kp = pair_norm_rope(slab(n_pairs), kg_ref[...])
    k_ref[0] = (jnp.where(first, kp, 0.0) + (lane == ATT_AUG_LANES[0]).astype(F32)).astype(BF16)
    k_ref[1] = (jnp.where(first, 0.0, kp) + (lane == ATT_AUG_LANES[1]).astype(F32)).astype(BF16)
    vp = slab(n_pairs + 1)
    v_ref[0] = jnp.where(low, vp, 1.0).astype(BF16)
    v_ref[1] = jnp.where(low, pltpu.roll(vp, A_HEAD_DIM, 1), 1.0).astype(BF16)

    pb = _dot(h, wb_ref[...])
    qb_ref[...] = pb[:, 0:B_K] * B_DK ** -0.5
    kb_ref[...] = pb[:, B_K:2 * B_K]
    vb_ref[...] = pb[:, 2 * B_K:2 * B_K + B_V]
    gb_ref[...] = pb[:, 2 * B_K + B_V:2 * B_K + 2 * B_V].astype(BF16)

    lr = _dot(h, wlr_ref[...])
    z = _dot_f32(lr, wgate_ref[...]) + bgate_ref[...]
    la = -_softplus(-z) * (1.0 / B_GATE_TAU)
    laf_ref[...] = la[:, 0:B_K]
    lab_ref[...] = la[:, B_K:2 * B_K]


def even_in_proj(stream, mods, gain, wqkv, wb, wlr, wgate, bgate, qg, kg, qaug, cos, sin, n_ctx_tiles):
    t = ROW_TILE
    (xc, xl), x_specs = _stream_specs(stream, n_ctx_tiles, t)
    bsz, d = xc.shape[0], xc.shape[-1]
    s = cos.shape[0]
    row = lambda w: pl.BlockSpec((None, t, w), lambda b, i: (b, i, 0))
    heads = lambda n: pl.BlockSpec((None, n, t, LANES), lambda b, i: (b, 0, i, 0))
    tab = pl.BlockSpec((t, LANES), lambda b, i: (i, 0))
    out_shape = [
        jax.ShapeDtypeStruct((bsz, A_HEADS, s, LANES), BF16),
        jax.ShapeDtypeStruct((bsz, A_KV_HEADS, s, LANES), BF16),
        jax.ShapeDtypeStruct((bsz, A_KV_HEADS, s, LANES), BF16),
        jax.ShapeDtypeStruct((bsz, s, B_K), F32),
        jax.ShapeDtypeStruct((bsz, s, B_K), F32),
        jax.ShapeDtypeStruct((bsz, s, B_V), F32),
        jax.ShapeDtypeStruct((bsz, s, B_V), BF16),
        jax.ShapeDtypeStruct((bsz, s, B_K), F32),
        jax.ShapeDtypeStruct((bsz, s, B_K), F32),
    ]
    return pl.pallas_call(
        functools.partial(_even_in_kernel, n_ctx_tiles=n_ctx_tiles),
        out_shape=out_shape,
        grid=(bsz, s // t),
        in_specs=x_specs + [_mods_spec(n_ctx_tiles, d), _resident(gain.shape),
                  _resident(wqkv.shape), _resident(wb.shape), _resident(wlr.shape),
                  _resident(wgate.shape), _resident(bgate.shape), _resident(qg.shape), _resident(kg.shape),
                  _resident(qaug.shape), tab, tab],
        out_specs=[heads(A_HEADS), heads(A_KV_HEADS), heads(A_KV_HEADS),
                   row(B_K), row(B_K), row(B_V), row(B_V), row(B_K), row(B_K)],
        compiler_params=_params("parallel", "parallel"),
        name="even_in_proj",
    )(xc, xl, mods, gain, wqkv, wb, wlr, wgate, bgate, qg, kg, qaug, cos, sin)


def _attn_kernel(shift_ref, q_ref, k_ref, v_ref, o_ref, s_ref, p_ref, m_ref, *, n_ctx_q_tiles, n_ctx_rows):
    i = pl.program_id(2)
    tq = q_ref.shape[1]
    rows = A_GROUP * tq
    n_keys = k_ref.shape[0]

    def finish(acc):
        out = acc / acc[:, A_HEAD_DIM:A_HEAD_DIM + 1]
        low = lax.broadcasted_iota(jnp.int32, (tq, LANES), 1) < A_HEAD_DIM
        for j in range(0, A_GROUP, 2):
            even = out[j * tq:(j + 1) * tq]
            odd = pltpu.roll(out[(j + 1) * tq:(j + 2) * tq], A_HEAD_DIM, 1)
            o_ref[:, (j // 2) * LANES:(j // 2 + 1) * LANES] = jnp.where(low, even, odd).astype(BF16)

    def attend_static_shift(nk):
        qs = q_ref[...].reshape(rows, LANES)
        acc = jnp.zeros((rows, LANES), F32)
        for c0 in range(0, nk, ATT_TK):
            s = lax.dot_general(qs, k_ref[c0:c0 + ATT_TK, :], NT_DIMS, preferred_element_type=F32)
            acc = acc + _dot(jnp.exp2(s).astype(BF16), v_ref[c0:c0 + ATT_TK, :])
        finish(acc)

    def attend_row_max(nk):
        qs = q_ref[...].reshape(rows, LANES)
        m_ref[...] = jnp.full(m_ref.shape, -jnp.inf, F32)
        for c0 in range(0, nk, ATT_TK):
            cols = slice(c0, c0 + ATT_TK)
            s = lax.dot_general(qs, k_ref[cols, :], NT_DIMS, preferred_element_type=F32)
            s_ref[:, cols] = s
            m_ref[...] = jnp.maximum(m_ref[...], jnp.maximum(s[:, 0:LANES], s[:, LANES:2 * LANES]))
        m = jnp.max(m_ref[...], axis=1, keepdims=True)
        for c0 in range(0, nk, ATT_TK):
            cols = slice(c0, c0 + ATT_TK)
            p_ref[:, cols] = jnp.exp2(s_ref[:, cols] - m).astype(BF16)
        finish(_dot(p_ref[:, 0:nk], v_ref[0:nk, :]))

    is_ctx = i < n_ctx_q_tiles
    static_ok = shift_ref[0] <= ATT_MAX_STATIC_SHIFT
    for ctx_tile, nk in ((True, n_ctx_rows), (False, n_keys)):
        tile_match = is_ctx if ctx_tile else jnp.logical_not(is_ctx)

        @pl.when(jnp.logical_and(tile_match, static_ok))
        def _():
            attend_static_shift(nk)

        @pl.when(jnp.logical_and(tile_match, jnp.logical_not(static_ok)))
        def _():
            attend_row_max(nk)


def attention(shift, q, k, v, n_ctx_rows):
    bsz, _, s, _ = q.shape
    tq = ATT_TQ
    assert ATT_TK == 2 * LANES and n_ctx_rows % ATT_TK == 0 and s % ATT_TK == 0
    kernel = functools.partial(_attn_kernel, n_ctx_q_tiles=n_ctx_rows // tq, n_ctx_rows=n_ctx_rows)
    kv_spec = pl.BlockSpec((None, None, s, LANES), lambda b, g, i: (b, g, 0, 0))
    return pl.pallas_call(
        kernel,
        out_shape=jax.ShapeDtypeStruct((bsz, s, A_HEADS * A_HEAD_DIM), BF16),
        grid=(bsz, A_KV_HEADS, s // tq),
        in_specs=[pl.BlockSpec(memory_space=pltpu.SMEM),
                  pl.BlockSpec((None, A_GROUP, tq, LANES), lambda b, g, i: (b, g, i, 0)), kv_spec, kv_spec],
        out_specs=pl.BlockSpec((None, tq, A_GROUP * A_HEAD_DIM), lambda b, g, i: (b, i, g)),
        scratch_shapes=[pltpu.VMEM((A_GROUP * tq, s), F32),
                        pltpu.VMEM((A_GROUP * tq, s), BF16),
                        pltpu.VMEM((A_GROUP * tq, LANES), F32)],
        compiler_params=_params("parallel", "parallel", "arbitrary"),
        name="gqa_attention",
    )(shift, q, k, v)


def _gla_consts(reverse):
    c, sub = GLA_CHUNK, GLA_SUB
    t = lax.broadcasted_iota(jnp.int32, (c, c), 0)
    s = lax.broadcasted_iota(jnp.int32, (c, c), 1)
    if reverse:
        tri = s >= t
        blk = s >= (t // sub + 1) * sub
    else:
        tri = s <= t
        blk = s < (t // sub) * sub
    sums = jnp.concatenate([tri, blk], axis=0).astype(BF16)
    sums = jnp.concatenate([sums, sums], axis=1)
    n_rows, n_cols = B_HEADS * sub, B_HEADS * c
    row = lax.broadcasted_iota(jnp.int32, (n_rows, n_cols), 0)
    col = lax.broadcasted_iota(jnp.int32, (n_rows, n_cols), 1)
    same_head = row // sub == col // c
    sub_masks = []
    for i in range(c // sub):
        rt = i * sub + row % sub
        causal = (col % c >= rt) if reverse else (col % c <= rt)
        sub_masks.append(jnp.logical_and(same_head, causal))
    lane = lax.broadcasted_iota(jnp.int32, (1, B_K), 1)
    head_masks = [(lane // B_DK == hh).astype(F32) for hh in range(B_HEADS)]
    key_row = lax.broadcasted_iota(jnp.int32, (c, 1), 0)
    return sums, sub_masks, head_masks, key_row


def _gla_decays(la, consts, reverse):
    sums = consts[0]
    c = GLA_CHUNK
    cr = _dot(sums, jnp.concatenate(_split_bf16(la), axis=0))
    cum = cr[0:c]
    ref = cr[c:2 * c]
    edge = 0 if reverse else c - 1
    return cum, ref, cum[edge:edge + 1]


def _gla_operands(q, k, v, cum, ref, total, consts, reverse):
    _, _, head_masks, key_row = consts
    c, sub = GLA_CHUNK, GLA_SUB

    def stack_heads(x):
        return jnp.concatenate([x * hm for hm in head_masks], axis=0)

    q_in = stack_heads(q * jnp.exp(cum)).astype(BF16)
    q_loc = stack_heads(q * jnp.exp(cum - ref)).astype(BF16)
    k_out = stack_heads(k * jnp.exp(total - cum)).astype(BF16)
    k_sub = []
    for i in range(c // sub):
        ref_i = ref[i * sub:i * sub + 1]
        valid = (key_row >= i * sub) if reverse else (key_row < (i + 1) * sub)
        k_sub.append((k * jnp.exp(jnp.where(valid, ref_i - cum, -jnp.inf))).astype(BF16))
    v_st = jnp.concatenate([v[:, hh * B_DV:(hh + 1) * B_DV] for hh in range(B_HEADS)], axis=0).astype(BF16)
    return q_in, q_loc, k_out, k_sub, v_st


def _gla_local(q_loc, k_out, k_sub, v_st, consts):
    sub_masks = consts[1]
    c, sub = GLA_CHUNK, GLA_SUB
    pieces = [[None] * (c // sub) for _ in range(B_HEADS)]
    for i in range(c // sub):
        q_i = jnp.concatenate([q_loc[hh * c + i * sub:hh * c + (i + 1) * sub] for hh in range(B_HEADS)], axis=0)
        s_i = lax.dot_general(q_i, jnp.concatenate([k_sub[i]] * B_HEADS, axis=0), NT_DIMS,
                              preferred_element_type=F32)
        s_i = jnp.where(sub_masks[i], s_i, 0.0)
        for hh in range(B_HEADS):
            pieces[hh][i] = s_i[hh * sub:(hh + 1) * sub]
    scores = jnp.concatenate([p for head in pieces for p in head], axis=0).astype(BF16)
    o_local = _dot(scores, v_st)
    st_inc = lax.dot_general(v_st, k_out, TN_DIMS, preferred_element_type=F32)
    return o_local, st_inc


def _gla_kernel(qf_ref, kf_ref, vf_ref, laf_ref, qr_ref, kr_ref, vr_ref, lar_ref,
                of_ref, or_ref, stf_ref, str_ref):
    @pl.when(pl.program_id(1) == 0)
    def _():
        stf_ref[...] = jnp.zeros_like(stf_ref)
        str_ref[...] = jnp.zeros_like(str_ref)

    c = GLA_CHUNK
    n_chunks = qf_ref.shape[0] // c
    dirs = ((False, qf_ref, kf_ref, vf_ref, laf_ref, of_ref, stf_ref),
            (True, qr_ref, kr_ref, vr_ref, lar_ref, or_ref, str_ref))
    consts = {rev: _gla_consts(rev) for rev in (False, True)}
    units = [(d, n) for n in range(n_chunks) for d in range(2)]
    rows = lambda n: slice(n * c, (n + 1) * c)

    decays = {}
    for d, n in units:
        rev, la_ref = dirs[d][0], dirs[d][4]
        decays[d, n] = _gla_decays(la_ref[rows(n), :], consts[rev], rev)
    operands = {}
    for d, n in units:
        rev, q_ref, k_ref, v_ref = dirs[d][:4]
        operands[d, n] = _gla_operands(q_ref[rows(n), :], k_ref[rows(n), :], v_ref[rows(n), :],
                                       *decays[d, n], consts[rev], rev)
    local = {}
    for d, n in units:
        _, q_loc, k_out, k_sub, v_st = operands[d, n]
        local[d, n] = _gla_local(q_loc, k_out, k_sub, v_st, consts[dirs[d][0]])

    for d in range(2):
        rev, o_ref, st_ref = dirs[d][0], dirs[d][5], dirs[d][6]
        st = st_ref[...]
        for n in (range(n_chunks - 1, -1, -1) if rev else range(n_chunks)):
            o_local, st_inc = local[d, n]
            q_in = operands[d, n][0]
            o = o_local + lax.dot_general(q_in, st.astype(BF16), NT_DIMS, preferred_element_type=F32)
            st = st * jnp.exp(decays[d, n][2]) + st_inc
            for hh in range(B_HEADS):
                o_ref[rows(n), hh * B_DV:(hh + 1) * B_DV] = o[hh * c:(hh + 1) * c].astype(BF16)
        st_ref[...] = st


def gla_bidir(qb, kb, vb, laf, lab, n_ctx_tiles):
    bsz, s, _ = qb.shape
    t = ROW_TILE
    nt = s // t

    def rev_tile(j):
        return jnp.where(j < n_ctx_tiles, n_ctx_tiles - 1 - j, nt - 1 - (j - n_ctx_tiles))

    fwd = lambda w: pl.BlockSpec((None, t, w), lambda b, j: (b, j, 0))
    rev = lambda w: pl.BlockSpec((None, t, w), lambda b, j: (b, rev_tile(j), 0))
    return pl.pallas_call(
        _gla_kernel,
        out_shape=[jax.ShapeDtypeStruct((bsz, s, B_V), BF16)] * 2,
        grid=(bsz, nt),
        in_specs=[fwd(B_K), fwd(B_K), fwd(B_V), fwd(B_K), rev(B_K), rev(B_K), rev(B_V), rev(B_K)],
        out_specs=[fwd(B_V), rev(B_V)],
        scratch_shapes=[pltpu.VMEM((B_DV, B_K), F32)] * 2,
        compiler_params=_params("parallel", "arbitrary"),
        name="gla_bidir",
    )(qb, kb, vb, laf, qb, kb, vb, lab)


def _even_out_kernel(xc_ref, xl_ref, mod_ref, a_ref, of_ref, or_ref, g_ref, og_ref, wa_ref, wb_ref, o_ref, *,
                     n_ctx_tiles):
    o = of_ref[...].astype(F32) + or_ref[...].astype(F32)
    g = g_ref[...].astype(F32)
    parts = []
    for hh in range(B_HEADS):
        oh = o[:, hh * B_DV:(hh + 1) * B_DV]
        ms = jnp.mean(oh * oh, axis=-1, keepdims=True)
        y = oh * lax.rsqrt(ms + NORM_EPS) * og_ref[...]
        parts.append((y * _silu(g[:, hh * B_DV:(hh + 1) * B_DV])).astype(BF16))
    gla = jnp.concatenate(parts, axis=1)
    y = _dot(a_ref[...], wa_ref[...]) + _dot(gla, wb_ref[...])
    o_ref[...] = _stream_tile(xc_ref, xl_ref, n_ctx_tiles) + mod_ref[2:3, :] * y


def even_out_proj(stream, mods, a, o_f, o_r, gb, o_gain, wa, wb, n_ctx_tiles):
    t = ROW_TILE
    (xc, xl), x_specs = _stream_specs(stream, n_ctx_tiles, t)
    bsz, s, d = a.shape[0], a.shape[1], xc.shape[-1]
    row = lambda w: pl.BlockSpec((None, t, w), lambda b, i: (b, i, 0))
    return pl.pallas_call(
        functools.partial(_even_out_kernel, n_ctx_tiles=n_ctx_tiles),
        out_shape=jax.ShapeDtypeStruct((bsz, s, d), F32),
        grid=(bsz, s // t),
        in_specs=x_specs + [_mods_spec(n_ctx_tiles, d), row(a.shape[-1]), row(B_V), row(B_V), row(B_V),
                            _resident(o_gain.shape), _resident(wa.shape), _resident(wb.shape)],
        out_specs=row(d),
        compiler_params=_params("parallel", "parallel"),
        name="even_out_proj",
    )(xc, xl, mods, a, o_f, o_r, gb, o_gain, wa, wb)


def _ffn_kernel(x_ref, xp_ref, xn_ref, mod_ref, gain_ref, wu_ref, cw_ref, wd_ref, fg_ref, o_ref,
                h_ref, u_ref, act_ref, *, tile0, n_ctx_tiles, n_tiles, final_norm):
    ti = pl.program_id(1) + tile0
    t = x_ref.shape[0]
    d_ff = wd_ref.shape[0]
    gain = gain_ref[...]
    shift = mod_ref[3:4, :]
    scale = mod_ref[4:5, :]
    x = x_ref[...]
    has_prev = jnp.logical_and(ti != 0, ti != n_ctx_tiles)
    has_next = jnp.logical_and(ti != n_ctx_tiles - 1, ti != n_tiles - 1)
    hp = jnp.where(has_prev, _norm_mod(xp_ref[...], gain, shift, scale), 0.0)
    hn = jnp.where(has_next, _norm_mod(xn_ref[...], gain, shift, scale), 0.0)
    h_ref[...] = jnp.concatenate([hp, _norm_mod(x, gain, shift, scale), hn], axis=0).astype(BF16)
    lo = FFN_HALO - 1

    def conv(cols):
        cw = cw_ref[:, cols]
        return (cw[0:1] * u_ref[lo:lo + t, cols] + cw[1:2] * u_ref[lo + 1:lo + 1 + t, cols]
                + cw[2:3] * u_ref[lo + 2:lo + 2 + t, cols])

    for c0 in range(0, d_ff, FFN_CN):
        gate_cols = slice(c0, c0 + FFN_CN)
        val_cols = slice(d_ff + c0, d_ff + c0 + FFN_CN)
        h = h_ref[...]
        u_ref[:, gate_cols] = _dot(h, wu_ref[:, gate_cols])
        u_ref[:, val_cols] = _dot(h, wu_ref[:, val_cols])
        act_ref[:, gate_cols] = (_silu(conv(gate_cols)) * conv(val_cols)).astype(BF16)
    y = x + mod_ref[5:6, :] * _dot(act_ref[...], wd_ref[...])
    if final_norm:
        ms = jnp.mean(y * y, axis=-1, keepdims=True)
        y = y * lax.rsqrt(ms + NORM_EPS) * fg_ref[...]
    o_ref[...] = y


def conv_ffn(xx, mods, gain, wu, cw, wd, final_gain, n_ctx_tiles, latents_only, final_norm, t=ROW_TILE):
    bsz, s, d = xx.shape
    assert s % t == 0 and t % FFN_HALO == 0
    nt = s // t
    tile0 = n_ctx_tiles if latents_only else 0
    hb = t // FFN_HALO
    last_hb = s // FFN_HALO - 1
    kernel = functools.partial(_ffn_kernel, tile0=tile0, n_ctx_tiles=n_ctx_tiles, n_tiles=nt, final_norm=final_norm)
    return pl.pallas_call(
        kernel,
        out_shape=jax.ShapeDtypeStruct((bsz, s - tile0 * t, d), F32),
        grid=(bsz, nt - tile0),
        in_specs=[pl.BlockSpec((None, t, d), lambda b, i: (b, i + tile0, 0)),
                  pl.BlockSpec((None, FFN_HALO, d), lambda b, i: (b, jnp.maximum((i + tile0) * hb - 1, 0), 0)),
                  pl.BlockSpec((None, FFN_HALO, d), lambda b, i: (b, jnp.minimum((i + tile0 + 1) * hb, last_hb), 0)),
                  pl.BlockSpec((None, None, N_MOD, d),
                               lambda b, i: (b, (i + tile0 >= n_ctx_tiles).astype(jnp.int32), 0, 0)),
                  _resident(gain.shape), _resident(wu.shape), _resident(cw.shape), _resident(wd.shape),
                  _resident(final_gain.shape)],
        out_specs=pl.BlockSpec((None, t, d), lambda b, i: (b, i, 0)),
        scratch_shapes=[pltpu.VMEM((t + 2 * FFN_HALO, d), BF16),
                        pltpu.VMEM((t + 2 * FFN_HALO, wu.shape[1]), F32),
                        pltpu.VMEM((t, wd.shape[0]), BF16)],
        compiler_params=_params("parallel", "parallel"),
        name="conv_ffn_final" if final_norm else "conv_ffn",
    )(xx, xx, xx, mods, gain, wu, cw, wd, final_gain)


def _odd_mods_spec(bsz, n_ctx_blocks, d):
    return pl.BlockSpec((bsz, None, N_MOD, d), lambda j: (0, (j >= n_ctx_blocks).astype(jnp.int32), 0, 0))


def _odd_in_kernel(x_ref, mod_ref, gain_ref, wg_ref, wr_ref, gate_ref, rec_ref, slab_ref):
    bsz, tt, d = x_ref.shape
    w = wr_ref.shape[1]
    h = _norm_mod(x_ref[...], gain_ref[...], mod_ref[:, 0:1, :], mod_ref[:, 1:2, :])
    h = h.reshape(bsz * tt, d).astype(BF16)
    gate_ref[...] = jax.nn.gelu(_dot(h, wg_ref[...]), approximate=True).astype(BF16).reshape(bsz, tt, w)
    rec = _dot(h, wr_ref[...])
    for b in range(bsz):
        for sl in range(w // LANES):
            slab_ref[sl, b * TM_PITCH:b * TM_PITCH + tt, :] = rec[b * tt:(b + 1) * tt, sl * LANES:(sl + 1) * LANES]
    for t in range(tt):
        for sl in range(w // LANES):
            rec_ref[t, :, sl * LANES:(sl + 1) * LANES] = slab_ref[sl, pl.ds(t, bsz, stride=TM_PITCH), :]


def odd_in_proj(xx, mods, gain, wg, wr, n_ctx_rows):
    bsz, s, d = xx.shape
    tt = LRU_TT
    w = wg.shape[1]
    assert tt <= TM_PITCH and TM_PITCH % 8 == 0
    return pl.pallas_call(
        _odd_in_kernel,
        out_shape=[jax.ShapeDtypeStruct((bsz, s, w), BF16), jax.ShapeDtypeStruct((s, bsz, w), F32)],
        grid=(s // tt,),
        in_specs=[pl.BlockSpec((bsz, tt, d), lambda j: (0, j, 0)), _odd_mods_spec(bsz, n_ctx_rows // tt, d),
                  _resident(gain.shape), _resident(wg.shape), _resident(wr.shape)],
        out_specs=[pl.BlockSpec((bsz, tt, w), lambda j: (0, j, 0)),
                   pl.BlockSpec((tt, bsz, w), lambda j: (j, 0, 0))],
        scratch_shapes=[pltpu.VMEM((w // LANES, bsz * TM_PITCH, LANES), F32)],
        compiler_params=_params("parallel"),
        name="odd_in_proj",
    )(xx, mods, gain, wg, wr)


def _lru_kernel(uf_ref, ufp_ref, ufn_ref, ur_ref, urp_ref, urn_ref, cw_ref, wax_ref, bax_ref, lam_ref,
                hf_ref, hr_ref, ue_ref, a_ref, x_ref, hst_ref, *, n_ctx_blocks, n_blocks):
    j = pl.program_id(0)
    tt, bsz, w = uf_ref.shape

    @pl.when(j == 0)
    def _():
        hst_ref[...] = jnp.zeros_like(hst_ref)

    rev_blk = jnp.where(j < n_ctx_blocks, n_ctx_blocks - 1 - j, n_blocks - 1 - (j - n_ctx_blocks))
    for d, (blk, u_ref, up_ref, un_ref, o_ref) in enumerate(
            ((j, uf_ref, ufp_ref, ufn_ref, hf_ref), (rev_blk, ur_ref, urp_ref, urn_ref, hr_ref))):
        has_prev = jnp.logical_and(blk != 0, blk != n_ctx_blocks)
        has_next = jnp.logical_and(blk != n_ctx_blocks - 1, blk != n_blocks - 1)
        ue_ref[0:1] = jnp.where(has_prev, up_ref[...], 0.0)
        ue_ref[1:tt + 1] = u_ref[...]
        ue_ref[tt + 1:tt + 3] = jnp.where(has_next, un_ref[...], 0.0)
        cw = 0.5 * cw_ref[...]
        uc = cw[0:1] * ue_ref[0:tt] + cw[1:2] * ue_ref[1:tt + 1] + cw[2:3] * ue_ref[2:tt + 2] + cw[3:4] * ue_ref[3:tt + 3]
        uc = uc.reshape(tt * bsz, w)
        half_c = (-0.5 * LRU_C) * _softplus(-lam_ref[d])
        for hh in range(LRU_HEADS):
            cols = slice(hh * LRU_HEAD_DIM, (hh + 1) * LRU_HEAD_DIM)
            uh = uc[:, cols]
            th = jnp.tanh(_dot(uh.astype(BF16), wax_ref[d, hh]) + bax_ref[d, hh])
            log_a = th[:, 0:LRU_HEAD_DIM] * half_c[:, cols] + half_c[:, cols]
            a = jnp.exp(log_a)
            m2 = jnp.tanh(log_a) * (-1.0 - a * a)
            mult = m2 * lax.rsqrt(jnp.maximum(m2, jnp.finfo(F32).tiny))
            a_ref[:, :, cols] = a.reshape(tt, bsz, LRU_HEAD_DIM)
            x_ref[:, :, cols] = (mult * (th[:, LRU_HEAD_DIM:2 * LRU_HEAD_DIM] + 1.0) * uh).reshape(tt, bsz, LRU_HEAD_DIM)
        h = hst_ref[d]
        for step in range(tt):
            tcur = tt - 1 - step if d == 1 else step
            h = a_ref[tcur] * h + x_ref[tcur]
            o_ref[tcur] = h
        hst_ref[d] = h


def lru_scan(u, cw, wax, bax, lam, n_ctx_rows):
    s, bsz, w = u.shape
    tt = LRU_TT
    nb = s // tt
    ncb = n_ctx_rows // tt

    def rev_blk(j):
        return jnp.where(j < ncb, ncb - 1 - j, nb - 1 - (j - ncb))

    def specs(blk):
        return [pl.BlockSpec((tt, bsz, w), lambda j: (blk(j), 0, 0)),
                pl.BlockSpec((1, bsz, w), lambda j: (jnp.maximum(blk(j) * tt - 1, 0), 0, 0)),
                pl.BlockSpec((2, bsz, w), lambda j: (jnp.minimum((blk(j) + 1) * (tt // 2), s // 2 - 1), 0, 0))]

    fwd_blk = lambda j: j
    kernel = functools.partial(_lru_kernel, n_ctx_blocks=ncb, n_blocks=nb)
    return pl.pallas_call(
        kernel,
        out_shape=[jax.ShapeDtypeStruct((s, bsz, w), F32)] * 2,
        grid=(nb,),
        in_specs=specs(fwd_blk) + specs(rev_blk) + [_resident(cw.shape), _resident(wax.shape),
                                                     _resident(bax.shape), _resident(lam.shape)],
        out_specs=[pl.BlockSpec((tt, bsz, w), lambda j: (j, 0, 0)),
                   pl.BlockSpec((tt, bsz, w), lambda j: (rev_blk(j), 0, 0))],
        scratch_shapes=[pltpu.VMEM((tt + 3, bsz, w), F32), pltpu.VMEM((tt, bsz, w), F32),
                        pltpu.VMEM((tt, bsz, w), F32), pltpu.VMEM((2, bsz, w), F32)],
        compiler_params=_params("arbitrary"),
        name="lru_scan",
    )(u, u, u, u, u, u, cw, wax, bax, lam)


def _odd_out_kernel(x_ref, mod_ref, gate_ref, hf_ref, hr_ref, w_ref, o_ref, slab_ref):
    bsz, tt, d = x_ref.shape
    w = w_ref.shape[0]
    for t in range(tt):
        hs = hf_ref[t] + hr_ref[t]
        for sl in range(w // LANES):
            slab_ref[sl, pl.ds(t, bsz, stride=TM_PITCH), :] = hs[:, sl * LANES:(sl + 1) * LANES]
    rec = jnp.concatenate(
        [jnp.concatenate([slab_ref[sl, b * TM_PITCH:b * TM_PITCH + tt, :] for sl in range(w // LANES)], axis=1)
         for b in range(bsz)], axis=0)
    mixed = (gate_ref[...].reshape(bsz * tt, w) * rec).astype(BF16)
    y = _dot(mixed, w_ref[...]).reshape(bsz, tt, d)
    o_ref[...] = x_ref[...] + mod_ref[:, 2:3, :] * y


def odd_out_proj(xx, mods, gate, hf, hr, w_out, n_ctx_rows, latents_only):
    bsz, s, d = xx.shape
    tt = LRU_TT
    w = gate.shape[-1]
    j0 = n_ctx_rows // tt if latents_only else 0
    bmaj = lambda n: pl.BlockSpec((bsz, tt, n), lambda j: (0, j + j0, 0))
    tmaj = pl.BlockSpec((tt, bsz, w), lambda j: (j + j0, 0, 0))
    mods_spec = pl.BlockSpec((bsz, None, N_MOD, d),
                             lambda j: (0, (j + j0 >= n_ctx_rows // tt).astype(jnp.int32), 0, 0))
    return pl.pallas_call(
        _odd_out_kernel,
        out_shape=jax.ShapeDtypeStruct((bsz, s - j0 * tt, d), F32),
        grid=(s // tt - j0,),
        in_specs=[bmaj(d), mods_spec, bmaj(w), tmaj, tmaj, _resident(w_out.shape)],
        out_specs=pl.BlockSpec((bsz, tt, d), lambda j: (0, j, 0)),
        scratch_shapes=[pltpu.VMEM((w // LANES, bsz * TM_PITCH, LANES), F32)],
        compiler_params=_params("parallel"),
        name="odd_out_proj",
    )(xx, mods, gate, hf, hr, w_out)


def _rope_pair_slab(a, b):
    n = A_HEAD_DIM // 4
    part = lambda t, i: t[..., i * n:(i + 1) * n]
    return jnp.concatenate([part(a, 0), part(a, 2), part(b, 0), part(b, 2),
                            part(a, 1), part(a, 3), part(b, 1), part(b, 3)], axis=-1)


def _rope_tables(n_ctx, n_lat):
    rows = n_lat // GRID_W
    row = jnp.repeat(jnp.arange(rows, dtype=F32), GRID_W)
    col = jnp.tile(jnp.arange(GRID_W, dtype=F32), rows)
    n_freq = A_HEAD_DIM // 4
    inv_freq = ROPE_THETA ** (-jnp.arange(n_freq, dtype=F32) / n_freq)
    ar = row[:, None] * inv_freq
    ac = col[:, None] * inv_freq
    cos = jnp.concatenate([jnp.cos(ar), jnp.cos(ar), jnp.cos(ac), jnp.cos(ac)], axis=-1)
    sin = jnp.concatenate([-jnp.sin(ar), jnp.sin(ar), -jnp.sin(ac), jnp.sin(ac)], axis=-1)
    cos = jnp.concatenate([jnp.ones((n_ctx, A_HEAD_DIM), F32), cos], axis=0)
    sin = jnp.concatenate([jnp.zeros((n_ctx, A_HEAD_DIM), F32), sin], axis=0)
    return _rope_pair_slab(cos, cos), _rope_pair_slab(sin, sin)


def kernel(x, c, ctx, c_ctx, ada_w, ada_b, norm_mix, norm_ffn, ffn_w_up, ffn_conv, ffn_w_down, even_w_in, even_w_out, attn_q_gain, attn_k_gain, gla_gate_w_up, gla_gate_b, gla_out_gain, lru_w_in, lru_conv, lru_lambda, lru_w_a, lru_b_a, lru_w_x, lru_b_x, lru_w_out, final_gain):
    bsz, n_lat, d = x.shape
    n_ctx = ctx.shape[1]
    depth = ada_w.shape[0]
    d_ff = ffn_w_down.shape[1]
    assert n_ctx % ROW_TILE == 0 and n_lat % ROW_TILE == 0 and d_ff % FFN_CN == 0
    n_ctx_tiles = n_ctx // ROW_TILE
    s = n_ctx + n_lat

    xx = (ctx, x)
    cond_rows = -(-(bsz + 1) // 8) * 8
    cond = jnp.zeros((cond_rows, d), F32).at[:bsz].set(c).at[bsz].set(c_ctx)
    cos, sin = _rope_tables(n_ctx, n_lat)
    pad_gain = lambda g: _rope_pair_slab(g.reshape(1, A_HEAD_DIM), g.reshape(1, A_HEAD_DIM))

    for l in range(depth):
        last = l == depth - 1
        j = l // 2
        table = adaln_table(cond, ada_w, ada_b.reshape(depth, 1, N_MOD * d), l)
        m_lat = table[:bsz].reshape(bsz, N_MOD, d)
        m_ctx = jnp.broadcast_to(table[bsz].reshape(1, N_MOD, d), (bsz, N_MOD, d))
        mods = jnp.stack([m_ctx, m_lat], axis=1)
        gain_mix = norm_mix[l].reshape(1, d)

        if l % 2 == 0:
            w_in = even_w_in[j]
            o0 = A_HEADS * A_HEAD_DIM
            o1 = o0 + A_KV_HEADS * A_HEAD_DIM
            o2 = o1 + A_KV_HEADS * A_HEAD_DIM
            o3 = o2 + 2 * B_K + 2 * B_V
            head = lambda hh: w_in[:, hh * A_HEAD_DIM:(hh + 1) * A_HEAD_DIM]
            wqkv = jnp.concatenate(
                [_rope_pair_slab(head(m), head(m + A_HEADS // 2)) for m in range(A_HEADS // 2)]
                + [_rope_pair_slab(head(A_HEADS), head(A_HEADS + 1)), w_in[:, o1:o2]], axis=1).astype(BF16)
            wb = w_in[:, o2:o3].astype(BF16)
            wlr = w_in[:, o3:].astype(BF16)
            zeros = jnp.zeros((B_GATE_RANK, B_K), F32)
            wgate = jnp.concatenate([jnp.concatenate([gla_gate_w_up[j, 0], zeros], axis=1),
                                     jnp.concatenate([zeros, gla_gate_w_up[j, 1]], axis=1)], axis=0)
            bgate = gla_gate_b[j].reshape(1, 2 * B_K)
            s_bound = A_HEAD_DIM ** 0.5 * jnp.max(jnp.abs(attn_q_gain[j])) * jnp.max(jnp.abs(attn_k_gain[j]))
            qaug = jnp.zeros((2, LANES), F32)
            qaug = qaug.at[0, ATT_AUG_LANES[0]].set(-s_bound * LOG2E).at[1, ATT_AUG_LANES[1]].set(-s_bound * LOG2E)
            q, k, v, qb, kb, vb, gb, laf, lab = even_in_proj(
                xx, mods, gain_mix, wqkv, wb, wlr, wgate, bgate,
                pad_gain(attn_q_gain[j]), pad_gain(attn_k_gain[j]), qaug, cos, sin, n_ctx_tiles)
            a = attention(s_bound.reshape(1), q, k, v, n_ctx)
            o_f, o_r = gla_bidir(qb, kb, vb, laf, lab, n_ctx_tiles)
            w_out = even_w_out[j]
            xx = even_out_proj(xx, mods, a, o_f, o_r, gb, gla_out_gain[j].reshape(1, B_DV),
                               w_out[:o0].astype(BF16), w_out[o0:].astype(BF16), n_ctx_tiles)
        else:
            w_in = lru_w_in[j]
            gate, rec = odd_in_proj(xx, mods, gain_mix, w_in[:, :LRU_WIDTH].astype(BF16),
                                    w_in[:, LRU_WIDTH:].astype(BF16), n_ctx)
            wax = jnp.concatenate([lru_w_a[j], lru_w_x[j]], axis=-1).astype(BF16)
            bax = 0.5 * jnp.concatenate([lru_b_a[j].reshape(2, LRU_HEADS, 1, LRU_HEAD_DIM),
                                         lru_b_x[j].reshape(2, LRU_HEADS, 1, LRU_HEAD_DIM)], axis=-1)
            hf, hr = lru_scan(rec, lru_conv[j], wax, bax, lru_lambda[j].reshape(2, 1, LRU_WIDTH), n_ctx)
            xx = odd_out_proj(xx, mods, gate, hf, hr, lru_w_out[j].astype(BF16), n_ctx, latents_only=last)

        ffn_weights = (norm_ffn[l].reshape(1, d), ffn_w_up[l].astype(BF16), ffn_conv[l],
                       ffn_w_down[l].astype(BF16), final_gain.reshape(1, d))
        if last and xx.shape[1] == n_lat:
            t = FFN_LATENT_TILE if n_lat % FFN_LATENT_TILE == 0 else ROW_TILE
            xx = conv_ffn(xx, mods, *ffn_weights, 0, latents_only=False, final_norm=True, t=t)
        else:
            xx = conv_ffn(xx, mods, *ffn_weights, n_ctx_tiles, latents_only=last, final_norm=last)
    return xx
```

```python
import functools

import numpy as np
import jax
import jax.numpy as jnp
from jax import lax
from jax.experimental import pallas as pl
from jax.experimental.pallas import tpu as pltpu

F32 = jnp.float32
BF16 = jnp.bfloat16

NORM_EPS = 1e-6
N_MOD = 6
GRID_W = 64
ROPE_THETA = 10000.0

A_HEADS = 8
A_KV_HEADS = 2
A_GROUP = A_HEADS // A_KV_HEADS
A_HEAD_DIM = 64

B_HEADS = 4
B_DK = 64
B_DV = 128
B_K = B_HEADS * B_DK
B_V = B_HEADS * B_DV
B_GATE_RANK = 16
B_GATE_TAU = 16.0
GLA_CHUNK = 64
GLA_SUB = 16

LRU_HEADS = 10
LRU_HEAD_DIM = 128
LRU_WIDTH = LRU_HEADS * LRU_HEAD_DIM
LRU_C = 8.0

LANES = 128
ROW_TILE = 256
ATT_TQ = 256
ATT_TK = 256
FFN_CN = 256
FFN_HALO = 8
FFN_LATENT_TILE = 512
LRU_TT = 32
TM_PITCH = 40
VMEM_LIMIT = 56 * 1024 * 1024

LOG2E = float(np.log2(np.e))
ATT_MAX_STATIC_SHIFT = 40.0
ATT_AUG_LANES = (32, 0)

NT_DIMS = (((1,), (1,)), ((), ()))
TN_DIMS = (((0,), (0,)), ((), ()))


def _params(*sem):
    return pltpu.CompilerParams(dimension_semantics=sem, vmem_limit_bytes=VMEM_LIMIT)


def _resident(shape):
    nd = len(shape)
    return pl.BlockSpec(shape, lambda *_: (0,) * nd, pipeline_mode=pl.Buffered(1))


def _dot(a, b):
    return jnp.dot(a, b, preferred_element_type=F32)


def _split_bf16(a):
    hi = a.astype(BF16)
    lo = (a - hi.astype(F32)).astype(BF16)
    return hi, lo


def _dot_f32(a, b):
    ah, al = _split_bf16(a)
    bh, bl = _split_bf16(b)
    return _dot(ah, bh) + _dot(ah, bl) + _dot(al, bh)


def _sigmoid(x):
    return 0.5 * jnp.tanh(0.5 * x) + 0.5


def _silu(x):
    return x * _sigmoid(x)


def _softplus(x):
    return jnp.maximum(x, 0.0) + jnp.log1p(jnp.exp(-jnp.abs(x)))


def _norm_mod(x, gain, shift, scale):
    ms = jnp.mean(x * x, axis=-1, keepdims=True)
    return (x * lax.rsqrt(ms + NORM_EPS) * gain) * (1.0 + scale) + shift


def _adaln_kernel(c_ref, w_ref, b_ref, o_ref):
    o_ref[...] = _dot_f32(_silu(c_ref[...]), w_ref[...]) + b_ref[...]


def adaln_table(cond, w, b, layer):
    rows, d = cond.shape
    n = w.shape[2]
    tn = 768
    return pl.pallas_call(
        _adaln_kernel,
        out_shape=jax.ShapeDtypeStruct((rows, n), F32),
        grid=(n // tn,),
        in_specs=[pl.BlockSpec((rows, d), lambda j: (0, 0)),
                  pl.BlockSpec((None, d, tn), lambda j: (layer, 0, j)),
                  pl.BlockSpec((None, 1, tn), lambda j: (layer, 0, j))],
        out_specs=pl.BlockSpec((rows, tn), lambda j: (0, j)),
        compiler_params=_params("arbitrary"),
        name="adaln_table",
    )(cond, w, b)


def _mods_spec(n_ctx_tiles, d):
    return pl.BlockSpec((None, None, N_MOD, d),
                        lambda b, i: (b, (i >= n_ctx_tiles).astype(jnp.int32), 0, 0))


def _stream_specs(stream, n_ctx_tiles, t):
    if isinstance(stream, tuple):
        ctx_arr, lat_arr = stream
        lat_off = 0
    else:
        ctx_arr = lat_arr = stream
        lat_off = n_ctx_tiles
    d = ctx_arr.shape[-1]
    ctx_spec = pl.BlockSpec((None, t, d), lambda b, i: (b, jnp.minimum(i, n_ctx_tiles - 1), 0))
    lat_spec = pl.BlockSpec((None, t, d), lambda b, i: (b, jnp.maximum(i - n_ctx_tiles, 0) + lat_off, 0))
    return (ctx_arr, lat_arr), [ctx_spec, lat_spec]


def _stream_tile(xc_ref, xl_ref, n_ctx_tiles):
    return jnp.where(pl.program_id(1) < n_ctx_tiles, xc_ref[...], xl_ref[...])


def _even_in_kernel(xc_ref, xl_ref, mod_ref, gain_ref, wqkv_ref, wb_ref, wlr_ref, wgate_ref, bgate_ref,
                    qg_ref, kg_ref, qaug_ref, cos_ref, sin_ref,
                    q_ref, k_ref, v_ref, qb_ref, kb_ref, vb_ref, gb_ref, laf_ref, lab_ref, *, n_ctx_tiles):
    x = _stream_tile(xc_ref, xl_ref, n_ctx_tiles)
    h = _norm_mod(x, gain_ref[...], mod_ref[0:1, :], mod_ref[1:2, :]).astype(BF16)
    cos = cos_ref[...]
    sin = sin_ref[...]
    lane = lax.broadcasted_iota(jnp.int32, cos.shape, 1)
    first = lane % (LANES // 2) < LANES // 4
    low = lane < A_HEAD_DIM

    def pair_norm_rope(xp, g):
        sq = xp * xp
        both = jnp.sum(sq, axis=-1, keepdims=True)
        ms_first = jnp.sum(jnp.where(first, sq, 0.0), axis=-1, keepdims=True)
        inv = jnp.where(first, lax.rsqrt(ms_first * (1.0 / A_HEAD_DIM) + NORM_EPS),
                        lax.rsqrt((both - ms_first) * (1.0 / A_HEAD_DIM) + NORM_EPS))
        y = xp * inv * g
        return y * cos + pltpu.roll(y, LANES // 2, 1) * sin

    qkv = _dot(h, wqkv_ref[...])
    slab = lambda i: qkv[:, i * LANES:(i + 1) * LANES]
    n_pairs = A_HEADS // 2
    for m in range(n_pairs):
        qp = pair_norm_rope(slab(m), qg_ref[...]) * (A_HEAD_DIM ** -0.5 * LOG2E)
        q_ref[m] = (jnp.where(first, qp, 0.0) + qaug_ref[0:1, :]).astype(BF16)
        q_ref[m + n_pairs] = (jnp.where(first, 0.0, qp) + qaug_ref[1:2, :]).astype(BF16)
    kp = pair_norm_rope(slab(n_pairs), kg_ref[...])
    k_ref[0] = (jnp.where(first, kp, 0.0) + (lane == ATT_AUG_LANES[0]).astype(F32)).astype(BF16)
    k_ref[1] = (jnp.where(first, 0.0, kp) + (lane == ATT_AUG_LANES[1]).astype(F32)).astype(BF16)
    vp = slab(n_pairs + 1)
    v_ref[0] = jnp.where(low, vp, 1.0).astype(BF16)
    v_ref[1] = jnp.where(low, pltpu.roll(vp, A_HEAD_DIM, 1), 1.0).astype(BF16)

    pb = _dot(h, wb_ref[...])
    qb_ref[...] = pb[:, 0:B_K] * B_DK ** -0.5
    kb_ref[...] = pb[:, B_K:2 * B_K]
    vb_ref[...] = pb[:, 2 * B_K:2 * B_K + B_V]
    gb_ref[...] = pb[:, 2 * B_K + B_V:2 * B_K + 2 * B_V].astype(BF16)

    lr = _dot(h, wlr_ref[...])
    z = _dot_f32(lr, wgate_ref[...]) + bgate_ref[...]
    la = -_softplus(-z) * (1.0 / B_GATE_TAU)
    laf_ref[...] = la[:, 0:B_K]
    lab_ref[...] = la[:, B_K:2 * B_K]


def even_in_proj(stream, mods, gain, wqkv, wb, wlr, wgate, bgate, qg, kg, qaug, cos, sin, n_ctx_tiles):
    t = ROW_TILE
    (xc, xl), x_specs = _stream_specs(stream, n_ctx_tiles, t)
    bsz, d = xc.shape[0], xc.shape[-1]
    s = cos.shape[0]
    row = lambda w: pl.BlockSpec((None, t, w), lambda b, i: (b, i, 0))
    heads = lambda n: pl.BlockSpec((None, n, t, LANES), lambda b, i: (b, 0, i, 0))
    tab = pl.BlockSpec((t, LANES), lambda b, i: (i, 0))
    out_shape = [
        jax.ShapeDtypeStruct((bsz, A_HEADS, s, LANES), BF16),
        jax.ShapeDtypeStruct((bsz, A_KV_HEADS, s, LANES), BF16),
        jax.ShapeDtypeStruct((bsz, A_KV_HEADS, s, LANES), BF16),
        jax.ShapeDtypeStruct((bsz, s, B_K), F32),
        jax.ShapeDtypeStruct((bsz, s, B_K), F32),
        jax.ShapeDtypeStruct((bsz, s, B_V), F32),
        jax.ShapeDtypeStruct((bsz, s, B_V), BF16),
        jax.ShapeDtypeStruct((bsz, s, B_K), F32),
        jax.ShapeDtypeStruct((bsz, s, B_K), F32),
    ]
    return pl.pallas_call(
        functools.partial(_even_in_kernel, n_ctx_tiles=n_ctx_tiles),
        out_shape=out_shape,
        grid=(bsz, s // t),
        in_specs=x_specs + [_mods_spec(n_ctx_tiles, d), _resident(gain.shape),
                  _resident(wqkv.shape), _resident(wb.shape), _resident(wlr.shape),
                  _resident(wgate.shape), _resident(bgate.shape), _resident(qg.shape), _resident(kg.shape),
                  _resident(qaug.shape), tab, tab],
        out_specs=[heads(A_HEADS), heads(A_KV_HEADS), heads(A_KV_HEADS),
                   row(B_K), row(B_K), row(B_V), row(B_V), row(B_K), row(B_K)],
        compiler_params=_params("parallel", "parallel"),
        name="even_in_proj",
    )(xc, xl, mods, gain, wqkv, wb, wlr, wgate, bgate, qg, kg, qaug, cos, sin)


def _attn_kernel(shift_ref, q_ref, k_ref, v_ref, o_ref, s_ref, p_ref, m_ref, *, n_ctx_q_tiles, n_ctx_rows):
    i = pl.program_id(2)
    tq = q_ref.shape[1]
    rows = A_GROUP * tq
    n_keys = k_ref.shape[0]

    def finish(acc):
        out = acc / acc[:, A_HEAD_DIM:A_HEAD_DIM + 1]
        low = lax.broadcasted_iota(jnp.int32, (tq, LANES), 1) < A_HEAD_DIM
        for j in range(0, A_GROUP, 2):
            even = out[j * tq:(j + 1) * tq]
            odd = pltpu.roll(out[(j + 1) * tq:(j + 2) * tq], A_HEAD_DIM, 1)
            o_ref[:, (j // 2) * LANES:(j // 2 + 1) * LANES] = jnp.where(low, even, odd).astype(BF16)

    def attend_static_shift(nk):
        qs = q_ref[...].reshape(rows, LANES)
        acc = jnp.zeros((rows, LANES), F32)
        for c0 in range(0, nk, ATT_TK):
            s = lax.dot_general(qs, k_ref[c0:c0 + ATT_TK, :], NT_DIMS, preferred_element_type=F32)
            acc = acc + _dot(jnp.exp2(s).astype(BF16), v_ref[c0:c0 + ATT_TK, :])
        finish(acc)

    def attend_row_max(nk):
        qs = q_ref[...].reshape(rows, LANES)
        m_ref[...] = jnp.full(m_ref.shape, -jnp.inf, F32)
        for c0 in range(0, nk, ATT_TK):
            cols = slice(c0, c0 + ATT_TK)
            s = lax.dot_general(qs, k_ref[cols, :], NT_DIMS, preferred_element_type=F32)
            s_ref[:, cols] = s
            m_ref[...] = jnp.maximum(m_ref[...], jnp.maximum(s[:, 0:LANES], s[:, LANES:2 * LANES]))
        m = jnp.max(m_ref[...], axis=1, keepdims=True)
        for c0 in range(0, nk, ATT_TK):
            cols = slice(c0, c0 + ATT_TK)
            p_ref[:, cols] = jnp.exp2(s_ref[:, cols] - m).astype(BF16)
        finish(_dot(p_ref[:, 0:nk], v_ref[0:nk, :]))

    is_ctx = i < n_ctx_q_tiles
    static_ok = shift_ref[0] <= ATT_MAX_STATIC_SHIFT
    for ctx_tile, nk in ((True, n_ctx_rows), (False, n_keys)):
        tile_match = is_ctx if ctx_tile else jnp.logical_not(is_ctx)

        @pl.when(jnp.logical_and(tile_match, static_ok))
        def _():
            attend_static_shift(nk)

        @pl.when(jnp.logical_and(tile_match, jnp.logical_not(static_ok)))
        def _():
            attend_row_max(nk)


def attention(shift, q, k, v, n_ctx_rows):
    bsz, _, s, _ = q.shape
    tq = ATT_TQ
    assert ATT_TK == 2 * LANES and n_ctx_rows % ATT_TK == 0 and s % ATT_TK == 0
    kernel = functools.partial(_attn_kernel, n_ctx_q_tiles=n_ctx_rows // tq, n_ctx_rows=n_ctx_rows)
    kv_spec = pl.BlockSpec((None, None, s, LANES), lambda b, g, i: (b, g, 0, 0))
    return pl.pallas_call(
        kernel,
        out_shape=jax.ShapeDtypeStruct((bsz, s, A_HEADS * A_HEAD_DIM), BF16),
        grid=(bsz, A_KV_HEADS, s // tq),
        in_specs=[pl.BlockSpec(memory_space=pltpu.SMEM),
                  pl.BlockSpec((None, A_GROUP, tq, LANES), lambda b, g, i: (b, g, i, 0)), kv_spec, kv_spec],
        out_specs=pl.BlockSpec((None, tq, A_GROUP * A_HEAD_DIM), lambda b, g, i: (b, i, g)),
        scratch_shapes=[pltpu.VMEM((A_GROUP * tq, s), F32),
                        pltpu.VMEM((A_GROUP * tq, s), BF16),
                        pltpu.VMEM((A_GROUP * tq, LANES), F32)],
        compiler_params=_params("parallel", "parallel", "arbitrary"),
        name="gqa_attention",
    )(shift, q, k, v)


def _gla_consts(reverse):
    c, sub = GLA_CHUNK, GLA_SUB
    t = lax.broadcasted_iota(jnp.int32, (c, c), 0)
    s = lax.broadcasted_iota(jnp.int32, (c, c), 1)
    if reverse:
        tri = s >= t
        blk = s >= (t // sub + 1) * sub
    else:
        tri = s <= t
        blk = s < (t // sub) * sub
    sums = jnp.concatenate([tri, blk], axis=0).astype(BF16)
    sums = jnp.concatenate([sums, sums], axis=1)
    lane = lax.broadcasted_iota(jnp.int32, (1, B_K), 1)
    head_masks = [(lane // B_DK == hh).astype(F32) for hh in range(B_HEADS)]
    key_row = lax.broadcasted_iota(jnp.int32, (c, 1), 0)
    return sums, tri, head_masks, key_row


def _gla_decays(la, consts, reverse):
    sums = consts[0]
    c = GLA_CHUNK
    cr = _dot(sums, jnp.concatenate(_split_bf16(la), axis=0))
    cum = cr[0:c]
    ref = cr[c:2 * c]
    edge = 0 if reverse else c - 1
    return cum, ref, cum[edge:edge + 1]


def _gla_operands(q, k, v, cum, ref, total, consts, reverse):
    _, _, head_masks, key_row = consts
    c, sub = GLA_CHUNK, GLA_SUB

    def stack_heads(x):
        return jnp.concatenate([x * hm for hm in head_masks], axis=0)

    q_in = stack_heads(q * jnp.exp(cum)).astype(BF16)
    q_loc = stack_heads(q * jnp.exp(cum - ref)).astype(BF16)
    k_out = stack_heads(k * jnp.exp(total - cum)).astype(BF16)
    k_sub = []
    for i in range(c // sub):
        ref_i = ref[i * sub:i * sub + 1]
        valid = (key_row >= i * sub) if reverse else (key_row < (i + 1) * sub)
        k_sub.append((k * jnp.exp(jnp.where(valid, ref_i - cum, -jnp.inf))).astype(BF16))
    v_st = jnp.concatenate([v[:, hh * B_DV:(hh + 1) * B_DV] for hh in range(B_HEADS)], axis=0).astype(BF16)
    return q_in, q_loc, k_out, k_sub, v_st


def _gla_scores(q_loc, k_sub):
    c, sub = GLA_CHUNK, GLA_SUB
    pieces = []
    for i in range(c // sub):
        q_i = jnp.concatenate([q_loc[hh * c + i * sub:hh * c + (i + 1) * sub] for hh in range(B_HEADS)], axis=0)
        pieces.append(lax.dot_general(q_i, k_sub[i], NT_DIMS, preferred_element_type=F32))
    return pieces


def _gla_local(pieces, k_out, v_st, consts):
    causal = consts[1]
    c, sub = GLA_CHUNK, GLA_SUB
    o_heads = []
    for hh in range(B_HEADS):
        s_h = jnp.concatenate([p[hh * sub:(hh + 1) * sub] for p in pieces], axis=0)
        s_h = jnp.where(causal, s_h, 0.0).astype(BF16)
        o_heads.append(_dot(s_h, v_st[hh * c:(hh + 1) * c]))
    o_local = jnp.concatenate(o_heads, axis=0)
    st_inc = lax.dot_general(v_st, k_out, TN_DIMS, preferred_element_type=F32)
    return o_local, st_inc


def _gla_kernel(qf_ref, kf_ref, vf_ref, laf_ref, qr_ref, kr_ref, vr_ref, lar_ref,
                of_ref, or_ref, stf_ref, str_ref):
    @pl.when(pl.program_id(1) == 0)
    def _():
        stf_ref[...] = jnp.zeros_like(stf_ref)
        str_ref[...] = jnp.zeros_like(str_ref)

    c = GLA_CHUNK
    n_chunks = qf_ref.shape[0] // c
    dirs = ((False, qf_ref, kf_ref, vf_ref, laf_ref, of_ref, stf_ref),
            (True, qr_ref, kr_ref, vr_ref, lar_ref, or_ref, str_ref))
    consts = {rev: _gla_consts(rev) for rev in (False, True)}
    units = [(d, n) for n in range(n_chunks) for d in range(2)]
    rows = lambda n: slice(n * c, (n + 1) * c)

    decays = {}
    for d, n in units:
        rev, la_ref = dirs[d][0], dirs[d][4]
        decays[d, n] = _gla_decays(la_ref[rows(n), :], consts[rev], rev)
    operands = {}
    for d, n in units:
        rev, q_ref, k_ref, v_ref = dirs[d][:4]
        operands[d, n] = _gla_operands(q_ref[rows(n), :], k_ref[rows(n), :], v_ref[rows(n), :],
                                       *decays[d, n], consts[rev], rev)
    scores = {}
    for d, n in units:
        scores[d, n] = _gla_scores(operands[d, n][1], operands[d, n][3])
    local = {}
    for d, n in units:
        local[d, n] = _gla_local(scores[d, n], operands[d, n][2], operands[d, n][4], consts[dirs[d][0]])

    for d in range(2):
        rev, o_ref, st_ref = dirs[d][0], dirs[d][5], dirs[d][6]
        st = st_ref[...]
        for n in (range(n_chunks - 1, -1, -1) if rev else range(n_chunks)):
            o_local, st_inc = local[d, n]
            q_in = operands[d, n][0]
            o = o_local + lax.dot_general(q_in, st.astype(BF16), NT_DIMS, preferred_element_type=F32)
            st = st * jnp.exp(decays[d, n][2]) + st_inc
            for hh in range(B_HEADS):
                o_ref[rows(n), hh * B_DV:(hh + 1) * B_DV] = o[hh * c:(hh + 1) * c].astype(BF16)
        st_ref[...] = st


def gla_bidir(qb, kb, vb, laf, lab, n_ctx_tiles):
    bsz, s, _ = qb.shape
    t = ROW_TILE
    nt = s // t

    def rev_tile(j):
        return jnp.where(j < n_ctx_tiles, n_ctx_tiles - 1 - j, nt - 1 - (j - n_ctx_tiles))

    fwd = lambda w: pl.BlockSpec((None, t, w), lambda b, j: (b, j, 0))
    rev = lambda w: pl.BlockSpec((None, t, w), lambda b, j: (b, rev_tile(j), 0))
    return pl.pallas_call(
        _gla_kernel,
        out_shape=[jax.ShapeDtypeStruct((bsz, s, B_V), BF16)] * 2,
        grid=(bsz, nt),
        in_specs=[fwd(B_K), fwd(B_K), fwd(B_V), fwd(B_K), rev(B_K), rev(B_K), rev(B_V), rev(B_K)],
        out_specs=[fwd(B_V), rev(B_V)],
        scratch_shapes=[pltpu.VMEM((B_DV, B_K), F32)] * 2,
        compiler_params=_params("parallel", "arbitrary"),
        name="gla_bidir",
    )(qb, kb, vb, laf, qb, kb, vb, lab)


def _even_out_kernel(xc_ref, xl_ref, mod_ref, a_ref, of_ref, or_ref, g_ref, og_ref, wa_ref, wb_ref, o_ref, *,
                     n_ctx_tiles):
    o = of_ref[...].astype(F32) + or_ref[...].astype(F32)
    g = g_ref[...].astype(F32)
    parts = []
    for hh in range(B_HEADS):
        oh = o[:, hh * B_DV:(hh + 1) * B_DV]
        ms = jnp.mean(oh * oh, axis=-1, keepdims=True)
        y = oh * lax.rsqrt(ms + NORM_EPS) * og_ref[...]
        parts.append((y * _silu(g[:, hh * B_DV:(hh + 1) * B_DV])).astype(BF16))
    gla = jnp.concatenate(parts, axis=1)
    y = _dot(a_ref[...], wa_ref[...]) + _dot(gla, wb_ref[...])
    o_ref[...] = _stream_tile(xc_ref, xl_ref, n_ctx_tiles) + mod_ref[2:3, :] * y


def even_out_proj(stream, mods, a, o_f, o_r, gb, o_gain, wa, wb, n_ctx_tiles):
    t = ROW_TILE
    (xc, xl), x_specs = _stream_specs(stream, n_ctx_tiles, t)
    bsz, s, d = a.shape[0], a.shape[1], xc.shape[-1]
    row = lambda w: pl.BlockSpec((None, t, w), lambda b, i: (b, i, 0))
    return pl.pallas_call(
        functools.partial(_even_out_kernel, n_ctx_tiles=n_ctx_tiles),
        out_shape=jax.ShapeDtypeStruct((bsz, s, d), F32),
        grid=(bsz, s // t),
        in_specs=x_specs + [_mods_spec(n_ctx_tiles, d), row(a.shape[-1]), row(B_V), row(B_V), row(B_V),
                            _resident(o_gain.shape), _resident(wa.shape), _resident(wb.shape)],
        out_specs=row(d),
        compiler_params=_params("parallel", "parallel"),
        name="even_out_proj",
    )(xc, xl, mods, a, o_f, o_r, gb, o_gain, wa, wb)


def _ffn_kernel(x_ref, xp_ref, xn_ref, mod_ref, gain_ref, wu_ref, cw_ref, wd_ref, fg_ref, o_ref,
                h_ref, act_ref, *, tile0, n_ctx_tiles, n_tiles, final_norm):
    ti = pl.program_id(1) + tile0
    t = x_ref.shape[0]
    d_ff = wd_ref.shape[0]
    gain = gain_ref[...]
    shift = mod_ref[3:4, :]
    scale = mod_ref[4:5, :]
    x = x_ref[...]
    has_prev = jnp.logical_and(ti != 0, ti != n_ctx_tiles)
    has_next = jnp.logical_and(ti != n_ctx_tiles - 1, ti != n_tiles - 1)
    hp = jnp.where(has_prev, _norm_mod(xp_ref[...], gain, shift, scale), 0.0)
    hn = jnp.where(has_next, _norm_mod(xn_ref[...], gain, shift, scale), 0.0)
    h_ref[...] = jnp.concatenate([hp, _norm_mod(x, gain, shift, scale), hn], axis=0).astype(BF16)
    n_ext = t + 2 * FFN_HALO
    body = slice(FFN_HALO, FFN_HALO + t)

    def conv(u, cw):
        prev = pltpu.roll(u, 1, 0)[body]
        nxt = pltpu.roll(u, n_ext - 1, 0)[body]
        return cw[0:1] * prev + cw[1:2] * u[body] + cw[2:3] * nxt

    for c0 in range(0, d_ff, FFN_CN):
        gate_cols = slice(c0, c0 + FFN_CN)
        val_cols = slice(d_ff + c0, d_ff + c0 + FFN_CN)
        h = h_ref[...]
        xh = conv(_dot(h, wu_ref[:, gate_cols]), 0.5 * cw_ref[:, gate_cols])
        val = conv(_dot(h, wu_ref[:, val_cols]), cw_ref[:, val_cols])
        act_ref[:, gate_cols] = (xh * (jnp.tanh(xh) + 1.0) * val).astype(BF16)
    y = x + mod_ref[5:6, :] * _dot(act_ref[...], wd_ref[...])
    if final_norm:
        ms = jnp.mean(y * y, axis=-1, keepdims=True)
        y = y * lax.rsqrt(ms + NORM_EPS) * fg_ref[...]
    o_ref[...] = y


def conv_ffn(xx, mods, gain, wu, cw, wd, final_gain, n_ctx_tiles, latents_only, final_norm, t=ROW_TILE):
    bsz, s, d = xx.shape
    assert s % t == 0 and t % FFN_HALO == 0
    nt = s // t
    tile0 = n_ctx_tiles if latents_only else 0
    hb = t // FFN_HALO
    last_hb = s // FFN_HALO - 1
    kernel = functools.partial(_ffn_kernel, tile0=tile0, n_ctx_tiles=n_ctx_tiles, n_tiles=nt, final_norm=final_norm)
    return pl.pallas_call(
        kernel,
        out_shape=jax.ShapeDtypeStruct((bsz, s - tile0 * t, d), F32),
        grid=(bsz, nt - tile0),
        in_specs=[pl.BlockSpec((None, t, d), lambda b, i: (b, i + tile0, 0)),
                  pl.BlockSpec((None, FFN_HALO, d), lambda b, i: (b, jnp.maximum((i + tile0) * hb - 1, 0), 0)),
                  pl.BlockSpec((None, FFN_HALO, d), lambda b, i: (b, jnp.minimum((i + tile0 + 1) * hb, last_hb), 0)),
                  pl.BlockSpec((None, None, N_MOD, d),
                               lambda b, i: (b, (i + tile0 >= n_ctx_tiles).astype(jnp.int32), 0, 0)),
                  _resident(gain.shape), _resident(wu.shape), _resident(cw.shape), _resident(wd.shape),
                  _resident(final_gain.shape)],
        out_specs=pl.BlockSpec((None, t, d), lambda b, i: (b, i, 0)),
        scratch_shapes=[pltpu.VMEM((t + 2 * FFN_HALO, d), BF16),
                        pltpu.VMEM((t, wd.shape[0]), BF16)],
        compiler_params=_params("parallel", "parallel"),
        name="conv_ffn_final" if final_norm else "conv_ffn",
    )(xx, xx, xx, mods, gain, wu, cw, wd, final_gain)


def _odd_mods_spec(bsz, n_ctx_blocks, d):
    return pl.BlockSpec((bsz, None, N_MOD, d), lambda j: (0, (j >= n_ctx_blocks).astype(jnp.int32), 0, 0))


def _odd_in_kernel(x_ref, mod_ref, gain_ref, wg_ref, wr_ref, gate_ref, rec_ref, slab_ref):
    bsz, tt, d = x_ref.shape
    w = wr_ref.shape[1]
    h = _norm_mod(x_ref[...], gain_ref[...], mod_ref[:, 0:1, :], mod_ref[:, 1:2, :])
    h = h.reshape(bsz * tt, d).astype(BF16)
    gate_ref[...] = jax.nn.gelu(_dot(h, wg_ref[...]), approximate=True).astype(BF16).reshape(bsz, tt, w)
    rec = _dot(h, wr_ref[...])
    for b in range(bsz):
        for sl in range(w // LANES):
            slab_ref[sl, b * TM_PITCH:b * TM_PITCH + tt, :] = rec[b * tt:(b + 1) * tt, sl * LANES:(sl + 1) * LANES]
    for t in range(tt):
        for sl in range(w // LANES):
            rec_ref[t, :, sl * LANES:(sl + 1) * LANES] = slab_ref[sl, pl.ds(t, bsz, stride=TM_PITCH), :]


def odd_in_proj(xx, mods, gain, wg, wr, n_ctx_rows):
    bsz, s, d = xx.shape
    tt = LRU_TT
    w = wg.shape[1]
    assert tt <= TM_PITCH and TM_PITCH % 8 == 0
    return pl.pallas_call(
        _odd_in_kernel,
        out_shape=[jax.ShapeDtypeStruct((bsz, s, w), BF16), jax.ShapeDtypeStruct((s, bsz, w), F32)],
        grid=(s // tt,),
        in_specs=[pl.BlockSpec((bsz, tt, d), lambda j: (0, j, 0)), _odd_mods_spec(bsz, n_ctx_rows // tt, d),
                  _resident(gain.shape), _resident(wg.shape), _resident(wr.shape)],
        out_specs=[pl.BlockSpec((bsz, tt, w), lambda j: (0, j, 0)),
                   pl.BlockSpec((tt, bsz, w), lambda j: (j, 0, 0))],
        scratch_shapes=[pltpu.VMEM((w // LANES, bsz * TM_PITCH, LANES), F32)],
        compiler_params=_params("parallel"),
        name="odd_in_proj",
    )(xx, mods, gain, wg, wr)


def _lru_kernel(uf_ref, ufp_ref, ufn_ref, ur_ref, urp_ref, urn_ref, cw_ref, wax_ref, bax_ref, lam_ref,
                hf_ref, hr_ref, ue_ref, a_ref, x_ref, hst_ref, *, n_ctx_blocks, n_blocks):
    j = pl.program_id(0)
    tt, bsz, w = uf_ref.shape

    @pl.when(j == 0)
    def _():
        hst_ref[...] = jnp.zeros_like(hst_ref)

    rev_blk = jnp.where(j < n_ctx_blocks, n_ctx_blocks - 1 - j, n_blocks - 1 - (j - n_ctx_blocks))
    for d, (blk, u_ref, up_ref, un_ref, o_ref) in enumerate(
            ((j, uf_ref, ufp_ref, ufn_ref, hf_ref), (rev_blk, ur_ref, urp_ref, urn_ref, hr_ref))):
        has_prev = jnp.logical_and(blk != 0, blk != n_ctx_blocks)
        has_next = jnp.logical_and(blk != n_ctx_blocks - 1, blk != n_blocks - 1)
        ue_ref[0:1] = jnp.where(has_prev, up_ref[...], 0.0)
        ue_ref[1:tt + 1] = u_ref[...]
        ue_ref[tt + 1:tt + 3] = jnp.where(has_next, un_ref[...], 0.0)
        cw = 0.5 * cw_ref[...]
        uc = cw[0:1] * ue_ref[0:tt] + cw[1:2] * ue_ref[1:tt + 1] + cw[2:3] * ue_ref[2:tt + 2] + cw[3:4] * ue_ref[3:tt + 3]
        uc = uc.reshape(tt * bsz, w)
        half_c = (-0.5 * LRU_C) * _softplus(-lam_ref[d])
        for hh in range(LRU_HEADS):
            cols = slice(hh * LRU_HEAD_DIM, (hh + 1) * LRU_HEAD_DIM)
            uh = uc[:, cols]
            th = jnp.tanh(_dot(uh.astype(BF16), wax_ref[d, hh]) + bax_ref[d, hh])
            log_a = th[:, 0:LRU_HEAD_DIM] * half_c[:, cols] + half_c[:, cols]
            a = jnp.exp(log_a)
            m2 = jnp.tanh(log_a) * (-1.0 - a * a)
            mult = m2 * lax.rsqrt(jnp.maximum(m2, jnp.finfo(F32).tiny))
            a_ref[:, :, cols] = a.reshape(tt, bsz, LRU_HEAD_DIM)
            x_ref[:, :, cols] = (mult * (th[:, LRU_HEAD_DIM:2 * LRU_HEAD_DIM] + 1.0) * uh).reshape(tt, bsz, LRU_HEAD_DIM)
        h = hst_ref[d]
        for step in range(tt):
            tcur = tt - 1 - step if d == 1 else step
            h = a_ref[tcur] * h + x_ref[tcur]
            o_ref[tcur] = h
        hst_ref[d] = h


def lru_scan(u, cw, wax, bax, lam, n_ctx_rows):
    s, bsz, w = u.shape
    tt = LRU_TT
    nb = s // tt
    ncb = n_ctx_rows // tt

    def rev_blk(j):
        return jnp.where(j < ncb, ncb - 1 - j, nb - 1 - (j - ncb))

    def specs(blk):
        return [pl.BlockSpec((tt, bsz, w), lambda j: (blk(j), 0, 0)),
                pl.BlockSpec((1, bsz, w), lambda j: (jnp.maximum(blk(j) * tt - 1, 0), 0, 0)),
                pl.BlockSpec((2, bsz, w), lambda j: (jnp.minimum((blk(j) + 1) * (tt // 2), s // 2 - 1), 0, 0))]

    fwd_blk = lambda j: j
    kernel = functools.partial(_lru_kernel, n_ctx_blocks=ncb, n_blocks=nb)
    return pl.pallas_call(
        kernel,
        out_shape=[jax.ShapeDtypeStruct((s, bsz, w), F32)] * 2,
        grid=(nb,),
        in_specs=specs(fwd_blk) + specs(rev_blk) + [_resident(cw.shape), _resident(wax.shape),
                                                     _resident(bax.shape), _resident(lam.shape)],
        out_specs=[pl.BlockSpec((tt, bsz, w), lambda j: (j, 0, 0)),
                   pl.BlockSpec((tt, bsz, w), lambda j: (rev_blk(j), 0, 0))],
        scratch_shapes=[pltpu.VMEM((tt + 3, bsz, w), F32), pltpu.VMEM((tt, bsz, w), F32),
                        pltpu.VMEM((tt, bsz, w), F32), pltpu.VMEM((2, bsz, w), F32)],
        compiler_params=_params("arbitrary"),
        name="lru_scan",
    )(u, u, u, u, u, u, cw, wax, bax, lam)


def _odd_out_kernel(x_ref, mod_ref, gate_ref, hf_ref, hr_ref, w_ref, o_ref, slab_ref):
    bsz, tt, d = x_ref.shape
    w = w_ref.shape[0]
    for t in range(tt):
        hs = hf_ref[t] + hr_ref[t]
        for sl in range(w // LANES):
            slab_ref[sl, pl.ds(t, bsz, stride=TM_PITCH), :] = hs[:, sl * LANES:(sl + 1) * LANES]
    rec = jnp.concatenate(
        [jnp.concatenate([slab_ref[sl, b * TM_PITCH:b * TM_PITCH + tt, :] for sl in range(w // LANES)], axis=1)
         for b in range(bsz)], axis=0)
    mixed = (gate_ref[...].reshape(bsz * tt, w) * rec).astype(BF16)
    y = _dot(mixed, w_ref[...]).reshape(bsz, tt, d)
    o_ref[...] = x_ref[...] + mod_ref[:, 2:3, :] * y


def odd_out_proj(xx, mods, gate, hf, hr, w_out, n_ctx_rows, latents_only):
    bsz, s, d = xx.shape
    tt = LRU_TT
    w = gate.shape[-1]
    j0 = n_ctx_rows // tt if latents_only else 0
    bmaj = lambda n: pl.BlockSpec((bsz, tt, n), lambda j: (0, j + j0, 0))
    tmaj = pl.BlockSpec((tt, bsz, w), lambda j: (j + j0, 0, 0))
    mods_spec = pl.BlockSpec((bsz, None, N_MOD, d),
                             lambda j: (0, (j + j0 >= n_ctx_rows // tt).astype(jnp.int32), 0, 0))
    return pl.pallas_call(
        _odd_out_kernel,
        out_shape=jax.ShapeDtypeStruct((bsz, s - j0 * tt, d), F32),
        grid=(s // tt - j0,),
        in_specs=[bmaj(d), mods_spec, bmaj(w), tmaj, tmaj, _resident(w_out.shape)],
        out_specs=pl.BlockSpec((bsz, tt, d), lambda j: (0, j, 0)),
        scratch_shapes=[pltpu.VMEM((w // LANES, bsz * TM_PITCH, LANES), F32)],
        compiler_params=_params("parallel"),
        name="odd_out_proj",
    )(xx, mods, gate, hf, hr, w_out)


def _rope_pair_slab(a, b):
    n = A_HEAD_DIM // 4
    part = lambda t, i: t[..., i * n:(i + 1) * n]
    return jnp.concatenate([part(a, 0), part(a, 2), part(b, 0), part(b, 2),
                            part(a, 1), part(a, 3), part(b, 1), part(b, 3)], axis=-1)


def _rope_tables(n_ctx, n_lat):
    rows = n_lat // GRID_W
    row = jnp.repeat(jnp.arange(rows, dtype=F32), GRID_W)
    col = jnp.tile(jnp.arange(GRID_W, dtype=F32), rows)
    n_freq = A_HEAD_DIM // 4
    inv_freq = ROPE_THETA ** (-jnp.arange(n_freq, dtype=F32) / n_freq)
    ar = row[:, None] * inv_freq
    ac = col[:, None] * inv_freq
    cos = jnp.concatenate([jnp.cos(ar), jnp.cos(ar), jnp.cos(ac), jnp.cos(ac)], axis=-1)
    sin = jnp.concatenate([-jnp.sin(ar), jnp.sin(ar), -jnp.sin(ac), jnp.sin(ac)], axis=-1)
    cos = jnp.concatenate([jnp.ones((n_ctx, A_HEAD_DIM), F32), cos], axis=0)
    sin = jnp.concatenate([jnp.zeros((n_ctx, A_HEAD_DIM), F32), sin], axis=0)
    return _rope_pair_slab(cos, cos), _rope_pair_slab(sin, sin)


def kernel(x, c, ctx, c_ctx, ada_w, ada_b, norm_mix, norm_ffn, ffn_w_up, ffn_conv, ffn_w_down, even_w_in, even_w_out, attn_q_gain, attn_k_gain, gla_gate_w_up, gla_gate_b, gla_out_gain, lru_w_in, lru_conv, lru_lambda, lru_w_a, lru_b_a, lru_w_x, lru_b_x, lru_w_out, final_gain):
    bsz, n_lat, d = x.shape
    n_ctx = ctx.shape[1]
    depth = ada_w.shape[0]
    d_ff = ffn_w_down.shape[1]
    assert n_ctx % ROW_TILE == 0 and n_lat % ROW_TILE == 0 and d_ff % FFN_CN == 0
    n_ctx_tiles = n_ctx // ROW_TILE
    s = n_ctx + n_lat

    xx = (ctx, x)
    cond_rows = -(-(bsz + 1) // 8) * 8
    cond = jnp.zeros((cond_rows, d), F32).at[:bsz].set(c).at[bsz].set(c_ctx)
    cos, sin = _rope_tables(n_ctx, n_lat)
    pad_gain = lambda g: _rope_pair_slab(g.reshape(1, A_HEAD_DIM), g.reshape(1, A_HEAD_DIM))

    for l in range(depth):
        last = l == depth - 1
        j = l // 2
        table = adaln_table(cond, ada_w, ada_b.reshape(depth, 1, N_MOD * d), l)
        m_lat = table[:bsz].reshape(bsz, N_MOD, d)
        m_ctx = jnp.broadcast_to(table[bsz].reshape(1, N_MOD, d), (bsz, N_MOD, d))
        mods = jnp.stack([m_ctx, m_lat], axis=1)
        gain_mix = norm_mix[l].reshape(1, d)

        if l % 2 == 0:
            w_in = even_w_in[j]
            o0 = A_HEADS * A_HEAD_DIM
            o1 = o0 + A_KV_HEADS * A_HEAD_DIM
            o2 = o1 + A_KV_HEADS * A_HEAD_DIM
            o3 = o2 + 2 * B_K + 2 * B_V
            head = lambda hh: w_in[:, hh * A_HEAD_DIM:(hh + 1) * A_HEAD_DIM]
            wqkv = jnp.concatenate(
                [_rope_pair_slab(head(m), head(m + A_HEADS // 2)) for m in range(A_HEADS // 2)]
                + [_rope_pair_slab(head(A_HEADS), head(A_HEADS + 1)), w_in[:, o1:o2]], axis=1).astype(BF16)
            wb = w_in[:, o2:o3].astype(BF16)
            wlr = w_in[:, o3:].astype(BF16)
            zeros = jnp.zeros((B_GATE_RANK, B_K), F32)
            wgate = jnp.concatenate([jnp.concatenate([gla_gate_w_up[j, 0], zeros], axis=1),
                                     jnp.concatenate([zeros, gla_gate_w_up[j, 1]], axis=1)], axis=0)
            bgate = gla_gate_b[j].reshape(1, 2 * B_K)
            s_bound = A_HEAD_DIM ** 0.5 * jnp.max(jnp.abs(attn_q_gain[j])) * jnp.max(jnp.abs(attn_k_gain[j]))
            qaug = jnp.zeros((2, LANES), F32)
            qaug = qaug.at[0, ATT_AUG_LANES[0]].set(-s_bound * LOG2E).at[1, ATT_AUG_LANES[1]].set(-s_bound * LOG2E)
            q, k, v, qb, kb, vb, gb, laf, lab = even_in_proj(
                xx, mods, gain_mix, wqkv, wb, wlr, wgate, bgate,
                pad_gain(attn_q_gain[j]), pad_gain(attn_k_gain[j]), qaug, cos, sin, n_ctx_tiles)
            a = attention(s_bound.reshape(1), q, k, v, n_ctx)
            o_f, o_r = gla_bidir(qb, kb, vb, laf, lab, n_ctx_tiles)
            w_out = even_w_out[j]
            xx = even_out_proj(xx, mods, a, o_f, o_r, gb, gla_out_gain[j].reshape(1, B_DV),
                               w_out[:o0].astype(BF16), w_out[o0:].astype(BF16), n_ctx_tiles)
        else:
            w_in = lru_w_in[j]
            gate, rec = odd_in_proj(xx, mods, gain_mix, w_in[:, :LRU_WIDTH].astype(BF16),
                                    w_in[:, LRU_WIDTH:].astype(BF16), n_ctx)
            wax = jnp.concatenate([lru_w_a[j], lru_w_x[j]], axis=-1).astype(BF16)
            bax = 0.5 * jnp.concatenate([lru_b_a[j].reshape(2, LRU_HEADS, 1, LRU_HEAD_DIM),
                                         lru_b_x[j].reshape(2, LRU_HEADS, 1, LRU_HEAD_DIM)], axis=-1)
            hf, hr = lru_scan(rec, lru_conv[j], wax, bax, lru_lambda[j].reshape(2, 1, LRU_WIDTH), n_ctx)
            xx = odd_out_proj(xx, mods, gate, hf, hr, lru_w_out[j].astype(BF16), n_ctx, latents_only=last)

        ffn_weights = (norm_ffn[l].reshape(1, d), ffn_w_up[l].astype(BF16), ffn_conv[l],
                       ffn_w_down[l].astype(BF16), final_gain.reshape(1, d))
        if last and xx.shape[1] == n_lat:
            t = FFN_LATENT_TILE if n_lat % FFN_LATENT_TILE == 0 else ROW_TILE
            xx = conv_ffn(xx, mods, *ffn_weights, 0, latents_only=False, final_norm=True, t=t)
        else:
            xx = conv_ffn(xx, mods, *ffn_weights, n_ctx_tiles, latents_only=last, final_norm=last)
    return xx
```

```python
import functools

import numpy as np
import jax
import jax.numpy as jnp
from jax import lax
from jax.experimental import pallas as pl
from jax.experimental.pallas import tpu as pltpu

F32 = jnp.float32
BF16 = jnp.bfloat16

NORM_EPS = 1e-6
N_MOD = 6
GRID_W = 64
ROPE_THETA = 10000.0

A_HEADS = 8
A_KV_HEADS = 2
A_GROUP = A_HEADS // A_KV_HEADS
A_HEAD_DIM = 64

B_HEADS = 4
B_DK = 64
B_DV = 128
B_K = B_HEADS * B_DK
B_V = B_HEADS * B_DV
B_GATE_RANK = 16
B_GATE_TAU = 16.0
GLA_CHUNK = 64
GLA_SUB = 16

LRU_HEADS = 10
LRU_HEAD_DIM = 128
LRU_WIDTH = LRU_HEADS * LRU_HEAD_DIM
LRU_C = 8.0

LANES = 128
ROW_TILE = 256
EVEN_OUT_BATCH = 2
ATT_TQ = 256
ATT_TK = 256
FFN_CN = 256
FFN_HALO = 8
FFN_LATENT_TILE = 512
FFN_ROWS = 512
LRU_TT = 32
TM_PITCH = 40
VMEM_LIMIT = 56 * 1024 * 1024

LOG2E = float(np.log2(np.e))
ATT_MAX_STATIC_SHIFT = 40.0
ATT_AUG_LANES = (32, 0)

NT_DIMS = (((1,), (1,)), ((), ()))
TN_DIMS = (((0,), (0,)), ((), ()))


def _params(*sem):
    return pltpu.CompilerParams(dimension_semantics=sem, vmem_limit_bytes=VMEM_LIMIT)


def _resident(shape):
    nd = len(shape)
    return pl.BlockSpec(shape, lambda *_: (0,) * nd, pipeline_mode=pl.Buffered(1))


def _resident_layer(shape, layer):
    nd = len(shape)
    return pl.BlockSpec((None,) + tuple(shape[1:]), lambda *_: (layer,) + (0,) * (nd - 1),
                        pipeline_mode=pl.Buffered(1))


def _dot(a, b):
    return jnp.dot(a, b, preferred_element_type=F32)


def _split_bf16(a):
    hi = a.astype(BF16)
    lo = (a - hi.astype(F32)).astype(BF16)
    return hi, lo


def _dot_f32(a, b):
    ah, al = _split_bf16(a)
    bh, bl = _split_bf16(b)
    return _dot(ah, bh) + _dot(ah, bl) + _dot(al, bh)


def _sigmoid(x):
    return 0.5 * jnp.tanh(0.5 * x) + 0.5


def _silu(x):
    return x * _sigmoid(x)


def _softplus(x):
    return jnp.maximum(x, 0.0) + jnp.log1p(jnp.exp(-jnp.abs(x)))


def _norm_mod(x, gain, shift, scale):
    ms = jnp.mean(x * x, axis=-1, keepdims=True)
    return (x * lax.rsqrt(ms + NORM_EPS) * gain) * (1.0 + scale) + shift


def _adaln_kernel(c_ref, w_ref, b_ref, o_ref):
    o_ref[...] = _dot_f32(_silu(c_ref[...]), w_ref[...]) + b_ref[...]


def adaln_table(cond, w, b, layer):
    rows, d = cond.shape
    n = w.shape[2]
    tn = 768
    return pl.pallas_call(
        _adaln_kernel,
        out_shape=jax.ShapeDtypeStruct((rows, n), F32),
        grid=(n // tn,),
        in_specs=[pl.BlockSpec((rows, d), lambda j: (0, 0)),
                  pl.BlockSpec((None, d, tn), lambda j: (layer, 0, j)),
                  pl.BlockSpec((None, 1, tn), lambda j: (layer, 0, j))],
        out_specs=pl.BlockSpec((rows, tn), lambda j: (0, j)),
        compiler_params=_params("arbitrary"),
        name="adaln_table",
    )(cond, w, b)


def _mods_spec(n_ctx_tiles, d):
    return pl.BlockSpec((None, None, N_MOD, d),
                        lambda b, i: (b, (i >= n_ctx_tiles).astype(jnp.int32), 0, 0))


def _stream_specs(stream, n_ctx_tiles, t, nb=None):
    if isinstance(stream, tuple):
        ctx_arr, lat_arr = stream
        lat_off = 0
    else:
        ctx_arr = lat_arr = stream
        lat_off = n_ctx_tiles
    d = ctx_arr.shape[-1]
    ctx_spec = pl.BlockSpec((nb, t, d), lambda b, i: (b, jnp.minimum(i, n_ctx_tiles - 1), 0))
    lat_spec = pl.BlockSpec((nb, t, d), lambda b, i: (b, jnp.maximum(i - n_ctx_tiles, 0) + lat_off, 0))
    return (ctx_arr, lat_arr), [ctx_spec, lat_spec]


def _stream_tile(xc_ref, xl_ref, n_ctx_tiles):
    return jnp.where(pl.program_id(1) < n_ctx_tiles, xc_ref[...], xl_ref[...])


def _even_in_kernel(xc_ref, xl_ref, mod_ref, gain_ref, wqkv_ref, wb_ref, wlr_ref, wgate_ref, bgate_ref,
                    qg_ref, kg_ref, qaug_ref, cos_ref, sin_ref,
                    q_ref, k_ref, v_ref, qb_ref, kb_ref, vb_ref, gb_ref, laf_ref, lab_ref, *, n_ctx_tiles):
    x = _stream_tile(xc_ref, xl_ref, n_ctx_tiles)
    h = _norm_mod(x, gain_ref[...], mod_ref[0:1, :], mod_ref[1:2, :]).astype(BF16)
    cos = cos_ref[...]
    sin = sin_ref[...]
    lane = lax.broadcasted_iota(jnp.int32, cos.shape, 1)
    first = lane % (LANES // 2) < LANES // 4
    low = lane < A_HEAD_DIM

    def pair_norm_rope(xp, g):
        sq = xp * xp
        both = jnp.sum(sq, axis=-1, keepdims=True)
        ms_first = jnp.sum(jnp.where(first, sq, 0.0), axis=-1, keepdims=True)
        inv = jnp.where(first, lax.rsqrt(ms_first * (1.0 / A_HEAD_DIM) + NORM_EPS),
                        lax.rsqrt((both - ms_first) * (1.0 / A_HEAD_DIM) + NORM_EPS))
        y = xp * inv * g
        return y * cos + pltpu.roll(y, LANES // 2, 1) * sin

    qkv = _dot(h, wqkv_ref[...])
    slab = lambda i: qkv[:, i * LANES:(i + 1) * LANES]
    n_pairs = A_HEADS // 2
    for m in range(n_pairs):
        qp = pair_norm_rope(slab(m), qg_ref[...]) * (A_HEAD_DIM ** -0.5 * LOG2E)
        q_ref[m] = (jnp.where(first, qp, 0.0) + qaug_ref[0:1, :]).astype(BF16)
        q_ref[m + n_pairs] = (jnp.where(first, 0.0, qp) + qaug_ref[1:2, :]).astype(BF16)
    kp = pair_norm_rope(slab(n_pairs), kg_ref[...])
    k_ref[0] = (jnp.where(first, kp, 0.0) + (lane == ATT_AUG_LANES[0]).astype(F32)).astype(BF16)
    k_ref[1] = (jnp.where(first, 0.0, kp) + (lane == ATT_AUG_LANES[1]).astype(F32)).astype(BF16)
    vp = slab(n_pairs + 1)
    v_ref[0] = jnp.where(low, vp, 1.0).astype(BF16)
    v_ref[1] = jnp.where(low, pltpu.roll(vp, A_HEAD_DIM, 1), 1.0).astype(BF16)

    pb = _dot(h, wb_ref[...])
    qb_ref[...] = pb[:, 0:B_K] * B_DK ** -0.5
    kb_ref[...] = pb[:, B_K:2 * B_K]
    vb_ref[...] = pb[:, 2 * B_K:2 * B_K + B_V]
    gb_ref[...] = pb[:, 2 * B_K + B_V:2 * B_K + 2 * B_V].astype(BF16)

    lr = _dot(h, wlr_ref[...])
    z = _dot_f32(lr, wgate_ref[...]) + bgate_ref[...]
    la = -_softplus(-z) * (1.0 / B_GATE_TAU)
    laf_ref[...] = la[:, 0:B_K]
    lab_ref[...] = la[:, B_K:2 * B_K]


def even_in_proj(stream, mods, gain, wqkv, wb, wlr, wgate, bgate, qg, kg, qaug, cos, sin, n_ctx_tiles):
    t = ROW_TILE
    (xc, xl), x_specs = _stream_specs(stream, n_ctx_tiles, t)
    bsz, d = xc.shape[0], xc.shape[-1]
    s = cos.shape[0]
    row = lambda w: pl.BlockSpec((None, t, w), lambda b, i: (b, i, 0))
    heads = lambda n: pl.BlockSpec((None, n, t, LANES), lambda b, i: (b, 0, i, 0))
    tab = pl.BlockSpec((t, LANES), lambda b, i: (i, 0))
    out_shape = [
        jax.ShapeDtypeStruct((bsz, A_HEADS, s, LANES), BF16),
        jax.ShapeDtypeStruct((bsz, A_KV_HEADS, s, LANES), BF16),
        jax.ShapeDtypeStruct((bsz, A_KV_HEADS, s, LANES), BF16),
        jax.ShapeDtypeStruct((bsz, s, B_K), F32),
        jax.ShapeDtypeStruct((bsz, s, B_K), F32),
        jax.ShapeDtypeStruct((bsz, s, B_V), F32),
        jax.ShapeDtypeStruct((bsz, s, B_V), BF16),
        jax.ShapeDtypeStruct((bsz, s, B_K), F32),
        jax.ShapeDtypeStruct((bsz, s, B_K), F32),
    ]
    return pl.pallas_call(
        functools.partial(_even_in_kernel, n_ctx_tiles=n_ctx_tiles),
        out_shape=out_shape,
        grid=(bsz, s // t),
        in_specs=x_specs + [_mods_spec(n_ctx_tiles, d), _resident(gain.shape),
                  _resident(wqkv.shape), _resident(wb.shape), _resident(wlr.shape),
                  _resident(wgate.shape), _resident(bgate.shape), _resident(qg.shape), _resident(kg.shape),
                  _resident(qaug.shape), tab, tab],
        out_specs=[heads(A_HEADS), heads(A_KV_HEADS), heads(A_KV_HEADS),
                   row(B_K), row(B_K), row(B_V), row(B_V), row(B_K), row(B_K)],
        compiler_params=_params("parallel", "parallel"),
        name="even_in_proj",
    )(xc, xl, mods, gain, wqkv, wb, wlr, wgate, bgate, qg, kg, qaug, cos, sin)


def _attn_kernel(shift_ref, q_ref, k_ref, v_ref, o_ref, s_ref, p_ref, m_ref, *, n_ctx_q_tiles, n_ctx_rows):
    i = pl.program_id(2)
    tq = q_ref.shape[1]
    rows = A_GROUP * tq
    n_keys = k_ref.shape[0]

    def finish(acc):
        out = acc / acc[:, A_HEAD_DIM:A_HEAD_DIM + 1]
        low = lax.broadcasted_iota(jnp.int32, (tq, LANES), 1) < A_HEAD_DIM
        for j in range(0, A_GROUP, 2):
            even = out[j * tq:(j + 1) * tq]
            odd = pltpu.roll(out[(j + 1) * tq:(j + 2) * tq], A_HEAD_DIM, 1)
            o_ref[:, (j // 2) * LANES:(j // 2 + 1) * LANES] = jnp.where(low, even, odd).astype(BF16)

    def attend_static_shift(nk):
        qs = q_ref[...].reshape(rows, LANES)
        acc = jnp.zeros((rows, LANES), F32)
        for c0 in range(0, nk, ATT_TK):
            s = lax.dot_general(qs, k_ref[c0:c0 + ATT_TK, :], NT_DIMS, preferred_element_type=F32)
            acc = acc + _dot(jnp.exp2(s).astype(BF16), v_ref[c0:c0 + ATT_TK, :])
        finish(acc)

    def attend_row_max(nk):
        qs = q_ref[...].reshape(rows, LANES)
        m_ref[...] = jnp.full(m_ref.shape, -jnp.inf, F32)
        for c0 in range(0, nk, ATT_TK):
            cols = slice(c0, c0 + ATT_TK)
            s = lax.dot_general(qs, k_ref[cols, :], NT_DIMS, preferred_element_type=F32)
            s_ref[:, cols] = s
            m_ref[...] = jnp.maximum(m_ref[...], jnp.maximum(s[:, 0:LANES], s[:, LANES:2 * LANES]))
        m = jnp.max(m_ref[...], axis=1, keepdims=True)
        for c0 in range(0, nk, ATT_TK):
            cols = slice(c0, c0 + ATT_TK)
            p_ref[:, cols] = jnp.exp2(s_ref[:, cols] - m).astype(BF16)
        finish(_dot(p_ref[:, 0:nk], v_ref[0:nk, :]))

    is_ctx = i < n_ctx_q_tiles
    static_ok = shift_ref[0] <= ATT_MAX_STATIC_SHIFT
    for ctx_tile, nk in ((True, n_ctx_rows), (False, n_keys)):
        tile_match = is_ctx if ctx_tile else jnp.logical_not(is_ctx)

        @pl.when(jnp.logical_and(tile_match, static_ok))
        def _():
            attend_static_shift(nk)

        @pl.when(jnp.logical_and(tile_match, jnp.logical_not(static_ok)))
        def _():
            attend_row_max(nk)


def attention(shift, q, k, v, n_ctx_rows):
    bsz, _, s, _ = q.shape
    tq = ATT_TQ
    assert ATT_TK == 2 * LANES and n_ctx_rows % ATT_TK == 0 and s % ATT_TK == 0
    kernel = functools.partial(_attn_kernel, n_ctx_q_tiles=n_ctx_rows // tq, n_ctx_rows=n_ctx_rows)
    kv_spec = pl.BlockSpec((None, None, s, LANES), lambda b, g, i: (b, g, 0, 0))
    return pl.pallas_call(
        kernel,
        out_shape=jax.ShapeDtypeStruct((bsz, s, A_HEADS * A_HEAD_DIM), BF16),
        grid=(bsz, A_KV_HEADS, s // tq),
        in_specs=[pl.BlockSpec(memory_space=pltpu.SMEM),
                  pl.BlockSpec((None, A_GROUP, tq, LANES), lambda b, g, i: (b, g, i, 0)), kv_spec, kv_spec],
        out_specs=pl.BlockSpec((None, tq, A_GROUP * A_HEAD_DIM), lambda b, g, i: (b, i, g)),
        scratch_shapes=[pltpu.VMEM((A_GROUP * tq, s), F32),
                        pltpu.VMEM((A_GROUP * tq, s), BF16),
                        pltpu.VMEM((A_GROUP * tq, LANES), F32)],
        compiler_params=_params("parallel", "parallel", "arbitrary"),
        name="gqa_attention",
    )(shift, q, k, v)


def _gla_consts(reverse):
    c, sub = GLA_CHUNK, GLA_SUB
    t = lax.broadcasted_iota(jnp.int32, (c, c), 0)
    s = lax.broadcasted_iota(jnp.int32, (c, c), 1)
    if reverse:
        tri = s >= t
        blk = s >= (t // sub + 1) * sub
    else:
        tri = s <= t
        blk = s < (t // sub) * sub
    sums = jnp.concatenate([tri, blk], axis=0).astype(BF16)
    sums = jnp.concatenate([sums, sums], axis=1)
    lane = lax.broadcasted_iota(jnp.int32, (1, B_K), 1)
    head_masks = [(lane // B_DK == hh).astype(F32) for hh in range(B_HEADS)]
    key_row = lax.broadcasted_iota(jnp.int32, (c, 1), 0)
    return sums, tri, head_masks, key_row


def _gla_decays(la, consts, reverse):
    sums = consts[0]
    c = GLA_CHUNK
    cr = _dot(sums, jnp.concatenate(_split_bf16(la), axis=0))
    cum = cr[0:c]
    ref = cr[c:2 * c]
    edge = 0 if reverse else c - 1
    return cum, ref, cum[edge:edge + 1]


def _gla_operands(q, k, v, cum, ref, total, consts, reverse):
    _, _, head_masks, key_row = consts
    c, sub = GLA_CHUNK, GLA_SUB

    def stack_heads(x):
        return jnp.concatenate([x * hm for hm in head_masks], axis=0)

    q_in = stack_heads(q * jnp.exp(cum)).astype(BF16)
    q_loc = stack_heads(q * jnp.exp(cum - ref)).astype(BF16)
    k_out = stack_heads(k * jnp.exp(total - cum)).astype(BF16)
    k_sub = []
    for i in range(c // sub):
        ref_i = ref[i * sub:i * sub + 1]
        valid = (key_row >= i * sub) if reverse else (key_row < (i + 1) * sub)
        k_sub.append((k * jnp.exp(jnp.where(valid, ref_i - cum, -jnp.inf))).astype(BF16))
    v_st = jnp.concatenate([v[:, hh * B_DV:(hh + 1) * B_DV] for hh in range(B_HEADS)], axis=0).astype(BF16)
    return q_in, q_loc, k_out, k_sub, v_st


def _gla_scores(q_loc, k_sub):
    c, sub = GLA_CHUNK, GLA_SUB
    pieces = []
    for i in range(c // sub):
        q_i = jnp.concatenate([q_loc[hh * c + i * sub:hh * c + (i + 1) * sub] for hh in range(B_HEADS)], axis=0)
        pieces.append(lax.dot_general(q_i, k_sub[i], NT_DIMS, preferred_element_type=F32))
    return pieces


def _gla_local(pieces, k_out, v_st, consts):
    causal = consts[1]
    c, sub = GLA_CHUNK, GLA_SUB
    o_heads = []
    for hh in range(B_HEADS):
        s_h = jnp.concatenate([p[hh * sub:(hh + 1) * sub] for p in pieces], axis=0)
        s_h = jnp.where(causal, s_h, 0.0).astype(BF16)
        o_heads.append(_dot(s_h, v_st[hh * c:(hh + 1) * c]))
    o_local = jnp.concatenate(o_heads, axis=0)
    st_inc = lax.dot_general(v_st, k_out, TN_DIMS, preferred_element_type=F32)
    return o_local, st_inc


def _gla_kernel(qf_ref, kf_ref, vf_ref, laf_ref, qr_ref, kr_ref, vr_ref, lar_ref,
                of_ref, or_ref, stf_ref, str_ref):
    @pl.when(pl.program_id(1) == 0)
    def _():
        stf_ref[...] = jnp.zeros_like(stf_ref)
        str_ref[...] = jnp.zeros_like(str_ref)

    c = GLA_CHUNK
    n_chunks = qf_ref.shape[0] // c
    dirs = ((False, qf_ref, kf_ref, vf_ref, laf_ref, of_ref, stf_ref),
            (True, qr_ref, kr_ref, vr_ref, lar_ref, or_ref, str_ref))
    consts = {rev: _gla_consts(rev) for rev in (False, True)}
    units = [(d, n) for n in range(n_chunks) for d in range(2)]
    rows = lambda n: slice(n * c, (n + 1) * c)

    decays = {}
    for d, n in units:
        rev, la_ref = dirs[d][0], dirs[d][4]
        decays[d, n] = _gla_decays(la_ref[rows(n), :], consts[rev], rev)
    operands = {}
    for d, n in units:
        rev, q_ref, k_ref, v_ref = dirs[d][:4]
        operands[d, n] = _gla_operands(q_ref[rows(n), :], k_ref[rows(n), :], v_ref[rows(n), :],
                                       *decays[d, n], consts[rev], rev)
    scores = {}
    for d, n in units:
        scores[d, n] = _gla_scores(operands[d, n][1], operands[d, n][3])
    local = {}
    for d, n in units:
        local[d, n] = _gla_local(scores[d, n], operands[d, n][2], operands[d, n][4], consts[dirs[d][0]])

    for d in range(2):
        rev, o_ref, st_ref = dirs[d][0], dirs[d][5], dirs[d][6]
        st = st_ref[...]
        for n in (range(n_chunks - 1, -1, -1) if rev else range(n_chunks)):
            o_local, st_inc = local[d, n]
            q_in = operands[d, n][0]
            o = o_local + lax.dot_general(q_in, st.astype(BF16), NT_DIMS, preferred_element_type=F32)
            st = st * jnp.exp(decays[d, n][2]) + st_inc
            for hh in range(B_HEADS):
                o_ref[rows(n), hh * B_DV:(hh + 1) * B_DV] = o[hh * c:(hh + 1) * c].astype(BF16)
        st_ref[...] = st


def gla_bidir(qb, kb, vb, laf, lab, n_ctx_tiles):
    bsz, s, _ = qb.shape
    t = ROW_TILE
    nt = s // t

    def rev_tile(j):
        return jnp.where(j < n_ctx_tiles, n_ctx_tiles - 1 - j, nt - 1 - (j - n_ctx_tiles))

    fwd = lambda w: pl.BlockSpec((None, t, w), lambda b, j: (b, j, 0))
    rev = lambda w: pl.BlockSpec((None, t, w), lambda b, j: (b, rev_tile(j), 0))
    return pl.pallas_call(
        _gla_kernel,
        out_shape=[jax.ShapeDtypeStruct((bsz, s, B_V), BF16)] * 2,
        grid=(bsz, nt),
        in_specs=[fwd(B_K), fwd(B_K), fwd(B_V), fwd(B_K), rev(B_K), rev(B_K), rev(B_V), rev(B_K)],
        out_specs=[fwd(B_V), rev(B_V)],
        scratch_shapes=[pltpu.VMEM((B_DV, B_K), F32)] * 2,
        compiler_params=_params("parallel", "arbitrary"),
        name="gla_bidir",
    )(qb, kb, vb, laf, qb, kb, vb, lab)


def _even_out_kernel(xc_ref, xl_ref, mod_ref, a_ref, of_ref, or_ref, g_ref, og_ref, wa_ref, wb_ref, o_ref, *,
                     n_ctx_tiles):
    nb, t, d = o_ref.shape
    rows = nb * t
    o = (of_ref[...].astype(F32) + or_ref[...].astype(F32)).reshape(rows, B_V)
    g = g_ref[...].astype(F32).reshape(rows, B_V)
    parts = []
    for hh in range(B_HEADS):
        oh = o[:, hh * B_DV:(hh + 1) * B_DV]
        ms = jnp.mean(oh * oh, axis=-1, keepdims=True)
        y = oh * lax.rsqrt(ms + NORM_EPS) * og_ref[...]
        parts.append((y * _silu(g[:, hh * B_DV:(hh + 1) * B_DV])).astype(BF16))
    gla = jnp.concatenate(parts, axis=1)
    y = _dot(a_ref[...].reshape(rows, a_ref.shape[-1]), wa_ref[...]) + _dot(gla, wb_ref[...])
    o_ref[...] = _stream_tile(xc_ref, xl_ref, n_ctx_tiles) + mod_ref[:, 2:3, :] * y.reshape(nb, t, d)


def even_out_proj(stream, mods, a, o_f, o_r, gb, o_gain, wa, wb, n_ctx_tiles):
    t = ROW_TILE
    nb = EVEN_OUT_BATCH
    (xc, xl), x_specs = _stream_specs(stream, n_ctx_tiles, t, nb)
    bsz, s, d = a.shape[0], a.shape[1], xc.shape[-1]
    assert bsz % nb == 0
    row = lambda w: pl.BlockSpec((nb, t, w), lambda b, i: (b, i, 0))
    mods_spec = pl.BlockSpec((nb, None, N_MOD, d), lambda b, i: (b, (i >= n_ctx_tiles).astype(jnp.int32), 0, 0))
    return pl.pallas_call(
        functools.partial(_even_out_kernel, n_ctx_tiles=n_ctx_tiles),
        out_shape=jax.ShapeDtypeStruct((bsz, s, d), F32),
        grid=(bsz // nb, s // t),
        in_specs=x_specs + [mods_spec, row(a.shape[-1]), row(B_V), row(B_V), row(B_V),
                            _resident(o_gain.shape), _resident(wa.shape), _resident(wb.shape)],
        out_specs=row(d),
        compiler_params=_params("parallel", "parallel"),
        name="even_out_proj",
    )(xc, xl, mods, a, o_f, o_r, gb, o_gain, wa, wb)


def _ffn_kernel(x_ref, xp_ref, xn_ref, mod_ref, gain_ref, wu_ref, cw_ref, wd_ref, fg_ref, o_ref,
                h_ref, act_ref, *, tile0, n_ctx_tiles, n_tiles, final_norm):
    ti = pl.program_id(1) + tile0
    nb, t, d = x_ref.shape
    d_ff = wd_ref.shape[0]
    gain = gain_ref[...]
    shift = mod_ref[:, 3:4, :]
    scale = mod_ref[:, 4:5, :]
    x = x_ref[...]
    has_prev = jnp.logical_and(ti != 0, ti != n_ctx_tiles)
    has_next = jnp.logical_and(ti != n_ctx_tiles - 1, ti != n_tiles - 1)
    hp = jnp.where(has_prev, _norm_mod(xp_ref[...], gain, shift, scale), 0.0)
    hn = jnp.where(has_next, _norm_mod(xn_ref[...], gain, shift, scale), 0.0)
    seg = t + 2 * FFN_HALO
    h_ref[...] = jnp.concatenate([hp, _norm_mod(x, gain, shift, scale), hn], axis=1).astype(BF16).reshape(nb * seg, d)

    def body(u):
        return jnp.concatenate([u[b * seg + FFN_HALO:b * seg + FFN_HALO + t] for b in range(nb)], axis=0)

    def conv(u, cw):
        prev = body(pltpu.roll(u, 1, 0))
        nxt = body(pltpu.roll(u, nb * seg - 1, 0))
        return cw[0:1] * prev + cw[1:2] * body(u) + cw[2:3] * nxt

    for c0 in range(0, d_ff, FFN_CN):
        gate_cols = slice(c0, c0 + FFN_CN)
        val_cols = slice(d_ff + c0, d_ff + c0 + FFN_CN)
        h = h_ref[...]
        xh = conv(_dot(h, wu_ref[:, gate_cols]), 0.5 * cw_ref[:, gate_cols])
        val = conv(_dot(h, wu_ref[:, val_cols]), cw_ref[:, val_cols])
        act_ref[:, gate_cols] = (xh * (jnp.tanh(xh) + 1.0) * val).astype(BF16)
    y = x + mod_ref[:, 5:6, :] * _dot(act_ref[...], wd_ref[...]).reshape(nb, t, d)
    if final_norm:
        ms = jnp.mean(y * y, axis=-1, keepdims=True)
        y = y * lax.rsqrt(ms + NORM_EPS) * fg_ref[...]
    o_ref[...] = y


def conv_ffn(xx, mods, layer, gain, wu, cw, wd, final_gain, n_ctx_tiles, latents_only, final_norm, t=ROW_TILE):
    bsz, s, d = xx.shape
    assert s % t == 0 and t % FFN_HALO == 0
    nb = FFN_ROWS // t
    assert nb >= 1 and bsz % nb == 0
    nt = s // t
    tile0 = n_ctx_tiles if latents_only else 0
    hb = t // FFN_HALO
    last_hb = s // FFN_HALO - 1
    kernel = functools.partial(_ffn_kernel, tile0=tile0, n_ctx_tiles=n_ctx_tiles, n_tiles=nt, final_norm=final_norm)
    return pl.pallas_call(
        kernel,
        out_shape=jax.ShapeDtypeStruct((bsz, s - tile0 * t, d), F32),
        grid=(bsz // nb, nt - tile0),
        in_specs=[pl.BlockSpec((nb, t, d), lambda b, i: (b, i + tile0, 0)),
                  pl.BlockSpec((nb, FFN_HALO, d), lambda b, i: (b, jnp.maximum((i + tile0) * hb - 1, 0), 0)),
                  pl.BlockSpec((nb, FFN_HALO, d), lambda b, i: (b, jnp.minimum((i + tile0 + 1) * hb, last_hb), 0)),
                  pl.BlockSpec((nb, None, N_MOD, d),
                               lambda b, i: (b, (i + tile0 >= n_ctx_tiles).astype(jnp.int32), 0, 0)),
                  _resident_layer(gain.shape, layer), _resident_layer(wu.shape, layer),
                  _resident_layer(cw.shape, layer), _resident_layer(wd.shape, layer),
                  _resident(final_gain.shape)],
        out_specs=pl.BlockSpec((nb, t, d), lambda b, i: (b, i, 0)),
        scratch_shapes=[pltpu.VMEM((nb * (t + 2 * FFN_HALO), d), BF16),
                        pltpu.VMEM((nb * t, wd.shape[1]), BF16)],
        compiler_params=_params("parallel", "parallel"),
        name="conv_ffn_final" if final_norm else "conv_ffn",
    )(xx, xx, xx, mods, gain, wu, cw, wd, final_gain)


def _odd_mods_spec(bsz, n_ctx_blocks, d):
    return pl.BlockSpec((bsz, None, N_MOD, d), lambda j: (0, (j >= n_ctx_blocks).astype(jnp.int32), 0, 0))


def _odd_in_kernel(x_ref, mod_ref, gain_ref, wg_ref, wr_ref, gate_ref, rec_ref, slab_ref):
    bsz, tt, d = x_ref.shape
    w = wr_ref.shape[1]
    h = _norm_mod(x_ref[...], gain_ref[...], mod_ref[:, 0:1, :], mod_ref[:, 1:2, :])
    h = h.reshape(bsz * tt, d).astype(BF16)
    gate_ref[...] = jax.nn.gelu(_dot(h, wg_ref[...]), approximate=True).astype(BF16).reshape(bsz, tt, w)
    rec = _dot(h, wr_ref[...])
    for b in range(bsz):
        for sl in range(w // LANES):
            slab_ref[sl, b * TM_PITCH:b * TM_PITCH + tt, :] = rec[b * tt:(b + 1) * tt, sl * LANES:(sl + 1) * LANES]
    for t in range(tt):
        for sl in range(w // LANES):
            rec_ref[t, :, sl * LANES:(sl + 1) * LANES] = slab_ref[sl, pl.ds(t, bsz, stride=TM_PITCH), :]


def odd_in_proj(xx, mods, gain, wg, wr, n_ctx_rows):
    bsz, s, d = xx.shape
    tt = LRU_TT
    w = wg.shape[1]
    assert tt <= TM_PITCH and TM_PITCH % 8 == 0
    return pl.pallas_call(
        _odd_in_kernel,
        out_shape=[jax.ShapeDtypeStruct((bsz, s, w), BF16), jax.ShapeDtypeStruct((s, bsz, w), F32)],
        grid=(s // tt,),
        in_specs=[pl.BlockSpec((bsz, tt, d), lambda j: (0, j, 0)), _odd_mods_spec(bsz, n_ctx_rows // tt, d),
                  _resident(gain.shape), _resident(wg.shape), _resident(wr.shape)],
        out_specs=[pl.BlockSpec((bsz, tt, w), lambda j: (0, j, 0)),
                   pl.BlockSpec((tt, bsz, w), lambda j: (j, 0, 0))],
        scratch_shapes=[pltpu.VMEM((w // LANES, bsz * TM_PITCH, LANES), F32)],
        compiler_params=_params("parallel"),
        name="odd_in_proj",
    )(xx, mods, gain, wg, wr)


def _lru_kernel(uf_ref, ufp_ref, ufn_ref, ur_ref, urp_ref, urn_ref, cw_ref, wax_ref, bax_ref, lam_ref,
                hf_ref, hr_ref, ue_ref, a_ref, x_ref, hst_ref, *, n_ctx_blocks, n_blocks):
    j = pl.program_id(0)
    tt, bsz, w = uf_ref.shape

    @pl.when(j == 0)
    def _():
        hst_ref[...] = jnp.zeros_like(hst_ref)

    rev_blk = jnp.where(j < n_ctx_blocks, n_ctx_blocks - 1 - j, n_blocks - 1 - (j - n_ctx_blocks))
    for d, (blk, u_ref, up_ref, un_ref, o_ref) in enumerate(
            ((j, uf_ref, ufp_ref, ufn_ref, hf_ref), (rev_blk, ur_ref, urp_ref, urn_ref, hr_ref))):
        has_prev = jnp.logical_and(blk != 0, blk != n_ctx_blocks)
        has_next = jnp.logical_and(blk != n_ctx_blocks - 1, blk != n_blocks - 1)
        ue_ref[0:1] = jnp.where(has_prev, up_ref[...], 0.0)
        ue_ref[1:tt + 1] = u_ref[...]
        ue_ref[tt + 1:tt + 3] = jnp.where(has_next, un_ref[...], 0.0)
        cw = 0.5 * cw_ref[...]
        uc = cw[0:1] * ue_ref[0:tt] + cw[1:2] * ue_ref[1:tt + 1] + cw[2:3] * ue_ref[2:tt + 2] + cw[3:4] * ue_ref[3:tt + 3]
        uc = uc.reshape(tt * bsz, w)
        half_c = (-0.5 * LRU_C) * _softplus(-lam_ref[d])
        for hh in range(LRU_HEADS):
            cols = slice(hh * LRU_HEAD_DIM, (hh + 1) * LRU_HEAD_DIM)
            uh = uc[:, cols]
            th = jnp.tanh(_dot(uh.astype(BF16), wax_ref[d, hh]) + bax_ref[d, hh])
            log_a = th[:, 0:LRU_HEAD_DIM] * half_c[:, cols] + half_c[:, cols]
            a = jnp.exp(log_a)
            m2 = jnp.tanh(log_a) * (-1.0 - a * a)
            mult = m2 * lax.rsqrt(jnp.maximum(m2, jnp.finfo(F32).tiny))
            a_ref[:, :, cols] = a.reshape(tt, bsz, LRU_HEAD_DIM)
            x_ref[:, :, cols] = (mult * (th[:, LRU_HEAD_DIM:2 * LRU_HEAD_DIM] + 1.0) * uh).reshape(tt, bsz, LRU_HEAD_DIM)
        h = hst_ref[d]
        for step in range(tt):
            tcur = tt - 1 - step if d == 1 else step
            h = a_ref[tcur] * h + x_ref[tcur]
            o_ref[tcur] = h
        hst_ref[d] = h


def lru_scan(u, cw, wax, bax, lam, n_ctx_rows):
    s, bsz, w = u.shape
    tt = LRU_TT
    nb = s // tt
    ncb = n_ctx_rows // tt

    def rev_blk(j):
        return jnp.where(j < ncb, ncb - 1 - j, nb - 1 - (j - ncb))

    def specs(blk):
        return [pl.BlockSpec((tt, bsz, w), lambda j: (blk(j), 0, 0)),
                pl.BlockSpec((1, bsz, w), lambda j: (jnp.maximum(blk(j) * tt - 1, 0), 0, 0)),
                pl.BlockSpec((2, bsz, w), lambda j: (jnp.minimum((blk(j) + 1) * (tt // 2), s // 2 - 1), 0, 0))]

    fwd_blk = lambda j: j
    kernel = functools.partial(_lru_kernel, n_ctx_blocks=ncb, n_blocks=nb)
    return pl.pallas_call(
        kernel,
        out_shape=[jax.ShapeDtypeStruct((s, bsz, w), F32)] * 2,
        grid=(nb,),
        in_specs=specs(fwd_blk) + specs(rev_blk) + [_resident(cw.shape), _resident(wax.shape),
                                                     _resident(bax.shape), _resident(lam.shape)],
        out_specs=[pl.BlockSpec((tt, bsz, w), lambda j: (j, 0, 0)),
                   pl.BlockSpec((tt, bsz, w), lambda j: (rev_blk(j), 0, 0))],
        scratch_shapes=[pltpu.VMEM((tt + 3, bsz, w), F32), pltpu.VMEM((tt, bsz, w), F32),
                        pltpu.VMEM((tt, bsz, w), F32), pltpu.VMEM((2, bsz, w), F32)],
        compiler_params=_params("arbitrary"),
        name="lru_scan",
    )(u, u, u, u, u, u, cw, wax, bax, lam)


def _odd_out_kernel(x_ref, mod_ref, gate_ref, hf_ref, hr_ref, w_ref, o_ref, slab_ref):
    bsz, tt, d = x_ref.shape
    w = w_ref.shape[0]
    for t in range(tt):
        hs = hf_ref[t] + hr_ref[t]
        for sl in range(w // LANES):
            slab_ref[sl, pl.ds(t, bsz, stride=TM_PITCH), :] = hs[:, sl * LANES:(sl + 1) * LANES]
    rec = jnp.concatenate(
        [jnp.concatenate([slab_ref[sl, b * TM_PITCH:b * TM_PITCH + tt, :] for sl in range(w // LANES)], axis=1)
         for b in range(bsz)], axis=0)
    mixed = (gate_ref[...].reshape(bsz * tt, w) * rec).astype(BF16)
    y = _dot(mixed, w_ref[...]).reshape(bsz, tt, d)
    o_ref[...] = x_ref[...] + mod_ref[:, 2:3, :] * y


def odd_out_proj(xx, mods, gate, hf, hr, w_out, n_ctx_rows, latents_only):
    bsz, s, d = xx.shape
    tt = LRU_TT
    w = gate.shape[-1]
    j0 = n_ctx_rows // tt if latents_only else 0
    bmaj = lambda n: pl.BlockSpec((bsz, tt, n), lambda j: (0, j + j0, 0))
    tmaj = pl.BlockSpec((tt, bsz, w), lambda j: (j + j0, 0, 0))
    mods_spec = pl.BlockSpec((bsz, None, N_MOD, d),
                             lambda j: (0, (j + j0 >= n_ctx_rows // tt).astype(jnp.int32), 0, 0))
    return pl.pallas_call(
        _odd_out_kernel,
        out_shape=jax.ShapeDtypeStruct((bsz, s - j0 * tt, d), F32),
        grid=(s // tt - j0,),
        in_specs=[bmaj(d), mods_spec, bmaj(w), tmaj, tmaj, _resident(w_out.shape)],
        out_specs=pl.BlockSpec((bsz, tt, d), lambda j: (0, j, 0)),
        scratch_shapes=[pltpu.VMEM((w // LANES, bsz * TM_PITCH, LANES), F32)],
        compiler_params=_params("parallel"),
        name="odd_out_proj",
    )(xx, mods, gate, hf, hr, w_out)


def _rope_pair_slab(a, b):
    n = A_HEAD_DIM // 4
    part = lambda t, i: t[..., i * n:(i + 1) * n]
    return jnp.concatenate([part(a, 0), part(a, 2), part(b, 0), part(b, 2),
                            part(a, 1), part(a, 3), part(b, 1), part(b, 3)], axis=-1)


def _rope_tables(n_ctx, n_lat):
    rows = n_lat // GRID_W
    row = jnp.repeat(jnp.arange(rows, dtype=F32), GRID_W)
    col = jnp.tile(jnp.arange(GRID_W, dtype=F32), rows)
    n_freq = A_HEAD_DIM // 4
    inv_freq = ROPE_THETA ** (-jnp.arange(n_freq, dtype=F32) / n_freq)
    ar = row[:, None] * inv_freq
    ac = col[:, None] * inv_freq
    cos = jnp.concatenate([jnp.cos(ar), jnp.cos(ar), jnp.cos(ac), jnp.cos(ac)], axis=-1)
    sin = jnp.concatenate([-jnp.sin(ar), jnp.sin(ar), -jnp.sin(ac), jnp.sin(ac)], axis=-1)
    cos = jnp.concatenate([jnp.ones((n_ctx, A_HEAD_DIM), F32), cos], axis=0)
    sin = jnp.concatenate([jnp.zeros((n_ctx, A_HEAD_DIM), F32), sin], axis=0)
    return _rope_pair_slab(cos, cos), _rope_pair_slab(sin, sin)


def kernel(x, c, ctx, c_ctx, ada_w, ada_b, norm_mix, norm_ffn, ffn_w_up, ffn_conv, ffn_w_down, even_w_in, even_w_out, attn_q_gain, attn_k_gain, gla_gate_w_up, gla_gate_b, gla_out_gain, lru_w_in, lru_conv, lru_lambda, lru_w_a, lru_b_a, lru_w_x, lru_b_x, lru_w_out, final_gain):
    bsz, n_lat, d = x.shape
    n_ctx = ctx.shape[1]
    depth = ada_w.shape[0]
    d_ff = ffn_w_down.shape[1]
    assert n_ctx % ROW_TILE == 0 and n_lat % ROW_TILE == 0 and d_ff % FFN_CN == 0
    n_ctx_tiles = n_ctx // ROW_TILE
    s = n_ctx + n_lat

    ffn_weights = (norm_ffn.reshape(depth, 1, d), ffn_w_up.astype(BF16), ffn_conv, ffn_w_down.astype(BF16),
                   final_gain.reshape(1, d))

    xx = (ctx, x)
    cond_rows = -(-(bsz + 1) // 8) * 8
    cond = jnp.zeros((cond_rows, d), F32).at[:bsz].set(c).at[bsz].set(c_ctx)
    cos, sin = _rope_tables(n_ctx, n_lat)
    pad_gain = lambda g: _rope_pair_slab(g.reshape(1, A_HEAD_DIM), g.reshape(1, A_HEAD_DIM))

    for l in range(depth):
        last = l == depth - 1
        j = l // 2
        table = adaln_table(cond, ada_w, ada_b.reshape(depth, 1, N_MOD * d), l)
        m_lat = table[:bsz].reshape(bsz, N_MOD, d)
        m_ctx = jnp.broadcast_to(table[bsz].reshape(1, N_MOD, d), (bsz, N_MOD, d))
        mods = jnp.stack([m_ctx, m_lat], axis=1)
        gain_mix = norm_mix[l].reshape(1, d)

        if l % 2 == 0:
            w_in = even_w_in[j]
            o0 = A_HEADS * A_HEAD_DIM
            o1 = o0 + A_KV_HEADS * A_HEAD_DIM
            o2 = o1 + A_KV_HEADS * A_HEAD_DIM
            o3 = o2 + 2 * B_K + 2 * B_V
            head = lambda hh: w_in[:, hh * A_HEAD_DIM:(hh + 1) * A_HEAD_DIM]
            wqkv = jnp.concatenate(
                [_rope_pair_slab(head(m), head(m + A_HEADS // 2)) for m in range(A_HEADS // 2)]
                + [_rope_pair_slab(head(A_HEADS), head(A_HEADS + 1)), w_in[:, o1:o2]], axis=1).astype(BF16)
            wb = w_in[:, o2:o3].astype(BF16)
            wlr = w_in[:, o3:].astype(BF16)
            zeros = jnp.zeros((B_GATE_RANK, B_K), F32)
            wgate = jnp.concatenate([jnp.concatenate([gla_gate_w_up[j, 0], zeros], axis=1),
                                     jnp.concatenate([zeros, gla_gate_w_up[j, 1]], axis=1)], axis=0)
            bgate = gla_gate_b[j].reshape(1, 2 * B_K)
            s_bound = A_HEAD_DIM ** 0.5 * jnp.max(jnp.abs(attn_q_gain[j])) * jnp.max(jnp.abs(attn_k_gain[j]))
            qaug = jnp.zeros((2, LANES), F32)
            qaug = qaug.at[0, ATT_AUG_LANES[0]].set(-s_bound * LOG2E).at[1, ATT_AUG_LANES[1]].set(-s_bound * LOG2E)
            q, k, v, qb, kb, vb, gb, laf, lab = even_in_proj(
                xx, mods, gain_mix, wqkv, wb, wlr, wgate, bgate,
                pad_gain(attn_q_gain[j]), pad_gain(attn_k_gain[j]), qaug, cos, sin, n_ctx_tiles)
            a = attention(s_bound.reshape(1), q, k, v, n_ctx)
            o_f, o_r = gla_bidir(qb, kb, vb, laf, lab, n_ctx_tiles)
            w_out = even_w_out[j]
            xx = even_out_proj(xx, mods, a, o_f, o_r, gb, gla_out_gain[j].reshape(1, B_DV),
                               w_out[:o0].astype(BF16), w_out[o0:].astype(BF16), n_ctx_tiles)
        else:
            w_in = lru_w_in[j]
            gate, rec = odd_in_proj(xx, mods, gain_mix, w_in[:, :LRU_WIDTH].astype(BF16),
                                    w_in[:, LRU_WIDTH:].astype(BF16), n_ctx)
            wax = jnp.concatenate([lru_w_a[j], lru_w_x[j]], axis=-1).astype(BF16)
            bax = 0.5 * jnp.concatenate([lru_b_a[j].reshape(2, LRU_HEADS, 1, LRU_HEAD_DIM),
                                         lru_b_x[j].reshape(2, LRU_HEADS, 1, LRU_HEAD_DIM)], axis=-1)
            hf, hr = lru_scan(rec, lru_conv[j], wax, bax, lru_lambda[j].reshape(2, 1, LRU_WIDTH), n_ctx)
            xx = odd_out_proj(xx, mods, gate, hf, hr, lru_w_out[j].astype(BF16), n_ctx, latents_only=last)

        if last and xx.shape[1] == n_lat:
            t = FFN_LATENT_TILE if n_lat % FFN_LATENT_TILE == 0 else ROW_TILE
            xx = conv_ffn(xx, mods, l, *ffn_weights, 0, latents_only=False, final_norm=True, t=t)
        else:
            xx = conv_ffn(xx, mods, l, *ffn_weights, n_ctx_tiles, latents_only=last, final_norm=last)
    return xx
```

```python
import functools

import numpy as np
import jax
import jax.numpy as jnp
from jax import lax
from jax.experimental import pallas as pl
from jax.experimental.pallas import tpu as pltpu

F32 = jnp.float32
BF16 = jnp.bfloat16

NORM_EPS = 1e-6
N_MOD = 6
GRID_W = 64
ROPE_THETA = 10000.0

A_HEADS = 8
A_KV_HEADS = 2
A_GROUP = A_HEADS // A_KV_HEADS
A_HEAD_DIM = 64

B_HEADS = 4
B_DK = 64
B_DV = 128
B_K = B_HEADS * B_DK
B_V = B_HEADS * B_DV
B_GATE_RANK = 16
B_GATE_TAU = 16.0
GLA_CHUNK = 64
GLA_SUB = 16
GLA_BATCH = 2

LRU_HEADS = 10
LRU_HEAD_DIM = 128
LRU_WIDTH = LRU_HEADS * LRU_HEAD_DIM
LRU_C = 8.0

LANES = 128
ROW_TILE = 256
EVEN_BATCH = 2
ATT_TQ = 256
ATT_TK = 256
FFN_CN = 256
FFN_HALO = 8
FFN_LATENT_TILE = 512
FFN_ROWS = 512
LRU_TT = 64
TM_PITCH = 72
VMEM_LIMIT = 56 * 1024 * 1024

LOG2E = float(np.log2(np.e))
ATT_MAX_STATIC_SHIFT = 40.0
ATT_AUG_LANES = (32, 0)

NT_DIMS = (((1,), (1,)), ((), ()))
TN_DIMS = (((0,), (0,)), ((), ()))


def _params(*sem):
    return pltpu.CompilerParams(dimension_semantics=sem, vmem_limit_bytes=VMEM_LIMIT)


def _resident(shape):
    nd = len(shape)
    return pl.BlockSpec(shape, lambda *_: (0,) * nd, pipeline_mode=pl.Buffered(1))


def _resident_layer(shape, layer):
    nd = len(shape)
    return pl.BlockSpec((None,) + tuple(shape[1:]), lambda *_: (layer,) + (0,) * (nd - 1),
                        pipeline_mode=pl.Buffered(1))


def _dot(a, b):
    return jnp.dot(a, b, preferred_element_type=F32)


def _split_bf16(a):
    hi = a.astype(BF16)
    lo = (a - hi.astype(F32)).astype(BF16)
    return hi, lo


def _dot_f32(a, b):
    ah, al = _split_bf16(a)
    bh, bl = _split_bf16(b)
    return _dot(ah, bh) + _dot(ah, bl) + _dot(al, bh)


def _sigmoid(x):
    return 0.5 * jnp.tanh(0.5 * x) + 0.5


def _silu(x):
    return x * _sigmoid(x)


def _softplus(x):
    return jnp.maximum(x, 0.0) + jnp.log1p(jnp.exp(-jnp.abs(x)))


def _norm_mod(x, gain, shift, scale):
    ms = jnp.mean(x * x, axis=-1, keepdims=True)
    return (x * lax.rsqrt(ms + NORM_EPS) * gain) * (1.0 + scale) + shift


def _adaln_kernel(c_ref, w_ref, b_ref, o_ref):
    o_ref[...] = _dot_f32(_silu(c_ref[...]), w_ref[...]) + b_ref[...]


def adaln_table(cond, w, b, layer):
    rows, d = cond.shape
    n = w.shape[2]
    tn = 768
    return pl.pallas_call(
        _adaln_kernel,
        out_shape=jax.ShapeDtypeStruct((rows, n), F32),
        grid=(n // tn,),
        in_specs=[pl.BlockSpec((rows, d), lambda j: (0, 0)),
                  pl.BlockSpec((None, d, tn), lambda j: (layer, 0, j)),
                  pl.BlockSpec((None, 1, tn), lambda j: (layer, 0, j))],
        out_specs=pl.BlockSpec((rows, tn), lambda j: (0, j)),
        compiler_params=_params("arbitrary"),
        name="adaln_table",
    )(cond, w, b)


def _stream_specs(stream, n_ctx_tiles, t, nb=None):
    if isinstance(stream, tuple):
        ctx_arr, lat_arr = stream
        lat_off = 0
    else:
        ctx_arr = lat_arr = stream
        lat_off = n_ctx_tiles
    d = ctx_arr.shape[-1]
    ctx_spec = pl.BlockSpec((nb, t, d), lambda b, i: (b, jnp.minimum(i, n_ctx_tiles - 1), 0))
    lat_spec = pl.BlockSpec((nb, t, d), lambda b, i: (b, jnp.maximum(i - n_ctx_tiles, 0) + lat_off, 0))
    return (ctx_arr, lat_arr), [ctx_spec, lat_spec]


def _stream_tile(xc_ref, xl_ref, n_ctx_tiles):
    return jnp.where(pl.program_id(1) < n_ctx_tiles, xc_ref[...], xl_ref[...])


def _even_in_kernel(xc_ref, xl_ref, mod_ref, gain_ref, wqkv_ref, wb_ref, wlr_ref, wgate_ref, bgate_ref,
                    qg_ref, kg_ref, qaug_ref, cos_ref, sin_ref,
                    q_ref, k_ref, v_ref, qb_ref, kb_ref, vb_ref, gb_ref, laf_ref, lab_ref, *, n_ctx_tiles):
    nb, t, d = xl_ref.shape
    x = _stream_tile(xc_ref, xl_ref, n_ctx_tiles)
    h = _norm_mod(x, gain_ref[...], mod_ref[:, 0:1, :], mod_ref[:, 1:2, :]).reshape(nb * t, d).astype(BF16)
    cos = jnp.concatenate([cos_ref[...]] * nb, axis=0)
    sin = jnp.concatenate([sin_ref[...]] * nb, axis=0)
    lane = lax.broadcasted_iota(jnp.int32, cos.shape, 1)

    def put_head(ref, hh, val):
        for b in range(nb):
            ref[b, hh] = val[b * t:(b + 1) * t]

    def put_rows(ref, val):
        ref[...] = val.reshape(nb, t, val.shape[-1])
    first = lane % (LANES // 2) < LANES // 4
    low = lane < A_HEAD_DIM

    def pair_norm_rope(xp, g):
        sq = xp * xp
        both = jnp.sum(sq, axis=-1, keepdims=True)
        ms_first = jnp.sum(jnp.where(first, sq, 0.0), axis=-1, keepdims=True)
        inv = jnp.where(first, lax.rsqrt(ms_first * (1.0 / A_HEAD_DIM) + NORM_EPS),
                        lax.rsqrt((both - ms_first) * (1.0 / A_HEAD_DIM) + NORM_EPS))
        y = xp * inv * g
        return y * cos + pltpu.roll(y, LANES // 2, 1) * sin

    qkv = _dot(h, wqkv_ref[...])
    slab = lambda i: qkv[:, i * LANES:(i + 1) * LANES]
    n_pairs = A_HEADS // 2
    for m in range(n_pairs):
        qp = pair_norm_rope(slab(m), qg_ref[...]) * (A_HEAD_DIM ** -0.5 * LOG2E)
        put_head(q_ref, m, (jnp.where(first, qp, 0.0) + qaug_ref[0:1, :]).astype(BF16))
        put_head(q_ref, m + n_pairs, (jnp.where(first, 0.0, qp) + qaug_ref[1:2, :]).astype(BF16))
    kp = pair_norm_rope(slab(n_pairs), kg_ref[...])
    put_head(k_ref, 0, (jnp.where(first, kp, 0.0) + (lane == ATT_AUG_LANES[0]).astype(F32)).astype(BF16))
    put_head(k_ref, 1, (jnp.where(first, 0.0, kp) + (lane == ATT_AUG_LANES[1]).astype(F32)).astype(BF16))
    vp = slab(n_pairs + 1)
    put_head(v_ref, 0, jnp.where(low, vp, 1.0).astype(BF16))
    put_head(v_ref, 1, jnp.where(low, pltpu.roll(vp, A_HEAD_DIM, 1), 1.0).astype(BF16))

    pb = _dot(h, wb_ref[...])
    put_rows(qb_ref, pb[:, 0:B_K] * B_DK ** -0.5)
    put_rows(kb_ref, pb[:, B_K:2 * B_K])
    put_rows(vb_ref, pb[:, 2 * B_K:2 * B_K + B_V])
    put_rows(gb_ref, pb[:, 2 * B_K + B_V:2 * B_K + 2 * B_V].astype(BF16))

    lr = _dot(h, wlr_ref[...])
    z = _dot_f32(lr, wgate_ref[...]) + bgate_ref[...]
    la = -_softplus(-z) * (1.0 / B_GATE_TAU)
    put_rows(laf_ref, la[:, 0:B_K])
    put_rows(lab_ref, la[:, B_K:2 * B_K])


def even_in_proj(stream, mods, gain, wqkv, wb, wlr, wgate, bgate, qg, kg, qaug, cos, sin, n_ctx_tiles):
    t = ROW_TILE
    nb = EVEN_BATCH
    (xc, xl), x_specs = _stream_specs(stream, n_ctx_tiles, t, nb)
    bsz, d = xc.shape[0], xc.shape[-1]
    assert bsz % nb == 0
    s = cos.shape[0]
    row = lambda w: pl.BlockSpec((nb, t, w), lambda b, i: (b, i, 0))
    heads = lambda n: pl.BlockSpec((nb, n, t, LANES), lambda b, i: (b, 0, i, 0))
    tab = pl.BlockSpec((t, LANES), lambda b, i: (i, 0))
    mods_spec = pl.BlockSpec((nb, None, N_MOD, d), lambda b, i: (b, (i >= n_ctx_tiles).astype(jnp.int32), 0, 0))
    out_shape = [
        jax.ShapeDtypeStruct((bsz, A_HEADS, s, LANES), BF16),
        jax.ShapeDtypeStruct((bsz, A_KV_HEADS, s, LANES), BF16),
        jax.ShapeDtypeStruct((bsz, A_KV_HEADS, s, LANES), BF16),
        jax.ShapeDtypeStruct((bsz, s, B_K), F32),
        jax.ShapeDtypeStruct((bsz, s, B_K), F32),
        jax.ShapeDtypeStruct((bsz, s, B_V), F32),
        jax.ShapeDtypeStruct((bsz, s, B_V), BF16),
        jax.ShapeDtypeStruct((bsz, s, B_K), F32),
        jax.ShapeDtypeStruct((bsz, s, B_K), F32),
    ]
    return pl.pallas_call(
        functools.partial(_even_in_kernel, n_ctx_tiles=n_ctx_tiles),
        out_shape=out_shape,
        grid=(bsz // nb, s // t),
        in_specs=x_specs + [mods_spec, _resident(gain.shape),
                  _resident(wqkv.shape), _resident(wb.shape), _resident(wlr.shape),
                  _resident(wgate.shape), _resident(bgate.shape), _resident(qg.shape), _resident(kg.shape),
                  _resident(qaug.shape), tab, tab],
        out_specs=[heads(A_HEADS), heads(A_KV_HEADS), heads(A_KV_HEADS),
                   row(B_K), row(B_K), row(B_V), row(B_V), row(B_K), row(B_K)],
        compiler_params=_params("parallel", "parallel"),
        name="even_in_proj",
    )(xc, xl, mods, gain, wqkv, wb, wlr, wgate, bgate, qg, kg, qaug, cos, sin)


def _attn_kernel(shift_ref, q_ref, k_ref, v_ref, o_ref, s_ref, p_ref, m_ref, *, n_ctx_q_tiles, n_ctx_rows):
    i = pl.program_id(2)
    tq = q_ref.shape[1]
    rows = A_GROUP * tq
    n_keys = k_ref.shape[0]

    def finish(acc):
        out = acc / acc[:, A_HEAD_DIM:A_HEAD_DIM + 1]
        low = lax.broadcasted_iota(jnp.int32, (tq, LANES), 1) < A_HEAD_DIM
        for j in range(0, A_GROUP, 2):
            even = out[j * tq:(j + 1) * tq]
            odd = pltpu.roll(out[(j + 1) * tq:(j + 2) * tq], A_HEAD_DIM, 1)
            o_ref[:, (j // 2) * LANES:(j // 2 + 1) * LANES] = jnp.where(low, even, odd).astype(BF16)

    def attend_static_shift(nk):
        qs = q_ref[...].reshape(rows, LANES)
        acc = jnp.zeros((rows, LANES), F32)
        for c0 in range(0, nk, ATT_TK):
            s = lax.dot_general(qs, k_ref[c0:c0 + ATT_TK, :], NT_DIMS, preferred_element_type=F32)
            acc = acc + _dot(jnp.exp2(s).astype(BF16), v_ref[c0:c0 + ATT_TK, :])
        finish(acc)

    def attend_row_max(nk):
        qs = q_ref[...].reshape(rows, LANES)
        m_ref[...] = jnp.full(m_ref.shape, -jnp.inf, F32)
        for c0 in range(0, nk, ATT_TK):
            cols = slice(c0, c0 + ATT_TK)
            s = lax.dot_general(qs, k_ref[cols, :], NT_DIMS, preferred_element_type=F32)
            s_ref[:, cols] = s
            m_ref[...] = jnp.maximum(m_ref[...], jnp.maximum(s[:, 0:LANES], s[:, LANES:2 * LANES]))
        m = jnp.max(m_ref[...], axis=1, keepdims=True)
        for c0 in range(0, nk, ATT_TK):
            cols = slice(c0, c0 + ATT_TK)
            p_ref[:, cols] = jnp.exp2(s_ref[:, cols] - m).astype(BF16)
        finish(_dot(p_ref[:, 0:nk], v_ref[0:nk, :]))

    is_ctx = i < n_ctx_q_tiles
    static_ok = shift_ref[0] <= ATT_MAX_STATIC_SHIFT
    for ctx_tile, nk in ((True, n_ctx_rows), (False, n_keys)):
        tile_match = is_ctx if ctx_tile else jnp.logical_not(is_ctx)

        @pl.when(jnp.logical_and(tile_match, static_ok))
        def _():
            attend_static_shift(nk)

        @pl.when(jnp.logical_and(tile_match, jnp.logical_not(static_ok)))
        def _():
            attend_row_max(nk)


def attention(shift, q, k, v, n_ctx_rows):
    bsz, _, s, _ = q.shape
    tq = ATT_TQ
    assert ATT_TK == 2 * LANES and n_ctx_rows % ATT_TK == 0 and s % ATT_TK == 0
    kernel = functools.partial(_attn_kernel, n_ctx_q_tiles=n_ctx_rows // tq, n_ctx_rows=n_ctx_rows)
    kv_spec = pl.BlockSpec((None, None, s, LANES), lambda b, g, i: (b, g, 0, 0))
    return pl.pallas_call(
        kernel,
        out_shape=jax.ShapeDtypeStruct((bsz, s, A_HEADS * A_HEAD_DIM), BF16),
        grid=(bsz, A_KV_HEADS, s // tq),
        in_specs=[pl.BlockSpec(memory_space=pltpu.SMEM),
                  pl.BlockSpec((None, A_GROUP, tq, LANES), lambda b, g, i: (b, g, i, 0)), kv_spec, kv_spec],
        out_specs=pl.BlockSpec((None, tq, A_GROUP * A_HEAD_DIM), lambda b, g, i: (b, i, g)),
        scratch_shapes=[pltpu.VMEM((A_GROUP * tq, s), F32),
                        pltpu.VMEM((A_GROUP * tq, s), BF16),
                        pltpu.VMEM((A_GROUP * tq, LANES), F32)],
        compiler_params=_params("parallel", "parallel", "arbitrary"),
        name="gqa_attention",
    )(shift, q, k, v)


def _gla_consts(reverse):
    c, sub = GLA_CHUNK, GLA_SUB
    t = lax.broadcasted_iota(jnp.int32, (c, c), 0)
    s = lax.broadcasted_iota(jnp.int32, (c, c), 1)
    if reverse:
        tri = s >= t
        blk = s >= (t // sub + 1) * sub
    else:
        tri = s <= t
        blk = s < (t // sub) * sub
    sums = jnp.concatenate([tri, blk], axis=0).astype(BF16)
    sums = jnp.concatenate([sums, sums], axis=1)
    lane = lax.broadcasted_iota(jnp.int32, (1, B_K), 1)
    head_masks = [(lane // B_DK == hh).astype(F32) for hh in range(B_HEADS)]
    key_row = lax.broadcasted_iota(jnp.int32, (c, 1), 0)
    return sums, tri, head_masks, key_row


def _gla_decays(la, consts, reverse):
    sums = consts[0]
    c = GLA_CHUNK
    cr = _dot(sums, jnp.concatenate(_split_bf16(la), axis=0))
    cum = cr[0:c]
    ref = cr[c:2 * c]
    edge = 0 if reverse else c - 1
    return cum, ref, cum[edge:edge + 1]


def _gla_operands(q, k, v, cum, ref, total, consts, reverse):
    _, _, head_masks, key_row = consts
    c, sub = GLA_CHUNK, GLA_SUB

    def stack_heads(x):
        return jnp.concatenate([x * hm for hm in head_masks], axis=0)

    q_in = stack_heads(q * jnp.exp(cum)).astype(BF16)
    q_loc = stack_heads(q * jnp.exp(cum - ref)).astype(BF16)
    k_out = stack_heads(k * jnp.exp(total - cum)).astype(BF16)
    k_sub = []
    for i in range(c // sub):
        ref_i = ref[i * sub:i * sub + 1]
        valid = (key_row >= i * sub) if reverse else (key_row < (i + 1) * sub)
        k_sub.append((k * jnp.exp(jnp.where(valid, ref_i - cum, -jnp.inf))).astype(BF16))
    v_st = jnp.concatenate([v[:, hh * B_DV:(hh + 1) * B_DV] for hh in range(B_HEADS)], axis=0).astype(BF16)
    return q_in, q_loc, k_out, k_sub, v_st


def _gla_scores(q_loc, k_sub):
    c, sub = GLA_CHUNK, GLA_SUB
    pieces = []
    for i in range(c // sub):
        q_i = jnp.concatenate([q_loc[hh * c + i * sub:hh * c + (i + 1) * sub] for hh in range(B_HEADS)], axis=0)
        pieces.append(lax.dot_general(q_i, k_sub[i], NT_DIMS, preferred_element_type=F32))
    return pieces


def _gla_local(pieces, k_out, v_st, consts):
    causal = consts[1]
    c, sub = GLA_CHUNK, GLA_SUB
    o_heads = []
    for hh in range(B_HEADS):
        s_h = jnp.concatenate([p[hh * sub:(hh + 1) * sub] for p in pieces], axis=0)
        s_h = jnp.where(causal, s_h, 0.0).astype(BF16)
        o_heads.append(_dot(s_h, v_st[hh * c:(hh + 1) * c]))
    o_local = jnp.concatenate(o_heads, axis=0)
    st_inc = lax.dot_general(v_st, k_out, TN_DIMS, preferred_element_type=F32)
    return o_local, st_inc


def _gla_kernel(qf_ref, kf_ref, vf_ref, laf_ref, qr_ref, kr_ref, vr_ref, lar_ref,
                of_ref, or_ref, stf_ref, str_ref):
    @pl.when(pl.program_id(1) == 0)
    def _():
        stf_ref[...] = jnp.zeros_like(stf_ref)
        str_ref[...] = jnp.zeros_like(str_ref)

    c = GLA_CHUNK
    nb = qf_ref.shape[0]
    n_chunks = qf_ref.shape[1] // c
    dirs = ((False, qf_ref, kf_ref, vf_ref, laf_ref, of_ref, stf_ref),
            (True, qr_ref, kr_ref, vr_ref, lar_ref, or_ref, str_ref))
    consts = {rev: _gla_consts(rev) for rev in (False, True)}
    units = [(b, d, n) for n in range(n_chunks) for b in range(nb) for d in range(2)]
    rows = lambda n: slice(n * c, (n + 1) * c)

    decays = {}
    for b, d, n in units:
        rev, la_ref = dirs[d][0], dirs[d][4]
        decays[b, d, n] = _gla_decays(la_ref[b, rows(n), :], consts[rev], rev)
    operands = {}
    for b, d, n in units:
        rev, q_ref, k_ref, v_ref = dirs[d][:4]
        operands[b, d, n] = _gla_operands(q_ref[b, rows(n), :], k_ref[b, rows(n), :], v_ref[b, rows(n), :],
                                          *decays[b, d, n], consts[rev], rev)
    scores = {}
    for u in units:
        scores[u] = _gla_scores(operands[u][1], operands[u][3])
    local = {}
    for u in units:
        local[u] = _gla_local(scores[u], operands[u][2], operands[u][4], consts[dirs[u[1]][0]])

    for b in range(nb):
        for d in range(2):
            rev, o_ref, st_ref = dirs[d][0], dirs[d][5], dirs[d][6]
            st = st_ref[b]
            for n in (range(n_chunks - 1, -1, -1) if rev else range(n_chunks)):
                o_local, st_inc = local[b, d, n]
                q_in = operands[b, d, n][0]
                o = o_local + lax.dot_general(q_in, st.astype(BF16), NT_DIMS, preferred_element_type=F32)
                st = st * jnp.exp(decays[b, d, n][2]) + st_inc
                for hh in range(B_HEADS):
                    o_ref[b, rows(n), hh * B_DV:(hh + 1) * B_DV] = o[hh * c:(hh + 1) * c].astype(BF16)
            st_ref[b] = st


def gla_bidir(qb, kb, vb, laf, lab, n_ctx_tiles):
    bsz, s, _ = qb.shape
    t = ROW_TILE
    nt = s // t

    def rev_tile(j):
        return jnp.where(j < n_ctx_tiles, n_ctx_tiles - 1 - j, nt - 1 - (j - n_ctx_tiles))

    nb = GLA_BATCH
    assert bsz % nb == 0
    fwd = lambda w: pl.BlockSpec((nb, t, w), lambda b, j: (b, j, 0))
    rev = lambda w: pl.BlockSpec((nb, t, w), lambda b, j: (b, rev_tile(j), 0))
    return pl.pallas_call(
        _gla_kernel,
        out_shape=[jax.ShapeDtypeStruct((bsz, s, B_V), BF16)] * 2,
        grid=(bsz // nb, nt),
        in_specs=[fwd(B_K), fwd(B_K), fwd(B_V), fwd(B_K), rev(B_K), rev(B_K), rev(B_V), rev(B_K)],
        out_specs=[fwd(B_V), rev(B_V)],
        scratch_shapes=[pltpu.VMEM((nb, B_DV, B_K), F32)] * 2,
        compiler_params=_params("parallel", "arbitrary"),
        name="gla_bidir",
    )(qb, kb, vb, laf, qb, kb, vb, lab)


def _even_out_kernel(xc_ref, xl_ref, mod_ref, a_ref, of_ref, or_ref, g_ref, og_ref, wa_ref, wb_ref, o_ref, *,
                     n_ctx_tiles):
    nb, t, d = o_ref.shape
    rows = nb * t
    o = (of_ref[...].astype(F32) + or_ref[...].astype(F32)).reshape(rows, B_V)
    g = g_ref[...].astype(F32).reshape(rows, B_V)
    parts = []
    for hh in range(B_HEADS):
        oh = o[:, hh * B_DV:(hh + 1) * B_DV]
        ms = jnp.mean(oh * oh, axis=-1, keepdims=True)
        y = oh * lax.rsqrt(ms + NORM_EPS) * og_ref[...]
        parts.append((y * _silu(g[:, hh * B_DV:(hh + 1) * B_DV])).astype(BF16))
    gla = jnp.concatenate(parts, axis=1)
    y = _dot(a_ref[...].reshape(rows, a_ref.shape[-1]), wa_ref[...]) + _dot(gla, wb_ref[...])
    o_ref[...] = _stream_tile(xc_ref, xl_ref, n_ctx_tiles) + mod_ref[:, 2:3, :] * y.reshape(nb, t, d)


def even_out_proj(stream, mods, a, o_f, o_r, gb, o_gain, wa, wb, n_ctx_tiles):
    t = ROW_TILE
    nb = EVEN_BATCH
    (xc, xl), x_specs = _stream_specs(stream, n_ctx_tiles, t, nb)
    bsz, s, d = a.shape[0], a.shape[1], xc.shape[-1]
    assert bsz % nb == 0
    row = lambda w: pl.BlockSpec((nb, t, w), lambda b, i: (b, i, 0))
    mods_spec = pl.BlockSpec((nb, None, N_MOD, d), lambda b, i: (b, (i >= n_ctx_tiles).astype(jnp.int32), 0, 0))
    return pl.pallas_call(
        functools.partial(_even_out_kernel, n_ctx_tiles=n_ctx_tiles),
        out_shape=jax.ShapeDtypeStruct((bsz, s, d), F32),
        grid=(bsz // nb, s // t),
        in_specs=x_specs + [mods_spec, row(a.shape[-1]), row(B_V), row(B_V), row(B_V),
                            _resident(o_gain.shape), _resident(wa.shape), _resident(wb.shape)],
        out_specs=row(d),
        compiler_params=_params("parallel", "parallel"),
        name="even_out_proj",
    )(xc, xl, mods, a, o_f, o_r, gb, o_gain, wa, wb)


def _ffn_kernel(x_ref, xp_ref, xn_ref, mod_ref, gain_ref, wu_ref, cw_ref, wd_ref, fg_ref, o_ref,
                h_ref, act_ref, *, tile0, n_ctx_tiles, n_tiles, final_norm):
    ti = pl.program_id(1) + tile0
    nb, t, d = x_ref.shape
    d_ff = wd_ref.shape[0]
    gain = gain_ref[...]
    shift = mod_ref[:, 3:4, :]
    scale = mod_ref[:, 4:5, :]
    x = x_ref[...]
    has_prev = jnp.logical_and(ti != 0, ti != n_ctx_tiles)
    has_next = jnp.logical_and(ti != n_ctx_tiles - 1, ti != n_tiles - 1)
    hp = jnp.where(has_prev, _norm_mod(xp_ref[...], gain, shift, scale), 0.0)
    hn = jnp.where(has_next, _norm_mod(xn_ref[...], gain, shift, scale), 0.0)
    seg = t + 2 * FFN_HALO
    h_ref[...] = jnp.concatenate([hp, _norm_mod(x, gain, shift, scale), hn], axis=1).astype(BF16).reshape(nb * seg, d)

    def body(u):
        return jnp.concatenate([u[b * seg + FFN_HALO:b * seg + FFN_HALO + t] for b in range(nb)], axis=0)

    def conv(u, cw):
        prev = body(pltpu.roll(u, 1, 0))
        nxt = body(pltpu.roll(u, nb * seg - 1, 0))
        return cw[0:1] * prev + cw[1:2] * body(u) + cw[2:3] * nxt

    for c0 in range(0, d_ff, FFN_CN):
        gate_cols = slice(c0, c0 + FFN_CN)
        val_cols = slice(d_ff + c0, d_ff + c0 + FFN_CN)
        h = h_ref[...]
        xh = conv(_dot(h, wu_ref[:, gate_cols]), 0.5 * cw_ref[:, gate_cols])
        val = conv(_dot(h, wu_ref[:, val_cols]), cw_ref[:, val_cols])
        act_ref[:, gate_cols] = (xh * (jnp.tanh(xh) + 1.0) * val).astype(BF16)
    y = x + mod_ref[:, 5:6, :] * _dot(act_ref[...], wd_ref[...]).reshape(nb, t, d)
    if final_norm:
        ms = jnp.mean(y * y, axis=-1, keepdims=True)
        y = y * lax.rsqrt(ms + NORM_EPS) * fg_ref[...]
    o_ref[...] = y


def conv_ffn(xx, mods, layer, gain, wu, cw, wd, final_gain, n_ctx_tiles, latents_only, final_norm, t=ROW_TILE):
    bsz, s, d = xx.shape
    assert s % t == 0 and t % FFN_HALO == 0
    nb = FFN_ROWS // t
    assert nb >= 1 and bsz % nb == 0
    nt = s // t
    tile0 = n_ctx_tiles if latents_only else 0
    hb = t // FFN_HALO
    last_hb = s // FFN_HALO - 1
    kernel = functools.partial(_ffn_kernel, tile0=tile0, n_ctx_tiles=n_ctx_tiles, n_tiles=nt, final_norm=final_norm)
    return pl.pallas_call(
        kernel,
        out_shape=jax.ShapeDtypeStruct((bsz, s - tile0 * t, d), F32),
        grid=(bsz // nb, nt - tile0),
        in_specs=[pl.BlockSpec((nb, t, d), lambda b, i: (b, i + tile0, 0)),
                  pl.BlockSpec((nb, FFN_HALO, d), lambda b, i: (b, jnp.maximum((i + tile0) * hb - 1, 0), 0)),
                  pl.BlockSpec((nb, FFN_HALO, d), lambda b, i: (b, jnp.minimum((i + tile0 + 1) * hb, last_hb), 0)),
                  pl.BlockSpec((nb, None, N_MOD, d),
                               lambda b, i: (b, (i + tile0 >= n_ctx_tiles).astype(jnp.int32), 0, 0)),
                  _resident_layer(gain.shape, layer), _resident_layer(wu.shape, layer),
                  _resident_layer(cw.shape, layer), _resident_layer(wd.shape, layer),
                  _resident(final_gain.shape)],
        out_specs=pl.BlockSpec((nb, t, d), lambda b, i: (b, i, 0)),
        scratch_shapes=[pltpu.VMEM((nb * (t + 2 * FFN_HALO), d), BF16),
                        pltpu.VMEM((nb * t, wd.shape[1]), BF16)],
        compiler_params=_params("parallel", "parallel"),
        name="conv_ffn_final" if final_norm else "conv_ffn",
    )(xx, xx, xx, mods, gain, wu, cw, wd, final_gain)


def _odd_mods_spec(bsz, n_ctx_blocks, d):
    return pl.BlockSpec((bsz, None, N_MOD, d), lambda j: (0, (j >= n_ctx_blocks).astype(jnp.int32), 0, 0))


def _odd_in_kernel(x_ref, mod_ref, gain_ref, wg_ref, wr_ref, gate_ref, rec_ref, slab_ref):
    bsz, tt, d = x_ref.shape
    w = wr_ref.shape[1]
    h = _norm_mod(x_ref[...], gain_ref[...], mod_ref[:, 0:1, :], mod_ref[:, 1:2, :])
    h = h.reshape(bsz * tt, d).astype(BF16)
    gate_ref[...] = jax.nn.gelu(_dot(h, wg_ref[...]), approximate=True).astype(BF16).reshape(bsz, tt, w)
    rec = _dot(h, wr_ref[...])
    for b in range(bsz):
        for sl in range(w // LANES):
            slab_ref[sl, b * TM_PITCH:b * TM_PITCH + tt, :] = rec[b * tt:(b + 1) * tt, sl * LANES:(sl + 1) * LANES]
    for t in range(tt):
        for sl in range(w // LANES):
            rec_ref[t, :, sl * LANES:(sl + 1) * LANES] = slab_ref[sl, pl.ds(t, bsz, stride=TM_PITCH), :]


def odd_in_proj(xx, mods, gain, wg, wr, n_ctx_rows):
    bsz, s, d = xx.shape
    tt = LRU_TT
    w = wg.shape[1]
    assert tt <= TM_PITCH and TM_PITCH % 8 == 0
    return pl.pallas_call(
        _odd_in_kernel,
        out_shape=[jax.ShapeDtypeStruct((bsz, s, w), BF16), jax.ShapeDtypeStruct((s, bsz, w), F32)],
        grid=(s // tt,),
        in_specs=[pl.BlockSpec((bsz, tt, d), lambda j: (0, j, 0)), _odd_mods_spec(bsz, n_ctx_rows // tt, d),
                  _resident(gain.shape), _resident(wg.shape), _resident(wr.shape)],
        out_specs=[pl.BlockSpec((bsz, tt, w), lambda j: (0, j, 0)),
                   pl.BlockSpec((tt, bsz, w), lambda j: (j, 0, 0))],
        scratch_shapes=[pltpu.VMEM((w // LANES, bsz * TM_PITCH, LANES), F32)],
        compiler_params=_params("parallel"),
        name="odd_in_proj",
    )(xx, mods, gain, wg, wr)


def _lru_kernel(uf_ref, ufp_ref, ufn_ref, ur_ref, urp_ref, urn_ref, cw_ref, wax_ref, bax_ref, lam_ref,
                hf_ref, hr_ref, ue_ref, a_ref, x_ref, hst_ref, *, n_ctx_blocks, n_blocks):
    j = pl.program_id(0)
    tt, bsz, w = uf_ref.shape

    @pl.when(j == 0)
    def _():
        hst_ref[...] = jnp.zeros_like(hst_ref)

    rev_blk = jnp.where(j < n_ctx_blocks, n_ctx_blocks - 1 - j, n_blocks - 1 - (j - n_ctx_blocks))
    for d, (blk, u_ref, up_ref, un_ref, o_ref) in enumerate(
            ((j, uf_ref, ufp_ref, ufn_ref, hf_ref), (rev_blk, ur_ref, urp_ref, urn_ref, hr_ref))):
        has_prev = jnp.logical_and(blk != 0, blk != n_ctx_blocks)
        has_next = jnp.logical_and(blk != n_ctx_blocks - 1, blk != n_blocks - 1)
        ue_ref[0:1] = jnp.where(has_prev, up_ref[...], 0.0)
        ue_ref[1:tt + 1] = u_ref[...]
        ue_ref[tt + 1:tt + 3] = jnp.where(has_next, un_ref[...], 0.0)
        cw = 0.5 * cw_ref[...]
        uc = cw[0:1] * ue_ref[0:tt] + cw[1:2] * ue_ref[1:tt + 1] + cw[2:3] * ue_ref[2:tt + 2] + cw[3:4] * ue_ref[3:tt + 3]
        uc = uc.reshape(tt * bsz, w)
        half_c = (-0.5 * LRU_C) * _softplus(-lam_ref[d])
        for hh in range(LRU_HEADS):
            cols = slice(hh * LRU_HEAD_DIM, (hh + 1) * LRU_HEAD_DIM)
            uh = uc[:, cols]
            th = jnp.tanh(_dot(uh.astype(BF16), wax_ref[d, hh]) + bax_ref[d, hh])
            log_a = th[:, 0:LRU_HEAD_DIM] * half_c[:, cols] + half_c[:, cols]
            a = jnp.exp(log_a)
            m2 = jnp.tanh(log_a) * (-1.0 - a * a)
            mult = m2 * lax.rsqrt(jnp.maximum(m2, jnp.finfo(F32).tiny))
            a_ref[:, :, cols] = a.reshape(tt, bsz, LRU_HEAD_DIM)
            x_ref[:, :, cols] = (mult * (th[:, LRU_HEAD_DIM:2 * LRU_HEAD_DIM] + 1.0) * uh).reshape(tt, bsz, LRU_HEAD_DIM)
        h = hst_ref[d]
        for step in range(tt):
            tcur = tt - 1 - step if d == 1 else step
            h = a_ref[tcur] * h + x_ref[tcur]
            o_ref[tcur] = h
        hst_ref[d] = h


def lru_scan(u, cw, wax, bax, lam, n_ctx_rows):
    s, bsz, w = u.shape
    tt = LRU_TT
    nb = s // tt
    ncb = n_ctx_rows // tt

    def rev_blk(j):
        return jnp.where(j < ncb, ncb - 1 - j, nb - 1 - (j - ncb))

    def specs(blk):
        return [pl.BlockSpec((tt, bsz, w), lambda j: (blk(j), 0, 0)),
                pl.BlockSpec((1, bsz, w), lambda j: (jnp.maximum(blk(j) * tt - 1, 0), 0, 0)),
                pl.BlockSpec((2, bsz, w), lambda j: (jnp.minimum((blk(j) + 1) * (tt // 2), s // 2 - 1), 0, 0))]

    fwd_blk = lambda j: j
    kernel = functools.partial(_lru_kernel, n_ctx_blocks=ncb, n_blocks=nb)
    return pl.pallas_call(
        kernel,
        out_shape=[jax.ShapeDtypeStruct((s, bsz, w), F32)] * 2,
        grid=(nb,),
        in_specs=specs(fwd_blk) + specs(rev_blk) + [_resident(cw.shape), _resident(wax.shape),
                                                     _resident(bax.shape), _resident(lam.shape)],
        out_specs=[pl.BlockSpec((tt, bsz, w), lambda j: (j, 0, 0)),
                   pl.BlockSpec((tt, bsz, w), lambda j: (rev_blk(j), 0, 0))],
        scratch_shapes=[pltpu.VMEM((tt + 3, bsz, w), F32), pltpu.VMEM((tt, bsz, w), F32),
                        pltpu.VMEM((tt, bsz, w), F32), pltpu.VMEM((2, bsz, w), F32)],
        compiler_params=_params("arbitrary"),
        name="lru_scan",
    )(u, u, u, u, u, u, cw, wax, bax, lam)


def _odd_out_kernel(x_ref, mod_ref, gate_ref, hf_ref, hr_ref, w_ref, o_ref, slab_ref):
    bsz, tt, d = x_ref.shape
    w = w_ref.shape[0]
    for t in range(tt):
        hs = hf_ref[t] + hr_ref[t]
        for sl in range(w // LANES):
            slab_ref[sl, pl.ds(t, bsz, stride=TM_PITCH), :] = hs[:, sl * LANES:(sl + 1) * LANES]
    rec = jnp.concatenate(
        [jnp.concatenate([slab_ref[sl, b * TM_PITCH:b * TM_PITCH + tt, :] for sl in range(w // LANES)], axis=1)
         for b in range(bsz)], axis=0)
    mixed = (gate_ref[...].reshape(bsz * tt, w) * rec).astype(BF16)
    y = _dot(mixed, w_ref[...]).reshape(bsz, tt, d)
    o_ref[...] = x_ref[...] + mod_ref[:, 2:3, :] * y


def odd_out_proj(xx, mods, gate, hf, hr, w_out, n_ctx_rows, latents_only):
    bsz, s, d = xx.shape
    tt = LRU_TT
    w = gate.shape[-1]
    j0 = n_ctx_rows // tt if latents_only else 0
    bmaj = lambda n: pl.BlockSpec((bsz, tt, n), lambda j: (0, j + j0, 0))
    tmaj = pl.BlockSpec((tt, bsz, w), lambda j: (j + j0, 0, 0))
    mods_spec = pl.BlockSpec((bsz, None, N_MOD, d),
                             lambda j: (0, (j + j0 >= n_ctx_rows // tt).astype(jnp.int32), 0, 0))
    return pl.pallas_call(
        _odd_out_kernel,
        out_shape=jax.ShapeDtypeStruct((bsz, s - j0 * tt, d), F32),
        grid=(s // tt - j0,),
        in_specs=[bmaj(d), mods_spec, bmaj(w), tmaj, tmaj, _resident(w_out.shape)],
        out_specs=pl.BlockSpec((bsz, tt, d), lambda j: (0, j, 0)),
        scratch_shapes=[pltpu.VMEM((w // LANES, bsz * TM_PITCH, LANES), F32)],
        compiler_params=_params("parallel"),
        name="odd_out_proj",
    )(xx, mods, gate, hf, hr, w_out)


def _rope_pair_slab(a, b):
    n = A_HEAD_DIM // 4
    part = lambda t, i: t[..., i * n:(i + 1) * n]
    return jnp.concatenate([part(a, 0), part(a, 2), part(b, 0), part(b, 2),
                            part(a, 1), part(a, 3), part(b, 1), part(b, 3)], axis=-1)


def _rope_tables(n_ctx, n_lat):
    rows = n_lat // GRID_W
    row = jnp.repeat(jnp.arange(rows, dtype=F32), GRID_W)
    col = jnp.tile(jnp.arange(GRID_W, dtype=F32), rows)
    n_freq = A_HEAD_DIM // 4
    inv_freq = ROPE_THETA ** (-jnp.arange(n_freq, dtype=F32) / n_freq)
    ar = row[:, None] * inv_freq
    ac = col[:, None] * inv_freq
    cos = jnp.concatenate([jnp.cos(ar), jnp.cos(ar), jnp.cos(ac), jnp.cos(ac)], axis=-1)
    sin = jnp.concatenate([-jnp.sin(ar), jnp.sin(ar), -jnp.sin(ac), jnp.sin(ac)], axis=-1)
    cos = jnp.concatenate([jnp.ones((n_ctx, A_HEAD_DIM), F32), cos], axis=0)
    sin = jnp.concatenate([jnp.zeros((n_ctx, A_HEAD_DIM), F32), sin], axis=0)
    return _rope_pair_slab(cos, cos), _rope_pair_slab(sin, sin)


def kernel(x, c, ctx, c_ctx, ada_w, ada_b, norm_mix, norm_ffn, ffn_w_up, ffn_conv, ffn_w_down, even_w_in, even_w_out, attn_q_gain, attn_k_gain, gla_gate_w_up, gla_gate_b, gla_out_gain, lru_w_in, lru_conv, lru_lambda, lru_w_a, lru_b_a, lru_w_x, lru_b_x, lru_w_out, final_gain):
    bsz, n_lat, d = x.shape
    n_ctx = ctx.shape[1]
    depth = ada_w.shape[0]
    d_ff = ffn_w_down.shape[1]
    assert n_ctx % ROW_TILE == 0 and n_lat % ROW_TILE == 0 and d_ff % FFN_CN == 0
    n_ctx_tiles = n_ctx // ROW_TILE
    s = n_ctx + n_lat

    ffn_weights = (norm_ffn.reshape(depth, 1, d), ffn_w_up.astype(BF16), ffn_conv, ffn_w_down.astype(BF16),
                   final_gain.reshape(1, d))

    xx = (ctx, x)
    cond_rows = -(-(bsz + 1) // 8) * 8
    cond = jnp.zeros((cond_rows, d), F32).at[:bsz].set(c).at[bsz].set(c_ctx)
    cos, sin = _rope_tables(n_ctx, n_lat)
    pad_gain = lambda g: _rope_pair_slab(g.reshape(1, A_HEAD_DIM), g.reshape(1, A_HEAD_DIM))

    for l in range(depth):
        last = l == depth - 1
        j = l // 2
        table = adaln_table(cond, ada_w, ada_b.reshape(depth, 1, N_MOD * d), l)
        m_lat = table[:bsz].reshape(bsz, N_MOD, d)
        m_ctx = jnp.broadcast_to(table[bsz].reshape(1, N_MOD, d), (bsz, N_MOD, d))
        mods = jnp.stack([m_ctx, m_lat], axis=1)
        gain_mix = norm_mix[l].reshape(1, d)

        if l % 2 == 0:
            w_in = even_w_in[j]
            o0 = A_HEADS * A_HEAD_DIM
            o1 = o0 + A_KV_HEADS * A_HEAD_DIM
            o2 = o1 + A_KV_HEADS * A_HEAD_DIM
            o3 = o2 + 2 * B_K + 2 * B_V
            head = lambda hh: w_in[:, hh * A_HEAD_DIM:(hh + 1) * A_HEAD_DIM]
            wqkv = jnp.concatenate(
                [_rope_pair_slab(head(m), head(m + A_HEADS // 2)) for m in range(A_HEADS // 2)]
                + [_rope_pair_slab(head(A_HEADS), head(A_HEADS + 1)), w_in[:, o1:o2]], axis=1).astype(BF16)
            wb = w_in[:, o2:o3].astype(BF16)
            wlr = w_in[:, o3:].astype(BF16)
            zeros = jnp.zeros((B_GATE_RANK, B_K), F32)
            wgate = jnp.concatenate([jnp.concatenate([gla_gate_w_up[j, 0], zeros], axis=1),
                                     jnp.concatenate([zeros, gla_gate_w_up[j, 1]], axis=1)], axis=0)
            bgate = gla_gate_b[j].reshape(1, 2 * B_K)
            s_bound = A_HEAD_DIM ** 0.5 * jnp.max(jnp.abs(attn_q_gain[j])) * jnp.max(jnp.abs(attn_k_gain[j]))
            qaug = jnp.zeros((2, LANES), F32)
            qaug = qaug.at[0, ATT_AUG_LANES[0]].set(-s_bound * LOG2E).at[1, ATT_AUG_LANES[1]].set(-s_bound * LOG2E)
            q, k, v, qb, kb, vb, gb, laf, lab = even_in_proj(
                xx, mods, gain_mix, wqkv, wb, wlr, wgate, bgate,
                pad_gain(attn_q_gain[j]), pad_gain(attn_k_gain[j]), qaug, cos, sin, n_ctx_tiles)
            a = attention(s_bound.reshape(1), q, k, v, n_ctx)
            o_f, o_r = gla_bidir(qb, kb, vb, laf, lab, n_ctx_tiles)
            w_out = even_w_out[j]
            xx = even_out_proj(xx, mods, a, o_f, o_r, gb, gla_out_gain[j].reshape(1, B_DV),
                               w_out[:o0].astype(BF16), w_out[o0:].astype(BF16), n_ctx_tiles)
        else:
            w_in = lru_w_in[j]
            gate, rec = odd_in_proj(xx, mods, gain_mix, w_in[:, :LRU_WIDTH].astype(BF16),
                                    w_in[:, LRU_WIDTH:].astype(BF16), n_ctx)
            wax = jnp.concatenate([lru_w_a[j], lru_w_x[j]], axis=-1).astype(BF16)
            bax = 0.5 * jnp.concatenate([lru_b_a[j].reshape(2, LRU_HEADS, 1, LRU_HEAD_DIM),
                                         lru_b_x[j].reshape(2, LRU_HEADS, 1, LRU_HEAD_DIM)], axis=-1)
            hf, hr = lru_scan(rec, lru_conv[j], wax, bax, lru_lambda[j].reshape(2, 1, LRU_WIDTH), n_ctx)
            xx = odd_out_proj(xx, mods, gate, hf, hr, lru_w_out[j].astype(BF16), n_ctx, latents_only=last)

        if last and xx.shape[1] == n_lat:
            t = FFN_LATENT_TILE if n_lat % FFN_LATENT_TILE == 0 else ROW_TILE
            xx = conv_ffn(xx, mods, l, *ffn_weights, 0, latents_only=False, final_norm=True, t=t)
        else:
            xx = conv_ffn(xx, mods, l, *ffn_weights, n_ctx_tiles, latents_only=last, final_norm=last)
    return xx
```

```python
import functools

import numpy as np
import jax
import jax.numpy as jnp
from jax import lax
from jax.experimental import pallas as pl
from jax.experimental.pallas import tpu as pltpu

F32 = jnp.float32
BF16 = jnp.bfloat16

NORM_EPS = 1e-6
N_MOD = 6
GRID_W = 64
ROPE_THETA = 10000.0

A_HEADS = 8
A_KV_HEADS = 2
A_GROUP = A_HEADS // A_KV_HEADS
A_HEAD_DIM = 64

B_HEADS = 4
B_DK = 64
B_DV = 128
B_K = B_HEADS * B_DK
B_V = B_HEADS * B_DV
B_GATE_RANK = 16
B_GATE_TAU = 16.0
GLA_CHUNK = 64
GLA_SUB = 16
GLA_BATCH = 2

LRU_HEADS = 10
LRU_HEAD_DIM = 128
LRU_WIDTH = LRU_HEADS * LRU_HEAD_DIM
LRU_C = 8.0

LANES = 128
ROW_TILE = 256
EVEN_BATCH = 2
ATT_TQ = 256
ATT_TK = 256
FFN_CN = 256
FFN_HALO = 8
FFN_LATENT_TILE = 512
FFN_ROWS = 512
LRU_TT = 64
TM_PITCH = 72
VMEM_LIMIT = 56 * 1024 * 1024

LOG2E = float(np.log2(np.e))
ATT_MAX_STATIC_SHIFT = 40.0
ATT_AUG_LANES = (32, 0)

NT_DIMS = (((1,), (1,)), ((), ()))
TN_DIMS = (((0,), (0,)), ((), ()))


def _params(*sem):
    return pltpu.CompilerParams(dimension_semantics=sem, vmem_limit_bytes=VMEM_LIMIT)


def _resident(shape):
    nd = len(shape)
    return pl.BlockSpec(shape, lambda *_: (0,) * nd, pipeline_mode=pl.Buffered(1))


def _resident_layer(shape, layer):
    nd = len(shape)
    return pl.BlockSpec((None,) + tuple(shape[1:]), lambda *_: (layer,) + (0,) * (nd - 1),
                        pipeline_mode=pl.Buffered(1))


def _dot(a, b):
    return jnp.dot(a, b, preferred_element_type=F32)


def _split_bf16(a):
    hi = a.astype(BF16)
    lo = (a - hi.astype(F32)).astype(BF16)
    return hi, lo


def _dot_f32(a, b):
    ah, al = _split_bf16(a)
    bh, bl = _split_bf16(b)
    return _dot(ah, bh) + _dot(ah, bl) + _dot(al, bh)


def _sigmoid(x):
    return 0.5 * jnp.tanh(0.5 * x) + 0.5


def _silu(x):
    return x * _sigmoid(x)


def _softplus(x):
    return jnp.maximum(x, 0.0) + jnp.log1p(jnp.exp(-jnp.abs(x)))


def _norm_mod(x, gain, shift, scale):
    ms = jnp.mean(x * x, axis=-1, keepdims=True)
    return (x * lax.rsqrt(ms + NORM_EPS) * gain) * (1.0 + scale) + shift


def _adaln_kernel(c_ref, w_ref, b_ref, o_ref):
    o_ref[...] = _dot_f32(_silu(c_ref[...]), w_ref[...]) + b_ref[...]


def adaln_table(cond, w, b, layer):
    rows, d = cond.shape
    n = w.shape[2]
    tn = 768
    return pl.pallas_call(
        _adaln_kernel,
        out_shape=jax.ShapeDtypeStruct((rows, n), F32),
        grid=(n // tn,),
        in_specs=[pl.BlockSpec((rows, d), lambda j: (0, 0)),
                  pl.BlockSpec((None, d, tn), lambda j: (layer, 0, j)),
                  pl.BlockSpec((None, 1, tn), lambda j: (layer, 0, j))],
        out_specs=pl.BlockSpec((rows, tn), lambda j: (0, j)),
        compiler_params=_params("arbitrary"),
        name="adaln_table",
    )(cond, w, b)


def _stream_specs(stream, n_ctx_tiles, t, nb=None):
    if isinstance(stream, tuple):
        ctx_arr, lat_arr = stream
        lat_off = 0
    else:
        ctx_arr = lat_arr = stream
        lat_off = n_ctx_tiles
    d = ctx_arr.shape[-1]
    ctx_spec = pl.BlockSpec((nb, t, d), lambda b, i: (b, jnp.minimum(i, n_ctx_tiles - 1), 0))
    lat_spec = pl.BlockSpec((nb, t, d), lambda b, i: (b, jnp.maximum(i - n_ctx_tiles, 0) + lat_off, 0))
    return (ctx_arr, lat_arr), [ctx_spec, lat_spec]


def _stream_tile(xc_ref, xl_ref, n_ctx_tiles):
    return jnp.where(pl.program_id(1) < n_ctx_tiles, xc_ref[...], xl_ref[...])


def _even_in_kernel(xc_ref, xl_ref, mod_ref, gain_ref, wqkv_ref, wb_ref, wlr_ref, wgate_ref, bgate_ref,
                    qg_ref, kg_ref, qaug_ref, cos_ref, sin_ref,
                    q_ref, k_ref, v_ref, qb_ref, kb_ref, vb_ref, gb_ref, laf_ref, lab_ref, *, n_ctx_tiles):
    nb, t, d = xl_ref.shape
    x = _stream_tile(xc_ref, xl_ref, n_ctx_tiles)
    h = _norm_mod(x, gain_ref[...], mod_ref[:, 0:1, :], mod_ref[:, 1:2, :]).reshape(nb * t, d).astype(BF16)
    cos = jnp.concatenate([cos_ref[...]] * nb, axis=0)
    sin = jnp.concatenate([sin_ref[...]] * nb, axis=0)
    lane = lax.broadcasted_iota(jnp.int32, cos.shape, 1)

    def put_head(ref, hh, val):
        for b in range(nb):
            ref[b, hh] = val[b * t:(b + 1) * t]

    def put_rows(ref, val):
        ref[...] = val.reshape(nb, t, val.shape[-1])
    first = lane % (LANES // 2) < LANES // 4
    low = lane < A_HEAD_DIM

    def pair_norm_rope(xp, g):
        sq = xp * xp
        both = jnp.sum(sq, axis=-1, keepdims=True)
        ms_first = jnp.sum(jnp.where(first, sq, 0.0), axis=-1, keepdims=True)
        inv = jnp.where(first, lax.rsqrt(ms_first * (1.0 / A_HEAD_DIM) + NORM_EPS),
                        lax.rsqrt((both - ms_first) * (1.0 / A_HEAD_DIM) + NORM_EPS))
        y = xp * inv * g
        return y * cos + pltpu.roll(y, LANES // 2, 1) * sin

    qkv = _dot(h, wqkv_ref[...])
    slab = lambda i: qkv[:, i * LANES:(i + 1) * LANES]
    n_pairs = A_HEADS // 2
    for m in range(n_pairs):
        qp = pair_norm_rope(slab(m), qg_ref[...]) * (A_HEAD_DIM ** -0.5 * LOG2E)
        put_head(q_ref, m, (jnp.where(first, qp, 0.0) + qaug_ref[0:1, :]).astype(BF16))
        put_head(q_ref, m + n_pairs, (jnp.where(first, 0.0, qp) + qaug_ref[1:2, :]).astype(BF16))
    kp = pair_norm_rope(slab(n_pairs), kg_ref[...])
    put_head(k_ref, 0, (jnp.where(first, kp, 0.0) + (lane == ATT_AUG_LANES[0]).astype(F32)).astype(BF16))
    put_head(k_ref, 1, (jnp.where(first, 0.0, kp) + (lane == ATT_AUG_LANES[1]).astype(F32)).astype(BF16))
    vp = slab(n_pairs + 1)
    put_head(v_ref, 0, jnp.where(low, vp, 1.0).astype(BF16))
    put_head(v_ref, 1, jnp.where(low, pltpu.roll(vp, A_HEAD_DIM, 1), 1.0).astype(BF16))

    pb = _dot(h, wb_ref[...])
    put_rows(qb_ref, pb[:, 0:B_K] * B_DK ** -0.5)
    put_rows(kb_ref, pb[:, B_K:2 * B_K])
    put_rows(vb_ref, pb[:, 2 * B_K:2 * B_K + B_V])
    put_rows(gb_ref, pb[:, 2 * B_K + B_V:2 * B_K + 2 * B_V].astype(BF16))

    lr = _dot(h, wlr_ref[...])
    z = _dot_f32(lr, wgate_ref[...]) + bgate_ref[...]
    la = -_softplus(-z) * (1.0 / B_GATE_TAU)
    put_rows(laf_ref, la[:, 0:B_K])
    put_rows(lab_ref, la[:, B_K:2 * B_K])


def even_in_proj(stream, mods, gain, wqkv, wb, wlr, wgate, bgate, qg, kg, qaug, cos, sin, n_ctx_tiles):
    t = ROW_TILE
    nb = EVEN_BATCH
    (xc, xl), x_specs = _stream_specs(stream, n_ctx_tiles, t, nb)
    bsz, d = xc.shape[0], xc.shape[-1]
    assert bsz % nb == 0
    s = cos.shape[0]
    row = lambda w: pl.BlockSpec((nb, t, w), lambda b, i: (b, i, 0))
    heads = lambda n: pl.BlockSpec((nb, n, t, LANES), lambda b, i: (b, 0, i, 0))
    tab = pl.BlockSpec((t, LANES), lambda b, i: (i, 0))
    mods_spec = pl.BlockSpec((nb, None, N_MOD, d), lambda b, i: (b, (i >= n_ctx_tiles).astype(jnp.int32), 0, 0))
    out_shape = [
        jax.ShapeDtypeStruct((bsz, A_HEADS, s, LANES), BF16),
        jax.ShapeDtypeStruct((bsz, A_KV_HEADS, s, LANES), BF16),
        jax.ShapeDtypeStruct((bsz, A_KV_HEADS, s, LANES), BF16),
        jax.ShapeDtypeStruct((bsz, s, B_K), F32),
        jax.ShapeDtypeStruct((bsz, s, B_K), F32),
        jax.ShapeDtypeStruct((bsz, s, B_V), F32),
        jax.ShapeDtypeStruct((bsz, s, B_V), BF16),
        jax.ShapeDtypeStruct((bsz, s, B_K), F32),
        jax.ShapeDtypeStruct((bsz, s, B_K), F32),
    ]
    return pl.pallas_call(
        functools.partial(_even_in_kernel, n_ctx_tiles=n_ctx_tiles),
        out_shape=out_shape,
        grid=(bsz // nb, s // t),
        in_specs=x_specs + [mods_spec, _resident(gain.shape),
                  _resident(wqkv.shape), _resident(wb.shape), _resident(wlr.shape),
                  _resident(wgate.shape), _resident(bgate.shape), _resident(qg.shape), _resident(kg.shape),
                  _resident(qaug.shape), tab, tab],
        out_specs=[heads(A_HEADS), heads(A_KV_HEADS), heads(A_KV_HEADS),
                   row(B_K), row(B_K), row(B_V), row(B_V), row(B_K), row(B_K)],
        compiler_params=_params("parallel", "parallel"),
        name="even_in_proj",
    )(xc, xl, mods, gain, wqkv, wb, wlr, wgate, bgate, qg, kg, qaug, cos, sin)


def _attn_kernel(shift_ref, q_ref, k_ref, v_ref, o_ref, s_ref, p_ref, m_ref, *, n_ctx_q_tiles, n_ctx_rows):
    i = pl.program_id(1)
    tq = q_ref.shape[1]
    rows = A_GROUP * tq
    n_keys = k_ref.shape[1]
    group_q = lambda g: q_ref[g * A_GROUP:(g + 1) * A_GROUP].reshape(rows, LANES)

    def finish(g, acc):
        out = acc / acc[:, A_HEAD_DIM:A_HEAD_DIM + 1]
        low = lax.broadcasted_iota(jnp.int32, (tq, LANES), 1) < A_HEAD_DIM
        for j in range(0, A_GROUP, 2):
            even = out[j * tq:(j + 1) * tq]
            odd = pltpu.roll(out[(j + 1) * tq:(j + 2) * tq], A_HEAD_DIM, 1)
            col = (g * A_GROUP + j) // 2
            o_ref[:, col * LANES:(col + 1) * LANES] = jnp.where(low, even, odd).astype(BF16)

    def attend_static_shift(nk):
        for g in range(A_KV_HEADS):
            qs = group_q(g)
            acc = jnp.zeros((rows, LANES), F32)
            for c0 in range(0, nk, ATT_TK):
                s = lax.dot_general(qs, k_ref[g, c0:c0 + ATT_TK, :], NT_DIMS, preferred_element_type=F32)
                acc = acc + _dot(jnp.exp2(s).astype(BF16), v_ref[g, c0:c0 + ATT_TK, :])
            finish(g, acc)

    def attend_row_max(nk):
        for g in range(A_KV_HEADS):
            qs = group_q(g)
            m_ref[...] = jnp.full(m_ref.shape, -jnp.inf, F32)
            for c0 in range(0, nk, ATT_TK):
                cols = slice(c0, c0 + ATT_TK)
                s = lax.dot_general(qs, k_ref[g, cols, :], NT_DIMS, preferred_element_type=F32)
                s_ref[:, cols] = s
                m_ref[...] = jnp.maximum(m_ref[...], jnp.maximum(s[:, 0:LANES], s[:, LANES:2 * LANES]))
            m = jnp.max(m_ref[...], axis=1, keepdims=True)
            for c0 in range(0, nk, ATT_TK):
                cols = slice(c0, c0 + ATT_TK)
                p_ref[:, cols] = jnp.exp2(s_ref[:, cols] - m).astype(BF16)
            finish(g, _dot(p_ref[:, 0:nk], v_ref[g, 0:nk, :]))

    is_ctx = i < n_ctx_q_tiles
    static_ok = shift_ref[0] <= ATT_MAX_STATIC_SHIFT
    for ctx_tile, nk in ((True, n_ctx_rows), (False, n_keys)):
        tile_match = is_ctx if ctx_tile else jnp.logical_not(is_ctx)

        @pl.when(jnp.logical_and(tile_match, static_ok))
        def _():
            attend_static_shift(nk)

        @pl.when(jnp.logical_and(tile_match, jnp.logical_not(static_ok)))
        def _():
            attend_row_max(nk)


def attention(shift, q, k, v, n_ctx_rows):
    bsz, _, s, _ = q.shape
    tq = ATT_TQ
    assert ATT_TK == 2 * LANES and n_ctx_rows % ATT_TK == 0 and s % ATT_TK == 0
    kernel = functools.partial(_attn_kernel, n_ctx_q_tiles=n_ctx_rows // tq, n_ctx_rows=n_ctx_rows)
    kv_spec = pl.BlockSpec((None, A_KV_HEADS, s, LANES), lambda b, i: (b, 0, 0, 0))
    return pl.pallas_call(
        kernel,
        out_shape=jax.ShapeDtypeStruct((bsz, s, A_HEADS * A_HEAD_DIM), BF16),
        grid=(bsz, s // tq),
        in_specs=[pl.BlockSpec(memory_space=pltpu.SMEM),
                  pl.BlockSpec((None, A_HEADS, tq, LANES), lambda b, i: (b, 0, i, 0)), kv_spec, kv_spec],
        out_specs=pl.BlockSpec((None, tq, A_HEADS * A_HEAD_DIM), lambda b, i: (b, i, 0)),
        scratch_shapes=[pltpu.VMEM((A_GROUP * tq, s), F32),
                        pltpu.VMEM((A_GROUP * tq, s), BF16),
                        pltpu.VMEM((A_GROUP * tq, LANES), F32)],
        compiler_params=_params("parallel", "arbitrary"),
        name="gqa_attention",
    )(shift, q, k, v)


def _gla_consts(reverse):
    c, sub = GLA_CHUNK, GLA_SUB
    t = lax.broadcasted_iota(jnp.int32, (c, c), 0)
    s = lax.broadcasted_iota(jnp.int32, (c, c), 1)
    if reverse:
        tri = s >= t
        blk = s >= (t // sub + 1) * sub
    else:
        tri = s <= t
        blk = s < (t // sub) * sub
    sums = jnp.concatenate([tri, blk], axis=0).astype(BF16)
    sums = jnp.concatenate([sums, sums], axis=1)
    lane = lax.broadcasted_iota(jnp.int32, (1, B_K), 1)
    head_masks = [(lane // B_DK == hh).astype(F32) for hh in range(B_HEADS)]
    key_row = lax.broadcasted_iota(jnp.int32, (c, 1), 0)
    return sums, tri, head_masks, key_row


def _gla_decays(la, consts, reverse):
    sums = consts[0]
    c = GLA_CHUNK
    cr = _dot(sums, jnp.concatenate(_split_bf16(la), axis=0))
    cum = cr[0:c]
    ref = cr[c:2 * c]
    edge = 0 if reverse else c - 1
    return cum, ref, cum[edge:edge + 1]


def _gla_operands(q, k, v, cum, ref, total, consts, reverse):
    _, _, head_masks, key_row = consts
    c, sub = GLA_CHUNK, GLA_SUB

    def stack_heads(x):
        return jnp.concatenate([x * hm for hm in head_masks], axis=0)

    q_in = stack_heads(q * jnp.exp(cum)).astype(BF16)
    q_loc = stack_heads(q * jnp.exp(cum - ref)).astype(BF16)
    k_out = stack_heads(k * jnp.exp(total - cum)).astype(BF16)
    k_sub = []
    for i in range(c // sub):
        ref_i = ref[i * sub:i * sub + 1]
        valid = (key_row >= i * sub) if reverse else (key_row < (i + 1) * sub)
        k_sub.append((k * jnp.exp(jnp.where(valid, ref_i - cum, -jnp.inf))).astype(BF16))
    v_st = jnp.concatenate([v[:, hh * B_DV:(hh + 1) * B_DV] for hh in range(B_HEADS)], axis=0).astype(BF16)
    return q_in, q_loc, k_out, k_sub, v_st


def _gla_scores(q_loc, k_sub):
    c, sub = GLA_CHUNK, GLA_SUB
    pieces = []
    for i in range(c // sub):
        q_i = jnp.concatenate([q_loc[hh * c + i * sub:hh * c + (i + 1) * sub] for hh in range(B_HEADS)], axis=0)
        pieces.append(lax.dot_general(q_i, k_sub[i], NT_DIMS, preferred_element_type=F32))
    return pieces


def _gla_local(pieces, k_out, v_st, consts):
    causal = consts[1]
    c, sub = GLA_CHUNK, GLA_SUB
    o_heads = []
    for hh in range(B_HEADS):
        s_h = jnp.concatenate([p[hh * sub:(hh + 1) * sub] for p in pieces], axis=0)
        s_h = jnp.where(causal, s_h, 0.0).astype(BF16)
        o_heads.append(_dot(s_h, v_st[hh * c:(hh + 1) * c]))
    o_local = jnp.concatenate(o_heads, axis=0)
    st_inc = lax.dot_general(v_st, k_out, TN_DIMS, preferred_element_type=F32)
    return o_local, st_inc


def _gla_kernel(qf_ref, kf_ref, vf_ref, laf_ref, qr_ref, kr_ref, vr_ref, lar_ref,
                of_ref, or_ref, stf_ref, str_ref):
    @pl.when(pl.program_id(1) == 0)
    def _():
        stf_ref[...] = jnp.zeros_like(stf_ref)
        str_ref[...] = jnp.zeros_like(str_ref)

    c = GLA_CHUNK
    nb = qf_ref.shape[0]
    n_chunks = qf_ref.shape[1] // c
    dirs = ((False, qf_ref, kf_ref, vf_ref, laf_ref, of_ref, stf_ref),
            (True, qr_ref, kr_ref, vr_ref, lar_ref, or_ref, str_ref))
    consts = {rev: _gla_consts(rev) for rev in (False, True)}
    units = [(b, d, n) for n in range(n_chunks) for b in range(nb) for d in range(2)]
    rows = lambda n: slice(n * c, (n + 1) * c)

    decays = {}
    for b, d, n in units:
        rev, la_ref = dirs[d][0], dirs[d][4]
        decays[b, d, n] = _gla_decays(la_ref[b, rows(n), :], consts[rev], rev)
    operands = {}
    for b, d, n in units:
        rev, q_ref, k_ref, v_ref = dirs[d][:4]
        operands[b, d, n] = _gla_operands(q_ref[b, rows(n), :], k_ref[b, rows(n), :], v_ref[b, rows(n), :],
                                          *decays[b, d, n], consts[rev], rev)
    scores = {}
    for u in units:
        scores[u] = _gla_scores(operands[u][1], operands[u][3])
    local = {}
    for u in units:
        local[u] = _gla_local(scores[u], operands[u][2], operands[u][4], consts[dirs[u[1]][0]])

    for b in range(nb):
        for d in range(2):
            rev, o_ref, st_ref = dirs[d][0], dirs[d][5], dirs[d][6]
            st = st_ref[b]
            for n in (range(n_chunks - 1, -1, -1) if rev else range(n_chunks)):
                o_local, st_inc = local[b, d, n]
                q_in = operands[b, d, n][0]
                o = o_local + lax.dot_general(q_in, st.astype(BF16), NT_DIMS, preferred_element_type=F32)
                st = st * jnp.exp(decays[b, d, n][2]) + st_inc
                for hh in range(B_HEADS):
                    o_ref[b, rows(n), hh * B_DV:(hh + 1) * B_DV] = o[hh * c:(hh + 1) * c].astype(BF16)
            st_ref[b] = st


def gla_bidir(qb, kb, vb, laf, lab, n_ctx_tiles):
    bsz, s, _ = qb.shape
    t = ROW_TILE
    nt = s // t

    def rev_tile(j):
        return jnp.where(j < n_ctx_tiles, n_ctx_tiles - 1 - j, nt - 1 - (j - n_ctx_tiles))

    nb = GLA_BATCH
    assert bsz % nb == 0
    fwd = lambda w: pl.BlockSpec((nb, t, w), lambda b, j: (b, j, 0))
    rev = lambda w: pl.BlockSpec((nb, t, w), lambda b, j: (b, rev_tile(j), 0))
    return pl.pallas_call(
        _gla_kernel,
        out_shape=[jax.ShapeDtypeStruct((bsz, s, B_V), BF16)] * 2,
        grid=(bsz // nb, nt),
        in_specs=[fwd(B_K), fwd(B_K), fwd(B_V), fwd(B_K), rev(B_K), rev(B_K), rev(B_V), rev(B_K)],
        out_specs=[fwd(B_V), rev(B_V)],
        scratch_shapes=[pltpu.VMEM((nb, B_DV, B_K), F32)] * 2,
        compiler_params=_params("parallel", "arbitrary"),
        name="gla_bidir",
    )(qb, kb, vb, laf, qb, kb, vb, lab)


def _even_out_kernel(xc_ref, xl_ref, mod_ref, a_ref, of_ref, or_ref, g_ref, og_ref, wa_ref, wb_ref, o_ref, *,
                     n_ctx_tiles):
    nb, t, d = o_ref.shape
    rows = nb * t
    o = (of_ref[...].astype(F32) + or_ref[...].astype(F32)).reshape(rows, B_V)
    g = g_ref[...].astype(F32).reshape(rows, B_V)
    parts = []
    for hh in range(B_HEADS):
        oh = o[:, hh * B_DV:(hh + 1) * B_DV]
        ms = jnp.mean(oh * oh, axis=-1, keepdims=True)
        y = oh * lax.rsqrt(ms + NORM_EPS) * og_ref[...]
        parts.append((y * _silu(g[:, hh * B_DV:(hh + 1) * B_DV])).astype(BF16))
    gla = jnp.concatenate(parts, axis=1)
    y = _dot(a_ref[...].reshape(rows, a_ref.shape[-1]), wa_ref[...]) + _dot(gla, wb_ref[...])
    o_ref[...] = _stream_tile(xc_ref, xl_ref, n_ctx_tiles) + mod_ref[:, 2:3, :] * y.reshape(nb, t, d)


def even_out_proj(stream, mods, a, o_f, o_r, gb, o_gain, wa, wb, n_ctx_tiles):
    t = ROW_TILE
    nb = EVEN_BATCH
    (xc, xl), x_specs = _stream_specs(stream, n_ctx_tiles, t, nb)
    bsz, s, d = a.shape[0], a.shape[1], xc.shape[-1]
    assert bsz % nb == 0
    row = lambda w: pl.BlockSpec((nb, t, w), lambda b, i: (b, i, 0))
    mods_spec = pl.BlockSpec((nb, None, N_MOD, d), lambda b, i: (b, (i >= n_ctx_tiles).astype(jnp.int32), 0, 0))
    return pl.pallas_call(
        functools.partial(_even_out_kernel, n_ctx_tiles=n_ctx_tiles),
        out_shape=jax.ShapeDtypeStruct((bsz, s, d), F32),
        grid=(bsz // nb, s // t),
        in_specs=x_specs + [mods_spec, row(a.shape[-1]), row(B_V), row(B_V), row(B_V),
                            _resident(o_gain.shape), _resident(wa.shape), _resident(wb.shape)],
        out_specs=row(d),
        compiler_params=_params("parallel", "parallel"),
        name="even_out_proj",
    )(xc, xl, mods, a, o_f, o_r, gb, o_gain, wa, wb)


def _ffn_kernel(x_ref, xp_ref, xn_ref, mod_ref, gain_ref, wu_ref, cw_ref, wd_ref, fg_ref, o_ref,
                h_ref, act_ref, *, tile0, n_ctx_tiles, n_tiles, final_norm):
    ti = pl.program_id(1) + tile0
    nb, t, d = x_ref.shape
    d_ff = wd_ref.shape[0]
    gain = gain_ref[...]
    shift = mod_ref[:, 3:4, :]
    scale = mod_ref[:, 4:5, :]
    x = x_ref[...]
    has_prev = jnp.logical_and(ti != 0, ti != n_ctx_tiles)
    has_next = jnp.logical_and(ti != n_ctx_tiles - 1, ti != n_tiles - 1)
    hp = jnp.where(has_prev, _norm_mod(xp_ref[...], gain, shift, scale), 0.0)
    hn = jnp.where(has_next, _norm_mod(xn_ref[...], gain, shift, scale), 0.0)
    seg = t + 2 * FFN_HALO
    h_ref[...] = jnp.concatenate([hp, _norm_mod(x, gain, shift, scale), hn], axis=1).astype(BF16).reshape(nb * seg, d)

    def body(u):
        return jnp.concatenate([u[b * seg + FFN_HALO:b * seg + FFN_HALO + t] for b in range(nb)], axis=0)

    def conv(u, cw):
        prev = body(pltpu.roll(u, 1, 0))
        nxt = body(pltpu.roll(u, nb * seg - 1, 0))
        return cw[0:1] * prev + cw[1:2] * body(u) + cw[2:3] * nxt

    for c0 in range(0, d_ff, FFN_CN):
        gate_cols = slice(c0, c0 + FFN_CN)
        val_cols = slice(d_ff + c0, d_ff + c0 + FFN_CN)
        h = h_ref[...]
        xh = conv(_dot(h, wu_ref[:, gate_cols]), 0.5 * cw_ref[:, gate_cols])
        val = conv(_dot(h, wu_ref[:, val_cols]), cw_ref[:, val_cols])
        act_ref[:, gate_cols] = (xh * (jnp.tanh(xh) + 1.0) * val).astype(BF16)
    y = x + mod_ref[:, 5:6, :] * _dot(act_ref[...], wd_ref[...]).reshape(nb, t, d)
    if final_norm:
        ms = jnp.mean(y * y, axis=-1, keepdims=True)
        y = y * lax.rsqrt(ms + NORM_EPS) * fg_ref[...]
    o_ref[...] = y


def conv_ffn(xx, mods, layer, gain, wu, cw, wd, final_gain, n_ctx_tiles, latents_only, final_norm, t=ROW_TILE):
    bsz, s, d = xx.shape
    assert s % t == 0 and t % FFN_HALO == 0
    nb = FFN_ROWS // t
    assert nb >= 1 and bsz % nb == 0
    nt = s // t
    tile0 = n_ctx_tiles if latents_only else 0
    hb = t // FFN_HALO
    last_hb = s // FFN_HALO - 1
    kernel = functools.partial(_ffn_kernel, tile0=tile0, n_ctx_tiles=n_ctx_tiles, n_tiles=nt, final_norm=final_norm)
    return pl.pallas_call(
        kernel,
        out_shape=jax.ShapeDtypeStruct((bsz, s - tile0 * t, d), F32),
        grid=(bsz // nb, nt - tile0),
        in_specs=[pl.BlockSpec((nb, t, d), lambda b, i: (b, i + tile0, 0)),
                  pl.BlockSpec((nb, FFN_HALO, d), lambda b, i: (b, jnp.maximum((i + tile0) * hb - 1, 0), 0)),
                  pl.BlockSpec((nb, FFN_HALO, d), lambda b, i: (b, jnp.minimum((i + tile0 + 1) * hb, last_hb), 0)),
                  pl.BlockSpec((nb, None, N_MOD, d),
                               lambda b, i: (b, (i + tile0 >= n_ctx_tiles).astype(jnp.int32), 0, 0)),
                  _resident_layer(gain.shape, layer), _resident_layer(wu.shape, layer),
                  _resident_layer(cw.shape, layer), _resident_layer(wd.shape, layer),
                  _resident(final_gain.shape)],
        out_specs=pl.BlockSpec((nb, t, d), lambda b, i: (b, i, 0)),
        scratch_shapes=[pltpu.VMEM((nb * (t + 2 * FFN_HALO), d), BF16),
                        pltpu.VMEM((nb * t, wd.shape[1]), BF16)],
        compiler_params=_params("parallel", "parallel"),
        name="conv_ffn_final" if final_norm else "conv_ffn",
    )(xx, xx, xx, mods, gain, wu, cw, wd, final_gain)


def _odd_mods_spec(bsz, n_ctx_blocks, d):
    return pl.BlockSpec((bsz, None, N_MOD, d), lambda j: (0, (j >= n_ctx_blocks).astype(jnp.int32), 0, 0))


def _odd_in_kernel(x_ref, mod_ref, gain_ref, wg_ref, wr_ref, gate_ref, rec_ref, slab_ref):
    bsz, tt, d = x_ref.shape
    w = wr_ref.shape[1]
    h = _norm_mod(x_ref[...], gain_ref[...], mod_ref[:, 0:1, :], mod_ref[:, 1:2, :])
    h = h.reshape(bsz * tt, d).astype(BF16)
    gate_ref[...] = jax.nn.gelu(_dot(h, wg_ref[...]), approximate=True).astype(BF16).reshape(bsz, tt, w)
    rec = _dot(h, wr_ref[...])
    for b in range(bsz):
        for sl in range(w // LANES):
            slab_ref[sl, b * TM_PITCH:b * TM_PITCH + tt, :] = rec[b * tt:(b + 1) * tt, sl * LANES:(sl + 1) * LANES]
    for t in range(tt):
        for sl in range(w // LANES):
            rec_ref[t, :, sl * LANES:(sl + 1) * LANES] = slab_ref[sl, pl.ds(t, bsz, stride=TM_PITCH), :]


def odd_in_proj(xx, mods, gain, wg, wr, n_ctx_rows):
    bsz, s, d = xx.shape
    tt = LRU_TT
    w = wg.shape[1]
    assert tt <= TM_PITCH and TM_PITCH % 8 == 0
    return pl.pallas_call(
        _odd_in_kernel,
        out_shape=[jax.ShapeDtypeStruct((bsz, s, w), BF16), jax.ShapeDtypeStruct((s, bsz, w), F32)],
        grid=(s // tt,),
        in_specs=[pl.BlockSpec((bsz, tt, d), lambda j: (0, j, 0)), _odd_mods_spec(bsz, n_ctx_rows // tt, d),
                  _resident(gain.shape), _resident(wg.shape), _resident(wr.shape)],
        out_specs=[pl.BlockSpec((bsz, tt, w), lambda j: (0, j, 0)),
                   pl.BlockSpec((tt, bsz, w), lambda j: (j, 0, 0))],
        scratch_shapes=[pltpu.VMEM((w // LANES, bsz * TM_PITCH, LANES), F32)],
        compiler_params=_params("parallel"),
        name="odd_in_proj",
    )(xx, mods, gain, wg, wr)


def _lru_kernel(uf_ref, ufp_ref, ufn_ref, ur_ref, urp_ref, urn_ref, cw_ref, wax_ref, bax_ref, lam_ref,
                hf_ref, hr_ref, ue_ref, a_ref, x_ref, hst_ref, *, n_ctx_blocks, n_blocks):
    j = pl.program_id(0)
    tt, bsz, w = uf_ref.shape

    @pl.when(j == 0)
    def _():
        hst_ref[...] = jnp.zeros_like(hst_ref)

    rev_blk = jnp.where(j < n_ctx_blocks, n_ctx_blocks - 1 - j, n_blocks - 1 - (j - n_ctx_blocks))
    for d, (blk, u_ref, up_ref, un_ref, o_ref) in enumerate(
            ((j, uf_ref, ufp_ref, ufn_ref, hf_ref), (rev_blk, ur_ref, urp_ref, urn_ref, hr_ref))):
        has_prev = jnp.logical_and(blk != 0, blk != n_ctx_blocks)
        has_next = jnp.logical_and(blk != n_ctx_blocks - 1, blk != n_blocks - 1)
        ue_ref[0:1] = jnp.where(has_prev, up_ref[...], 0.0)
        ue_ref[1:tt + 1] = u_ref[...]
        ue_ref[tt + 1:tt + 3] = jnp.where(has_next, un_ref[...], 0.0)
        cw = 0.5 * cw_ref[...]
        uc = cw[0:1] * ue_ref[0:tt] + cw[1:2] * ue_ref[1:tt + 1] + cw[2:3] * ue_ref[2:tt + 2] + cw[3:4] * ue_ref[3:tt + 3]
        uc = uc.reshape(tt * bsz, w)
        half_c = (-0.5 * LRU_C) * _softplus(-lam_ref[d])
        for hh in range(LRU_HEADS):
            cols = slice(hh * LRU_HEAD_DIM, (hh + 1) * LRU_HEAD_DIM)
            uh = uc[:, cols]
            th = jnp.tanh(_dot(uh.astype(BF16), wax_ref[d, hh]) + bax_ref[d, hh])
            log_a = th[:, 0:LRU_HEAD_DIM] * half_c[:, cols] + half_c[:, cols]
            a = jnp.exp(log_a)
            m2 = jnp.tanh(log_a) * (-1.0 - a * a)
            mult = m2 * lax.rsqrt(jnp.maximum(m2, jnp.finfo(F32).tiny))
            a_ref[:, :, cols] = a.reshape(tt, bsz, LRU_HEAD_DIM)
            x_ref[:, :, cols] = (mult * (th[:, LRU_HEAD_DIM:2 * LRU_HEAD_DIM] + 1.0) * uh).reshape(tt, bsz, LRU_HEAD_DIM)
        h = hst_ref[d]
        for step in range(tt):
            tcur = tt - 1 - step if d == 1 else step
            h = a_ref[tcur] * h + x_ref[tcur]
            o_ref[tcur] = h
        hst_ref[d] = h


def lru_scan(u, cw, wax, bax, lam, n_ctx_rows):
    s, bsz, w = u.shape
    tt = LRU_TT
    nb = s // tt
    ncb = n_ctx_rows // tt

    def rev_blk(j):
        return jnp.where(j < ncb, ncb - 1 - j, nb - 1 - (j - ncb))

    def specs(blk):
        return [pl.BlockSpec((tt, bsz, w), lambda j: (blk(j), 0, 0)),
                pl.BlockSpec((1, bsz, w), lambda j: (jnp.maximum(blk(j) * tt - 1, 0), 0, 0)),
                pl.BlockSpec((2, bsz, w), lambda j: (jnp.minimum((blk(j) + 1) * (tt // 2), s // 2 - 1), 0, 0))]

    fwd_blk = lambda j: j
    kernel = functools.partial(_lru_kernel, n_ctx_blocks=ncb, n_blocks=nb)
    return pl.pallas_call(
        kernel,
        out_shape=[jax.ShapeDtypeStruct((s, bsz, w), F32)] * 2,
        grid=(nb,),
        in_specs=specs(fwd_blk) + specs(rev_blk) + [_resident(cw.shape), _resident(wax.shape),
                                                     _resident(bax.shape), _resident(lam.shape)],
        out_specs=[pl.BlockSpec((tt, bsz, w), lambda j: (j, 0, 0)),
                   pl.BlockSpec((tt, bsz, w), lambda j: (rev_blk(j), 0, 0))],
        scratch_shapes=[pltpu.VMEM((tt + 3, bsz, w), F32), pltpu.VMEM((tt, bsz, w), F32),
                        pltpu.VMEM((tt, bsz, w), F32), pltpu.VMEM((2, bsz, w), F32)],
        compiler_params=_params("arbitrary"),
        name="lru_scan",
    )(u, u, u, u, u, u, cw, wax, bax, lam)


def _odd_out_kernel(x_ref, mod_ref, gate_ref, hf_ref, hr_ref, w_ref, o_ref, slab_ref):
    bsz, tt, d = x_ref.shape
    w = w_ref.shape[0]
    for t in range(tt):
        hs = hf_ref[t] + hr_ref[t]
        for sl in range(w // LANES):
            slab_ref[sl, pl.ds(t, bsz, stride=TM_PITCH), :] = hs[:, sl * LANES:(sl + 1) * LANES]
    rec = jnp.concatenate(
        [jnp.concatenate([slab_ref[sl, b * TM_PITCH:b * TM_PITCH + tt, :] for sl in range(w // LANES)], axis=1)
         for b in range(bsz)], axis=0)
    mixed = (gate_ref[...].reshape(bsz * tt, w) * rec).astype(BF16)
    y = _dot(mixed, w_ref[...]).reshape(bsz, tt, d)
    o_ref[...] = x_ref[...] + mod_ref[:, 2:3, :] * y


def odd_out_proj(xx, mods, gate, hf, hr, w_out, n_ctx_rows, latents_only):
    bsz, s, d = xx.shape
    tt = LRU_TT
    w = gate.shape[-1]
    j0 = n_ctx_rows // tt if latents_only else 0
    bmaj = lambda n: pl.BlockSpec((bsz, tt, n), lambda j: (0, j + j0, 0))
    tmaj = pl.BlockSpec((tt, bsz, w), lambda j: (j + j0, 0, 0))
    mods_spec = pl.BlockSpec((bsz, None, N_MOD, d),
                             lambda j: (0, (j + j0 >= n_ctx_rows // tt).astype(jnp.int32), 0, 0))
    return pl.pallas_call(
        _odd_out_kernel,
        out_shape=jax.ShapeDtypeStruct((bsz, s - j0 * tt, d), F32),
        grid=(s // tt - j0,),
        in_specs=[bmaj(d), mods_spec, bmaj(w), tmaj, tmaj, _resident(w_out.shape)],
        out_specs=pl.BlockSpec((bsz, tt, d), lambda j: (0, j, 0)),
        scratch_shapes=[pltpu.VMEM((w // LANES, bsz * TM_PITCH, LANES), F32)],
        compiler_params=_params("parallel"),
        name="odd_out_proj",
    )(xx, mods, gate, hf, hr, w_out)


def _rope_pair_slab(a, b):
    n = A_HEAD_DIM // 4
    part = lambda t, i: t[..., i * n:(i + 1) * n]
    return jnp.concatenate([part(a, 0), part(a, 2), part(b, 0), part(b, 2),
                            part(a, 1), part(a, 3), part(b, 1), part(b, 3)], axis=-1)


def _rope_tables(n_ctx, n_lat):
    rows = n_lat // GRID_W
    row = jnp.repeat(jnp.arange(rows, dtype=F32), GRID_W)
    col = jnp.tile(jnp.arange(GRID_W, dtype=F32), rows)
    n_freq = A_HEAD_DIM // 4
    inv_freq = ROPE_THETA ** (-jnp.arange(n_freq, dtype=F32) / n_freq)
    ar = row[:, None] * inv_freq
    ac = col[:, None] * inv_freq
    cos = jnp.concatenate([jnp.cos(ar), jnp.cos(ar), jnp.cos(ac), jnp.cos(ac)], axis=-1)
    sin = jnp.concatenate([-jnp.sin(ar), jnp.sin(ar), -jnp.sin(ac), jnp.sin(ac)], axis=-1)
    cos = jnp.concatenate([jnp.ones((n_ctx, A_HEAD_DIM), F32), cos], axis=0)
    sin = jnp.concatenate([jnp.zeros((n_ctx, A_HEAD_DIM), F32), sin], axis=0)
    return _rope_pair_slab(cos, cos), _rope_pair_slab(sin, sin)


def kernel(x, c, ctx, c_ctx, ada_w, ada_b, norm_mix, norm_ffn, ffn_w_up, ffn_conv, ffn_w_down, even_w_in, even_w_out, attn_q_gain, attn_k_gain, gla_gate_w_up, gla_gate_b, gla_out_gain, lru_w_in, lru_conv, lru_lambda, lru_w_a, lru_b_a, lru_w_x, lru_b_x, lru_w_out, final_gain):
    bsz, n_lat, d = x.shape
    n_ctx = ctx.shape[1]
    depth = ada_w.shape[0]
    d_ff = ffn_w_down.shape[1]
    assert n_ctx % ROW_TILE == 0 and n_lat % ROW_TILE == 0 and d_ff % FFN_CN == 0
    n_ctx_tiles = n_ctx // ROW_TILE
    s = n_ctx + n_lat

    ffn_weights = (norm_ffn.reshape(depth, 1, d), ffn_w_up.astype(BF16), ffn_conv, ffn_w_down.astype(BF16),
                   final_gain.reshape(1, d))

    xx = (ctx, x)
    cond_rows = -(-(bsz + 1) // 8) * 8
    cond = jnp.zeros((cond_rows, d), F32).at[:bsz].set(c).at[bsz].set(c_ctx)
    cos, sin = _rope_tables(n_ctx, n_lat)
    pad_gain = lambda g: _rope_pair_slab(g.reshape(1, A_HEAD_DIM), g.reshape(1, A_HEAD_DIM))

    for l in range(depth):
        last = l == depth - 1
        j = l // 2
        table = adaln_table(cond, ada_w, ada_b.reshape(depth, 1, N_MOD * d), l)
        m_lat = table[:bsz].reshape(bsz, N_MOD, d)
        m_ctx = jnp.broadcast_to(table[bsz].reshape(1, N_MOD, d), (bsz, N_MOD, d))
        mods = jnp.stack([m_ctx, m_lat], axis=1)
        gain_mix = norm_mix[l].reshape(1, d)

        if l % 2 == 0:
            w_in = even_w_in[j]
            o0 = A_HEADS * A_HEAD_DIM
            o1 = o0 + A_KV_HEADS * A_HEAD_DIM
            o2 = o1 + A_KV_HEADS * A_HEAD_DIM
            o3 = o2 + 2 * B_K + 2 * B_V
            head = lambda hh: w_in[:, hh * A_HEAD_DIM:(hh + 1) * A_HEAD_DIM]
            wqkv = jnp.concatenate(
                [_rope_pair_slab(head(m), head(m + A_HEADS // 2)) for m in range(A_HEADS // 2)]
                + [_rope_pair_slab(head(A_HEADS), head(A_HEADS + 1)), w_in[:, o1:o2]], axis=1).astype(BF16)
            wb = w_in[:, o2:o3].astype(BF16)
            wlr = w_in[:, o3:].astype(BF16)
            zeros = jnp.zeros((B_GATE_RANK, B_K), F32)
            wgate = jnp.concatenate([jnp.concatenate([gla_gate_w_up[j, 0], zeros], axis=1),
                                     jnp.concatenate([zeros, gla_gate_w_up[j, 1]], axis=1)], axis=0)
            bgate = gla_gate_b[j].reshape(1, 2 * B_K)
            s_bound = A_HEAD_DIM ** 0.5 * jnp.max(jnp.abs(attn_q_gain[j])) * jnp.max(jnp.abs(attn_k_gain[j]))
            qaug = jnp.zeros((2, LANES), F32)
            qaug = qaug.at[0, ATT_AUG_LANES[0]].set(-s_bound * LOG2E).at[1, ATT_AUG_LANES[1]].set(-s_bound * LOG2E)
            q, k, v, qb, kb, vb, gb, laf, lab = even_in_proj(
                xx, mods, gain_mix, wqkv, wb, wlr, wgate, bgate,
                pad_gain(attn_q_gain[j]), pad_gain(attn_k_gain[j]), qaug, cos, sin, n_ctx_tiles)
            a = attention(s_bound.reshape(1), q, k, v, n_ctx)
            o_f, o_r = gla_bidir(qb, kb, vb, laf, lab, n_ctx_tiles)
            w_out = even_w_out[j]
            xx = even_out_proj(xx, mods, a, o_f, o_r, gb, gla_out_gain[j].reshape(1, B_DV),
                               w_out[:o0].astype(BF16), w_out[o0:].astype(BF16), n_ctx_tiles)
        else:
            w_in = lru_w_in[j]
            gate, rec = odd_in_proj(xx, mods, gain_mix, w_in[:, :LRU_WIDTH].astype(BF16),
                                    w_in[:, LRU_WIDTH:].astype(BF16), n_ctx)
            wax = jnp.concatenate([lru_w_a[j], lru_w_x[j]], axis=-1).astype(BF16)
            bax = 0.5 * jnp.concatenate([lru_b_a[j].reshape(2, LRU_HEADS, 1, LRU_HEAD_DIM),
                                         lru_b_x[j].reshape(2, LRU_HEADS, 1, LRU_HEAD_DIM)], axis=-1)
            hf, hr = lru_scan(rec, lru_conv[j], wax, bax, lru_lambda[j].reshape(2, 1, LRU_WIDTH), n_ctx)
            xx = odd_out_proj(xx, mods, gate, hf, hr, lru_w_out[j].astype(BF16), n_ctx, latents_only=last)

        if last and xx.shape[1] == n_lat:
            t = FFN_LATENT_TILE if n_lat % FFN_LATENT_TILE == 0 else ROW_TILE
            xx = conv_ffn(xx, mods, l, *ffn_weights, 0, latents_only=False, final_norm=True, t=t)
        else:
            xx = conv_ffn(xx, mods, l, *ffn_weights, n_ctx_tiles, latents_only=last, final_norm=last)
    return xx
```

```python
import functools

import numpy as np
import jax
import jax.numpy as jnp
from jax import lax
from jax.experimental import pallas as pl
from jax.experimental.pallas import tpu as pltpu

F32 = jnp.float32
BF16 = jnp.bfloat16

NORM_EPS = 1e-6
N_MOD = 6
GRID_W = 64
ROPE_THETA = 10000.0

A_HEADS = 8
A_KV_HEADS = 2
A_GROUP = A_HEADS // A_KV_HEADS
A_HEAD_DIM = 64

B_HEADS = 4
B_DK = 64
B_DV = 128
B_K = B_HEADS * B_DK
B_V = B_HEADS * B_DV
B_GATE_RANK = 16
B_GATE_TAU = 16.0
GLA_CHUNK = 64
GLA_SUB = 16
GLA_BATCH = 4

LRU_HEADS = 10
LRU_HEAD_DIM = 128
LRU_WIDTH = LRU_HEADS * LRU_HEAD_DIM
LRU_C = 8.0

LANES = 128
ROW_TILE = 256
EVEN_BATCH = 4
ATT_TQ = 256
ATT_TK = 256
FFN_CN = 256
FFN_HALO = 8
FFN_LATENT_TILE = 512
FFN_ROWS = 512
LRU_TT = 64
TM_PITCH = 72
VMEM_LIMIT = 56 * 1024 * 1024

LOG2E = float(np.log2(np.e))
ATT_MAX_STATIC_SHIFT = 40.0
ATT_AUG_LANES = (32, 0)

NT_DIMS = (((1,), (1,)), ((), ()))
TN_DIMS = (((0,), (0,)), ((), ()))


def _params(*sem):
    return pltpu.CompilerParams(dimension_semantics=sem, vmem_limit_bytes=VMEM_LIMIT)


def _batch_block(bsz, want):
    return max(n for n in range(1, want + 1) if bsz % n == 0)


def _resident(shape):
    nd = len(shape)
    return pl.BlockSpec(shape, lambda *_: (0,) * nd, pipeline_mode=pl.Buffered(1))


def _resident_layer(shape, layer):
    nd = len(shape)
    return pl.BlockSpec((None,) + tuple(shape[1:]), lambda *_: (layer,) + (0,) * (nd - 1),
                        pipeline_mode=pl.Buffered(1))


def _dot(a, b):
    return jnp.dot(a, b, preferred_element_type=F32)


def _split_bf16(a):
    hi = a.astype(BF16)
    lo = (a - hi.astype(F32)).astype(BF16)
    return hi, lo


def _dot_f32(a, b):
    ah, al = _split_bf16(a)
    bh, bl = _split_bf16(b)
    return _dot(ah, bh) + _dot(ah, bl) + _dot(al, bh)


def _sigmoid(x):
    return 0.5 * jnp.tanh(0.5 * x) + 0.5


def _silu(x):
    return x * _sigmoid(x)


def _softplus(x):
    return jnp.maximum(x, 0.0) + jnp.log1p(jnp.exp(-jnp.abs(x)))


def _norm_mod(x, gain, shift, scale):
    ms = jnp.mean(x * x, axis=-1, keepdims=True)
    return (x * lax.rsqrt(ms + NORM_EPS) * gain) * (1.0 + scale) + shift


def _adaln_kernel(c_ref, w_ref, b_ref, o_ref):
    o_ref[...] = _dot_f32(_silu(c_ref[...]), w_ref[...]) + b_ref[...]


def adaln_table(cond, w, b, layer):
    rows, d = cond.shape
    n = w.shape[2]
    tn = 768
    return pl.pallas_call(
        _adaln_kernel,
        out_shape=jax.ShapeDtypeStruct((rows, n), F32),
        grid=(n // tn,),
        in_specs=[pl.BlockSpec((rows, d), lambda j: (0, 0)),
                  pl.BlockSpec((None, d, tn), lambda j: (layer, 0, j)),
                  pl.BlockSpec((None, 1, tn), lambda j: (layer, 0, j))],
        out_specs=pl.BlockSpec((rows, tn), lambda j: (0, j)),
        compiler_params=_params("arbitrary"),
        name="adaln_table",
    )(cond, w, b)


def _stream_specs(stream, n_ctx_tiles, t, nb=None):
    if isinstance(stream, tuple):
        ctx_arr, lat_arr = stream
        lat_off = 0
    else:
        ctx_arr = lat_arr = stream
        lat_off = n_ctx_tiles
    d = ctx_arr.shape[-1]
    ctx_spec = pl.BlockSpec((nb, t, d), lambda b, i: (b, jnp.minimum(i, n_ctx_tiles - 1), 0))
    lat_spec = pl.BlockSpec((nb, t, d), lambda b, i: (b, jnp.maximum(i - n_ctx_tiles, 0) + lat_off, 0))
    return (ctx_arr, lat_arr), [ctx_spec, lat_spec]


def _stream_tile(xc_ref, xl_ref, n_ctx_tiles):
    return jnp.where(pl.program_id(1) < n_ctx_tiles, xc_ref[...], xl_ref[...])


def _even_in_kernel(xc_ref, xl_ref, mod_ref, gain_ref, wqkv_ref, wb_ref, wlr_ref, wgate_ref, bgate_ref,
                    qg_ref, kg_ref, qaug_ref, cos_ref, sin_ref,
                    q_ref, k_ref, v_ref, qb_ref, kb_ref, vb_ref, gb_ref, laf_ref, lab_ref, *, n_ctx_tiles):
    nb, t, d = xl_ref.shape
    x = _stream_tile(xc_ref, xl_ref, n_ctx_tiles)
    h = _norm_mod(x, gain_ref[...], mod_ref[:, 0:1, :], mod_ref[:, 1:2, :]).reshape(nb * t, d).astype(BF16)
    cos = jnp.concatenate([cos_ref[...]] * nb, axis=0)
    sin = jnp.concatenate([sin_ref[...]] * nb, axis=0)
    lane = lax.broadcasted_iota(jnp.int32, cos.shape, 1)

    def put_head(ref, hh, val):
        for b in range(nb):
            ref[b, hh] = val[b * t:(b + 1) * t]

    def put_rows(ref, val):
        ref[...] = val.reshape(nb, t, val.shape[-1])
    first = lane % (LANES // 2) < LANES // 4
    low = lane < A_HEAD_DIM

    def pair_norm_rope(xp, g):
        sq = xp * xp
        both = jnp.sum(sq, axis=-1, keepdims=True)
        ms_first = jnp.sum(jnp.where(first, sq, 0.0), axis=-1, keepdims=True)
        inv = jnp.where(first, lax.rsqrt(ms_first * (1.0 / A_HEAD_DIM) + NORM_EPS),
                        lax.rsqrt((both - ms_first) * (1.0 / A_HEAD_DIM) + NORM_EPS))
        y = xp * inv * g
        return y * cos + pltpu.roll(y, LANES // 2, 1) * sin

    qkv = _dot(h, wqkv_ref[...])
    slab = lambda i: qkv[:, i * LANES:(i + 1) * LANES]
    n_pairs = A_HEADS // 2
    for m in range(n_pairs):
        qp = pair_norm_rope(slab(m), qg_ref[...]) * (A_HEAD_DIM ** -0.5 * LOG2E)
        put_head(q_ref, m, (jnp.where(first, qp, 0.0) + qaug_ref[0:1, :]).astype(BF16))
        put_head(q_ref, m + n_pairs, (jnp.where(first, 0.0, qp) + qaug_ref[1:2, :]).astype(BF16))
    kp = pair_norm_rope(slab(n_pairs), kg_ref[...])
    put_head(k_ref, 0, (jnp.where(first, kp, 0.0) + (lane == ATT_AUG_LANES[0]).astype(F32)).astype(BF16))
    put_head(k_ref, 1, (jnp.where(first, 0.0, kp) + (lane == ATT_AUG_LANES[1]).astype(F32)).astype(BF16))
    vp = slab(n_pairs + 1)
    put_head(v_ref, 0, jnp.where(low, vp, 1.0).astype(BF16))
    put_head(v_ref, 1, jnp.where(low, pltpu.roll(vp, A_HEAD_DIM, 1), 1.0).astype(BF16))

    pb = _dot(h, wb_ref[...])
    put_rows(qb_ref, pb[:, 0:B_K] * B_DK ** -0.5)
    put_rows(kb_ref, pb[:, B_K:2 * B_K])
    put_rows(vb_ref, pb[:, 2 * B_K:2 * B_K + B_V])
    put_rows(gb_ref, pb[:, 2 * B_K + B_V:2 * B_K + 2 * B_V].astype(BF16))

    lr = _dot(h, wlr_ref[...])
    z = _dot_f32(lr, wgate_ref[...]) + bgate_ref[...]
    la = -_softplus(-z) * (1.0 / B_GATE_TAU)
    put_rows(laf_ref, la[:, 0:B_K])
    put_rows(lab_ref, la[:, B_K:2 * B_K])


def even_in_proj(stream, mods, gain, wqkv, wb, wlr, wgate, bgate, qg, kg, qaug, cos, sin, n_ctx_tiles):
    t = ROW_TILE
    nb = _batch_block(stream[0].shape[0] if isinstance(stream, tuple) else stream.shape[0], EVEN_BATCH)
    (xc, xl), x_specs = _stream_specs(stream, n_ctx_tiles, t, nb)
    bsz, d = xc.shape[0], xc.shape[-1]
    s = cos.shape[0]
    row = lambda w: pl.BlockSpec((nb, t, w), lambda b, i: (b, i, 0))
    heads = lambda n: pl.BlockSpec((nb, n, t, LANES), lambda b, i: (b, 0, i, 0))
    tab = pl.BlockSpec((t, LANES), lambda b, i: (i, 0))
    mods_spec = pl.BlockSpec((nb, None, N_MOD, d), lambda b, i: (b, (i >= n_ctx_tiles).astype(jnp.int32), 0, 0))
    out_shape = [
        jax.ShapeDtypeStruct((bsz, A_HEADS, s, LANES), BF16),
        jax.ShapeDtypeStruct((bsz, A_KV_HEADS, s, LANES), BF16),
        jax.ShapeDtypeStruct((bsz, A_KV_HEADS, s, LANES), BF16),
        jax.ShapeDtypeStruct((bsz, s, B_K), F32),
        jax.ShapeDtypeStruct((bsz, s, B_K), F32),
        jax.ShapeDtypeStruct((bsz, s, B_V), F32),
        jax.ShapeDtypeStruct((bsz, s, B_V), BF16),
        jax.ShapeDtypeStruct((bsz, s, B_K), F32),
        jax.ShapeDtypeStruct((bsz, s, B_K), F32),
    ]
    return pl.pallas_call(
        functools.partial(_even_in_kernel, n_ctx_tiles=n_ctx_tiles),
        out_shape=out_shape,
        grid=(bsz // nb, s // t),
        in_specs=x_specs + [mods_spec, _resident(gain.shape),
                  _resident(wqkv.shape), _resident(wb.shape), _resident(wlr.shape),
                  _resident(wgate.shape), _resident(bgate.shape), _resident(qg.shape), _resident(kg.shape),
                  _resident(qaug.shape), tab, tab],
        out_specs=[heads(A_HEADS), heads(A_KV_HEADS), heads(A_KV_HEADS),
                   row(B_K), row(B_K), row(B_V), row(B_V), row(B_K), row(B_K)],
        compiler_params=_params("parallel", "parallel"),
        name="even_in_proj",
    )(xc, xl, mods, gain, wqkv, wb, wlr, wgate, bgate, qg, kg, qaug, cos, sin)


def _attn_kernel(shift_ref, q_ref, k_ref, v_ref, o_ref, s_ref, p_ref, m_ref, *, n_ctx_q_tiles, n_ctx_rows):
    i = pl.program_id(1)
    tq = q_ref.shape[1]
    rows = A_GROUP * tq
    n_keys = k_ref.shape[1]
    group_q = lambda g: q_ref[g * A_GROUP:(g + 1) * A_GROUP].reshape(rows, LANES)

    def finish(g, acc):
        out = acc / acc[:, A_HEAD_DIM:A_HEAD_DIM + 1]
        low = lax.broadcasted_iota(jnp.int32, (tq, LANES), 1) < A_HEAD_DIM
        for j in range(0, A_GROUP, 2):
            even = out[j * tq:(j + 1) * tq]
            odd = pltpu.roll(out[(j + 1) * tq:(j + 2) * tq], A_HEAD_DIM, 1)
            col = (g * A_GROUP + j) // 2
            o_ref[:, col * LANES:(col + 1) * LANES] = jnp.where(low, even, odd).astype(BF16)

    def attend_static_shift(nk):
        for g in range(A_KV_HEADS):
            qs = group_q(g)
            acc = jnp.zeros((rows, LANES), F32)
            for c0 in range(0, nk, ATT_TK):
                s = lax.dot_general(qs, k_ref[g, c0:c0 + ATT_TK, :], NT_DIMS, preferred_element_type=F32)
                acc = acc + _dot(jnp.exp2(s).astype(BF16), v_ref[g, c0:c0 + ATT_TK, :])
            finish(g, acc)

    def attend_row_max(nk):
        for g in range(A_KV_HEADS):
            qs = group_q(g)
            m_ref[...] = jnp.full(m_ref.shape, -jnp.inf, F32)
            for c0 in range(0, nk, ATT_TK):
                cols = slice(c0, c0 + ATT_TK)
                s = lax.dot_general(qs, k_ref[g, cols, :], NT_DIMS, preferred_element_type=F32)
                s_ref[:, cols] = s
                m_ref[...] = jnp.maximum(m_ref[...], jnp.maximum(s[:, 0:LANES], s[:, LANES:2 * LANES]))
            m = jnp.max(m_ref[...], axis=1, keepdims=True)
            for c0 in range(0, nk, ATT_TK):
                cols = slice(c0, c0 + ATT_TK)
                p_ref[:, cols] = jnp.exp2(s_ref[:, cols] - m).astype(BF16)
            finish(g, _dot(p_ref[:, 0:nk], v_ref[g, 0:nk, :]))

    is_ctx = i < n_ctx_q_tiles
    static_ok = shift_ref[0] <= ATT_MAX_STATIC_SHIFT
    for ctx_tile, nk in ((True, n_ctx_rows), (False, n_keys)):
        tile_match = is_ctx if ctx_tile else jnp.logical_not(is_ctx)

        @pl.when(jnp.logical_and(tile_match, static_ok))
        def _():
            attend_static_shift(nk)

        @pl.when(jnp.logical_and(tile_match, jnp.logical_not(static_ok)))
        def _():
            attend_row_max(nk)


def attention(shift, q, k, v, n_ctx_rows):
    bsz, _, s, _ = q.shape
    tq = ATT_TQ
    assert ATT_TK == 2 * LANES and n_ctx_rows % ATT_TK == 0 and s % ATT_TK == 0
    kernel = functools.partial(_attn_kernel, n_ctx_q_tiles=n_ctx_rows // tq, n_ctx_rows=n_ctx_rows)
    kv_spec = pl.BlockSpec((None, A_KV_HEADS, s, LANES), lambda b, i: (b, 0, 0, 0))
    return pl.pallas_call(
        kernel,
        out_shape=jax.ShapeDtypeStruct((bsz, s, A_HEADS * A_HEAD_DIM), BF16),
        grid=(bsz, s // tq),
        in_specs=[pl.BlockSpec(memory_space=pltpu.SMEM),
                  pl.BlockSpec((None, A_HEADS, tq, LANES), lambda b, i: (b, 0, i, 0)), kv_spec, kv_spec],
        out_specs=pl.BlockSpec((None, tq, A_HEADS * A_HEAD_DIM), lambda b, i: (b, i, 0)),
        scratch_shapes=[pltpu.VMEM((A_GROUP * tq, s), F32),
                        pltpu.VMEM((A_GROUP * tq, s), BF16),
                        pltpu.VMEM((A_GROUP * tq, LANES), F32)],
        compiler_params=_params("parallel", "arbitrary"),
        name="gqa_attention",
    )(shift, q, k, v)


def _gla_consts(reverse):
    c, sub = GLA_CHUNK, GLA_SUB
    t = lax.broadcasted_iota(jnp.int32, (c, c), 0)
    s = lax.broadcasted_iota(jnp.int32, (c, c), 1)
    if reverse:
        tri = s >= t
        blk = s >= (t // sub + 1) * sub
    else:
        tri = s <= t
        blk = s < (t // sub) * sub
    sums = jnp.concatenate([tri, blk], axis=0).astype(BF16)
    sums = jnp.concatenate([sums, sums], axis=1)
    lane = lax.broadcasted_iota(jnp.int32, (1, B_K), 1)
    head_masks = [(lane // B_DK == hh).astype(F32) for hh in range(B_HEADS)]
    key_row = lax.broadcasted_iota(jnp.int32, (c, 1), 0)
    return sums, tri, head_masks, key_row


def _gla_decays(la, consts, reverse):
    sums = consts[0]
    c = GLA_CHUNK
    cr = _dot(sums, jnp.concatenate(_split_bf16(la), axis=0))
    cum = cr[0:c]
    ref = cr[c:2 * c]
    edge = 0 if reverse else c - 1
    return cum, ref, cum[edge:edge + 1]


def _gla_operands(q, k, v, cum, ref, total, consts, reverse):
    _, _, head_masks, key_row = consts
    c, sub = GLA_CHUNK, GLA_SUB

    def stack_heads(x):
        return jnp.concatenate([x * hm for hm in head_masks], axis=0)

    q_in = stack_heads(q * jnp.exp(cum)).astype(BF16)
    q_loc = stack_heads(q * jnp.exp(cum - ref)).astype(BF16)
    k_out = stack_heads(k * jnp.exp(total - cum)).astype(BF16)
    k_sub = []
    for i in range(c // sub):
        ref_i = ref[i * sub:i * sub + 1]
        valid = (key_row >= i * sub) if reverse else (key_row < (i + 1) * sub)
        k_sub.append((k * jnp.exp(jnp.where(valid, ref_i - cum, -jnp.inf))).astype(BF16))
    v_st = jnp.concatenate([v[:, hh * B_DV:(hh + 1) * B_DV] for hh in range(B_HEADS)], axis=0).astype(BF16)
    return q_in, q_loc, k_out, k_sub, v_st


def _gla_scores(q_loc, k_sub):
    c, sub = GLA_CHUNK, GLA_SUB
    pieces = []
    for i in range(c // sub):
        q_i = jnp.concatenate([q_loc[hh * c + i * sub:hh * c + (i + 1) * sub] for hh in range(B_HEADS)], axis=0)
        pieces.append(lax.dot_general(q_i, k_sub[i], NT_DIMS, preferred_element_type=F32))
    return pieces


def _gla_local(pieces, k_out, v_st, consts):
    causal = consts[1]
    c, sub = GLA_CHUNK, GLA_SUB
    o_heads = []
    for hh in range(B_HEADS):
        s_h = jnp.concatenate([p[hh * sub:(hh + 1) * sub] for p in pieces], axis=0)
        s_h = jnp.where(causal, s_h, 0.0).astype(BF16)
        o_heads.append(_dot(s_h, v_st[hh * c:(hh + 1) * c]))
    o_local = jnp.concatenate(o_heads, axis=0)
    st_inc = lax.dot_general(v_st, k_out, TN_DIMS, preferred_element_type=F32)
    return o_local, st_inc


def _gla_kernel(qf_ref, kf_ref, vf_ref, laf_ref, qr_ref, kr_ref, vr_ref, lar_ref,
                of_ref, or_ref, stf_ref, str_ref):
    @pl.when(pl.program_id(1) == 0)
    def _():
        stf_ref[...] = jnp.zeros_like(stf_ref)
        str_ref[...] = jnp.zeros_like(str_ref)

    c = GLA_CHUNK
    nb = qf_ref.shape[0]
    n_chunks = qf_ref.shape[1] // c
    dirs = ((False, qf_ref, kf_ref, vf_ref, laf_ref, of_ref, stf_ref),
            (True, qr_ref, kr_ref, vr_ref, lar_ref, or_ref, str_ref))
    consts = {rev: _gla_consts(rev) for rev in (False, True)}
    units = [(b, d, n) for n in range(n_chunks) for b in range(nb) for d in range(2)]
    rows = lambda n: slice(n * c, (n + 1) * c)

    decays = {}
    for b, d, n in units:
        rev, la_ref = dirs[d][0], dirs[d][4]
        decays[b, d, n] = _gla_decays(la_ref[b, rows(n), :], consts[rev], rev)
    operands = {}
    for b, d, n in units:
        rev, q_ref, k_ref, v_ref = dirs[d][:4]
        operands[b, d, n] = _gla_operands(q_ref[b, rows(n), :], k_ref[b, rows(n), :], v_ref[b, rows(n), :],
                                          *decays[b, d, n], consts[rev], rev)
    scores = {}
    for u in units:
        scores[u] = _gla_scores(operands[u][1], operands[u][3])
    local = {}
    for u in units:
        local[u] = _gla_local(scores[u], operands[u][2], operands[u][4], consts[dirs[u[1]][0]])

    for b in range(nb):
        for d in range(2):
            rev, o_ref, st_ref = dirs[d][0], dirs[d][5], dirs[d][6]
            st = st_ref[b]
            for n in (range(n_chunks - 1, -1, -1) if rev else range(n_chunks)):
                o_local, st_inc = local[b, d, n]
                q_in = operands[b, d, n][0]
                o = o_local + lax.dot_general(q_in, st.astype(BF16), NT_DIMS, preferred_element_type=F32)
                st = st * jnp.exp(decays[b, d, n][2]) + st_inc
                for hh in range(B_HEADS):
                    o_ref[b, rows(n), hh * B_DV:(hh + 1) * B_DV] = o[hh * c:(hh + 1) * c].astype(BF16)
            st_ref[b] = st


def gla_bidir(qb, kb, vb, laf, lab, n_ctx_tiles):
    bsz, s, _ = qb.shape
    t = ROW_TILE
    nt = s // t

    def rev_tile(j):
        return jnp.where(j < n_ctx_tiles, n_ctx_tiles - 1 - j, nt - 1 - (j - n_ctx_tiles))

    nb = _batch_block(bsz, GLA_BATCH)
    fwd = lambda w: pl.BlockSpec((nb, t, w), lambda b, j: (b, j, 0))
    rev = lambda w: pl.BlockSpec((nb, t, w), lambda b, j: (b, rev_tile(j), 0))
    return pl.pallas_call(
        _gla_kernel,
        out_shape=[jax.ShapeDtypeStruct((bsz, s, B_V), BF16)] * 2,
        grid=(bsz // nb, nt),
        in_specs=[fwd(B_K), fwd(B_K), fwd(B_V), fwd(B_K), rev(B_K), rev(B_K), rev(B_V), rev(B_K)],
        out_specs=[fwd(B_V), rev(B_V)],
        scratch_shapes=[pltpu.VMEM((nb, B_DV, B_K), F32)] * 2,
        compiler_params=_params("parallel", "arbitrary"),
        name="gla_bidir",
    )(qb, kb, vb, laf, qb, kb, vb, lab)


def _even_out_kernel(xc_ref, xl_ref, mod_ref, a_ref, of_ref, or_ref, g_ref, og_ref, wa_ref, wb_ref, o_ref, *,
                     n_ctx_tiles):
    nb, t, d = o_ref.shape
    rows = nb * t
    o = (of_ref[...].astype(F32) + or_ref[...].astype(F32)).reshape(rows, B_V)
    g = g_ref[...].astype(F32).reshape(rows, B_V)
    parts = []
    for hh in range(B_HEADS):
        oh = o[:, hh * B_DV:(hh + 1) * B_DV]
        ms = jnp.mean(oh * oh, axis=-1, keepdims=True)
        y = oh * lax.rsqrt(ms + NORM_EPS) * og_ref[...]
        parts.append((y * _silu(g[:, hh * B_DV:(hh + 1) * B_DV])).astype(BF16))
    gla = jnp.concatenate(parts, axis=1)
    y = _dot(a_ref[...].reshape(rows, a_ref.shape[-1]), wa_ref[...]) + _dot(gla, wb_ref[...])
    o_ref[...] = _stream_tile(xc_ref, xl_ref, n_ctx_tiles) + mod_ref[:, 2:3, :] * y.reshape(nb, t, d)


def even_out_proj(stream, mods, a, o_f, o_r, gb, o_gain, wa, wb, n_ctx_tiles):
    t = ROW_TILE
    nb = _batch_block(a.shape[0], EVEN_BATCH)
    (xc, xl), x_specs = _stream_specs(stream, n_ctx_tiles, t, nb)
    bsz, s, d = a.shape[0], a.shape[1], xc.shape[-1]
    row = lambda w: pl.BlockSpec((nb, t, w), lambda b, i: (b, i, 0))
    mods_spec = pl.BlockSpec((nb, None, N_MOD, d), lambda b, i: (b, (i >= n_ctx_tiles).astype(jnp.int32), 0, 0))
    return pl.pallas_call(
        functools.partial(_even_out_kernel, n_ctx_tiles=n_ctx_tiles),
        out_shape=jax.ShapeDtypeStruct((bsz, s, d), F32),
        grid=(bsz // nb, s // t),
        in_specs=x_specs + [mods_spec, row(a.shape[-1]), row(B_V), row(B_V), row(B_V),
                            _resident(o_gain.shape), _resident(wa.shape), _resident(wb.shape)],
        out_specs=row(d),
        compiler_params=_params("parallel", "parallel"),
        name="even_out_proj",
    )(xc, xl, mods, a, o_f, o_r, gb, o_gain, wa, wb)


def _ffn_kernel(x_ref, xp_ref, xn_ref, mod_ref, gain_ref, wu_ref, cw_ref, wd_ref, fg_ref, o_ref,
                h_ref, act_ref, *, tile0, n_ctx_tiles, n_tiles, final_norm):
    ti = pl.program_id(1) + tile0
    nb, t, d = x_ref.shape
    d_ff = wd_ref.shape[0]
    gain = gain_ref[...]
    shift = mod_ref[:, 3:4, :]
    scale = mod_ref[:, 4:5, :]
    x = x_ref[...]
    has_prev = jnp.logical_and(ti != 0, ti != n_ctx_tiles)
    has_next = jnp.logical_and(ti != n_ctx_tiles - 1, ti != n_tiles - 1)
    hp = jnp.where(has_prev, _norm_mod(xp_ref[...], gain, shift, scale), 0.0)
    hn = jnp.where(has_next, _norm_mod(xn_ref[...], gain, shift, scale), 0.0)
    seg = t + 2 * FFN_HALO
    h_ref[...] = jnp.concatenate([hp, _norm_mod(x, gain, shift, scale), hn], axis=1).astype(BF16).reshape(nb * seg, d)

    def body(u):
        return jnp.concatenate([u[b * seg + FFN_HALO:b * seg + FFN_HALO + t] for b in range(nb)], axis=0)

    def conv(u, cw):
        prev = body(pltpu.roll(u, 1, 0))
        nxt = body(pltpu.roll(u, nb * seg - 1, 0))
        return cw[0:1] * prev + cw[1:2] * body(u) + cw[2:3] * nxt

    for c0 in range(0, d_ff, FFN_CN):
        gate_cols = slice(c0, c0 + FFN_CN)
        val_cols = slice(d_ff + c0, d_ff + c0 + FFN_CN)
        h = h_ref[...]
        xh = conv(_dot(h, wu_ref[:, gate_cols]), 0.5 * cw_ref[:, gate_cols])
        val = conv(_dot(h, wu_ref[:, val_cols]), cw_ref[:, val_cols])
        act_ref[:, gate_cols] = (xh * (jnp.tanh(xh) + 1.0) * val).astype(BF16)
    y = x + mod_ref[:, 5:6, :] * _dot(act_ref[...], wd_ref[...]).reshape(nb, t, d)
    if final_norm:
        ms = jnp.mean(y * y, axis=-1, keepdims=True)
        y = y * lax.rsqrt(ms + NORM_EPS) * fg_ref[...]
    o_ref[...] = y


def conv_ffn(xx, mods, layer, gain, wu, cw, wd, final_gain, n_ctx_tiles, latents_only, final_norm, t=ROW_TILE):
    bsz, s, d = xx.shape
    assert s % t == 0 and t % FFN_HALO == 0
    nb = _batch_block(bsz, max(FFN_ROWS // t, 1))
    nt = s // t
    tile0 = n_ctx_tiles if latents_only else 0
    hb = t // FFN_HALO
    last_hb = s // FFN_HALO - 1
    kernel = functools.partial(_ffn_kernel, tile0=tile0, n_ctx_tiles=n_ctx_tiles, n_tiles=nt, final_norm=final_norm)
    return pl.pallas_call(
        kernel,
        out_shape=jax.ShapeDtypeStruct((bsz, s - tile0 * t, d), F32),
        grid=(bsz // nb, nt - tile0),
        in_specs=[pl.BlockSpec((nb, t, d), lambda b, i: (b, i + tile0, 0)),
                  pl.BlockSpec((nb, FFN_HALO, d), lambda b, i: (b, jnp.maximum((i + tile0) * hb - 1, 0), 0)),
                  pl.BlockSpec((nb, FFN_HALO, d), lambda b, i: (b, jnp.minimum((i + tile0 + 1) * hb, last_hb), 0)),
                  pl.BlockSpec((nb, None, N_MOD, d),
                               lambda b, i: (b, (i + tile0 >= n_ctx_tiles).astype(jnp.int32), 0, 0)),
                  _resident_layer(gain.shape, layer), _resident_layer(wu.shape, layer),
                  _resident_layer(cw.shape, layer), _resident_layer(wd.shape, layer),
                  _resident(final_gain.shape)],
        out_specs=pl.BlockSpec((nb, t, d), lambda b, i: (b, i, 0)),
        scratch_shapes=[pltpu.VMEM((nb * (t + 2 * FFN_HALO), d), BF16),
                        pltpu.VMEM((nb * t, wd.shape[1]), BF16)],
        compiler_params=_params("parallel", "parallel"),
        name="conv_ffn_final" if final_norm else "conv_ffn",
    )(xx, xx, xx, mods, gain, wu, cw, wd, final_gain)


def _odd_mods_spec(bsz, n_ctx_blocks, d):
    return pl.BlockSpec((bsz, None, N_MOD, d), lambda j: (0, (j >= n_ctx_blocks).astype(jnp.int32), 0, 0))


def _odd_in_kernel(x_ref, mod_ref, gain_ref, wg_ref, wr_ref, gate_ref, rec_ref, slab_ref):
    bsz, tt, d = x_ref.shape
    w = wr_ref.shape[1]
    h = _norm_mod(x_ref[...], gain_ref[...], mod_ref[:, 0:1, :], mod_ref[:, 1:2, :])
    h = h.reshape(bsz * tt, d).astype(BF16)
    gate_ref[...] = jax.nn.gelu(_dot(h, wg_ref[...]), approximate=True).astype(BF16).reshape(bsz, tt, w)
    rec = _dot(h, wr_ref[...])
    for b in range(bsz):
        for sl in range(w // LANES):
            slab_ref[sl, b * TM_PITCH:b * TM_PITCH + tt, :] = rec[b * tt:(b + 1) * tt, sl * LANES:(sl + 1) * LANES]
    for t in range(tt):
        for sl in range(w // LANES):
            rec_ref[t, :, sl * LANES:(sl + 1) * LANES] = slab_ref[sl, pl.ds(t, bsz, stride=TM_PITCH), :]


def odd_in_proj(xx, mods, gain, wg, wr, n_ctx_rows):
    bsz, s, d = xx.shape
    tt = LRU_TT
    w = wg.shape[1]
    assert tt <= TM_PITCH and TM_PITCH % 8 == 0
    return pl.pallas_call(
        _odd_in_kernel,
        out_shape=[jax.ShapeDtypeStruct((bsz, s, w), BF16), jax.ShapeDtypeStruct((s, bsz, w), F32)],
        grid=(s // tt,),
        in_specs=[pl.BlockSpec((bsz, tt, d), lambda j: (0, j, 0)), _odd_mods_spec(bsz, n_ctx_rows // tt, d),
                  _resident(gain.shape), _resident(wg.shape), _resident(wr.shape)],
        out_specs=[pl.BlockSpec((bsz, tt, w), lambda j: (0, j, 0)),
                   pl.BlockSpec((tt, bsz, w), lambda j: (j, 0, 0))],
        scratch_shapes=[pltpu.VMEM((w // LANES, bsz * TM_PITCH, LANES), F32)],
        compiler_params=_params("parallel"),
        name="odd_in_proj",
    )(xx, mods, gain, wg, wr)


def _lru_kernel(uf_ref, ufp_ref, ufn_ref, ur_ref, urp_ref, urn_ref, cw_ref, wax_ref, bax_ref, lam_ref,
                hf_ref, hr_ref, ue_ref, a_ref, x_ref, hst_ref, *, n_ctx_blocks, n_blocks):
    j = pl.program_id(0)
    tt, bsz, w = uf_ref.shape

    @pl.when(j == 0)
    def _():
        hst_ref[...] = jnp.zeros_like(hst_ref)

    rev_blk = jnp.where(j < n_ctx_blocks, n_ctx_blocks - 1 - j, n_blocks - 1 - (j - n_ctx_blocks))
    for d, (blk, u_ref, up_ref, un_ref, o_ref) in enumerate(
            ((j, uf_ref, ufp_ref, ufn_ref, hf_ref), (rev_blk, ur_ref, urp_ref, urn_ref, hr_ref))):
        has_prev = jnp.logical_and(blk != 0, blk != n_ctx_blocks)
        has_next = jnp.logical_and(blk != n_ctx_blocks - 1, blk != n_blocks - 1)
        ue_ref[0:1] = jnp.where(has_prev, up_ref[...], 0.0)
        ue_ref[1:tt + 1] = u_ref[...]
        ue_ref[tt + 1:tt + 3] = jnp.where(has_next, un_ref[...], 0.0)
        cw = 0.5 * cw_ref[...]
        uc = cw[0:1] * ue_ref[0:tt] + cw[1:2] * ue_ref[1:tt + 1] + cw[2:3] * ue_ref[2:tt + 2] + cw[3:4] * ue_ref[3:tt + 3]
        uc = uc.reshape(tt * bsz, w)
        half_c = (-0.5 * LRU_C) * _softplus(-lam_ref[d])
        for hh in range(LRU_HEADS):
            cols = slice(hh * LRU_HEAD_DIM, (hh + 1) * LRU_HEAD_DIM)
            uh = uc[:, cols]
            th = jnp.tanh(_dot(uh.astype(BF16), wax_ref[d, hh]) + bax_ref[d, hh])
            log_a = th[:, 0:LRU_HEAD_DIM] * half_c[:, cols] + half_c[:, cols]
            a = jnp.exp(log_a)
            m2 = jnp.tanh(log_a) * (-1.0 - a * a)
            mult = m2 * lax.rsqrt(jnp.maximum(m2, jnp.finfo(F32).tiny))
            a_ref[:, :, cols] = a.reshape(tt, bsz, LRU_HEAD_DIM)
            x_ref[:, :, cols] = (mult * (th[:, LRU_HEAD_DIM:2 * LRU_HEAD_DIM] + 1.0) * uh).reshape(tt, bsz, LRU_HEAD_DIM)
        h = hst_ref[d]
        for step in range(tt):
            tcur = tt - 1 - step if d == 1 else step
            h = a_ref[tcur] * h + x_ref[tcur]
            o_ref[tcur] = h
        hst_ref[d] = h


def lru_scan(u, cw, wax, bax, lam, n_ctx_rows):
    s, bsz, w = u.shape
    tt = LRU_TT
    nb = s // tt
    ncb = n_ctx_rows // tt

    def rev_blk(j):
        return jnp.where(j < ncb, ncb - 1 - j, nb - 1 - (j - ncb))

    def specs(blk):
        return [pl.BlockSpec((tt, bsz, w), lambda j: (blk(j), 0, 0)),
                pl.BlockSpec((1, bsz, w), lambda j: (jnp.maximum(blk(j) * tt - 1, 0), 0, 0)),
                pl.BlockSpec((2, bsz, w), lambda j: (jnp.minimum((blk(j) + 1) * (tt // 2), s // 2 - 1), 0, 0))]

    fwd_blk = lambda j: j
    kernel = functools.partial(_lru_kernel, n_ctx_blocks=ncb, n_blocks=nb)
    return pl.pallas_call(
        kernel,
        out_shape=[jax.ShapeDtypeStruct((s, bsz, w), F32)] * 2,
        grid=(nb,),
        in_specs=specs(fwd_blk) + specs(rev_blk) + [_resident(cw.shape), _resident(wax.shape),
                                                     _resident(bax.shape), _resident(lam.shape)],
        out_specs=[pl.BlockSpec((tt, bsz, w), lambda j: (j, 0, 0)),
                   pl.BlockSpec((tt, bsz, w), lambda j: (rev_blk(j), 0, 0))],
        scratch_shapes=[pltpu.VMEM((tt + 3, bsz, w), F32), pltpu.VMEM((tt, bsz, w), F32),
                        pltpu.VMEM((tt, bsz, w), F32), pltpu.VMEM((2, bsz, w), F32)],
        compiler_params=_params("arbitrary"),
        name="lru_scan",
    )(u, u, u, u, u, u, cw, wax, bax, lam)


def _odd_out_kernel(x_ref, mod_ref, gate_ref, hf_ref, hr_ref, w_ref, o_ref, slab_ref):
    bsz, tt, d = x_ref.shape
    w = w_ref.shape[0]
    for t in range(tt):
        hs = hf_ref[t] + hr_ref[t]
        for sl in range(w // LANES):
            slab_ref[sl, pl.ds(t, bsz, stride=TM_PITCH), :] = hs[:, sl * LANES:(sl + 1) * LANES]
    rec = jnp.concatenate(
        [jnp.concatenate([slab_ref[sl, b * TM_PITCH:b * TM_PITCH + tt, :] for sl in range(w // LANES)], axis=1)
         for b in range(bsz)], axis=0)
    mixed = (gate_ref[...].reshape(bsz * tt, w) * rec).astype(BF16)
    y = _dot(mixed, w_ref[...]).reshape(bsz, tt, d)
    o_ref[...] = x_ref[...] + mod_ref[:, 2:3, :] * y


def odd_out_proj(xx, mods, gate, hf, hr, w_out, n_ctx_rows, latents_only):
    bsz, s, d = xx.shape
    tt = LRU_TT
    w = gate.shape[-1]
    j0 = n_ctx_rows // tt if latents_only else 0
    bmaj = lambda n: pl.BlockSpec((bsz, tt, n), lambda j: (0, j + j0, 0))
    tmaj = pl.BlockSpec((tt, bsz, w), lambda j: (j + j0, 0, 0))
    mods_spec = pl.BlockSpec((bsz, None, N_MOD, d),
                             lambda j: (0, (j + j0 >= n_ctx_rows // tt).astype(jnp.int32), 0, 0))
    return pl.pallas_call(
        _odd_out_kernel,
        out_shape=jax.ShapeDtypeStruct((bsz, s - j0 * tt, d), F32),
        grid=(s // tt - j0,),
        in_specs=[bmaj(d), mods_spec, bmaj(w), tmaj, tmaj, _resident(w_out.shape)],
        out_specs=pl.BlockSpec((bsz, tt, d), lambda j: (0, j, 0)),
        scratch_shapes=[pltpu.VMEM((w // LANES, bsz * TM_PITCH, LANES), F32)],
        compiler_params=_params("parallel"),
        name="odd_out_proj",
    )(xx, mods, gate, hf, hr, w_out)


def _rope_pair_slab(a, b):
    n = A_HEAD_DIM // 4
    part = lambda t, i: t[..., i * n:(i + 1) * n]
    return jnp.concatenate([part(a, 0), part(a, 2), part(b, 0), part(b, 2),
                            part(a, 1), part(a, 3), part(b, 1), part(b, 3)], axis=-1)


def _rope_tables(n_ctx, n_lat):
    rows = n_lat // GRID_W
    row = jnp.repeat(jnp.arange(rows, dtype=F32), GRID_W)
    col = jnp.tile(jnp.arange(GRID_W, dtype=F32), rows)
    n_freq = A_HEAD_DIM // 4
    inv_freq = ROPE_THETA ** (-jnp.arange(n_freq, dtype=F32) / n_freq)
    ar = row[:, None] * inv_freq
    ac = col[:, None] * inv_freq
    cos = jnp.concatenate([jnp.cos(ar), jnp.cos(ar), jnp.cos(ac), jnp.cos(ac)], axis=-1)
    sin = jnp.concatenate([-jnp.sin(ar), jnp.sin(ar), -jnp.sin(ac), jnp.sin(ac)], axis=-1)
    cos = jnp.concatenate([jnp.ones((n_ctx, A_HEAD_DIM), F32), cos], axis=0)
    sin = jnp.concatenate([jnp.zeros((n_ctx, A_HEAD_DIM), F32), sin], axis=0)
    return _rope_pair_slab(cos, cos), _rope_pair_slab(sin, sin)


def kernel(x, c, ctx, c_ctx, ada_w, ada_b, norm_mix, norm_ffn, ffn_w_up, ffn_conv, ffn_w_down, even_w_in, even_w_out, attn_q_gain, attn_k_gain, gla_gate_w_up, gla_gate_b, gla_out_gain, lru_w_in, lru_conv, lru_lambda, lru_w_a, lru_b_a, lru_w_x, lru_b_x, lru_w_out, final_gain):
    bsz, n_lat, d = x.shape
    n_ctx = ctx.shape[1]
    depth = ada_w.shape[0]
    d_ff = ffn_w_down.shape[1]
    assert n_ctx % ROW_TILE == 0 and n_lat % ROW_TILE == 0 and d_ff % FFN_CN == 0
    n_ctx_tiles = n_ctx // ROW_TILE
    s = n_ctx + n_lat

    ffn_weights = (norm_ffn.reshape(depth, 1, d), ffn_w_up.astype(BF16), ffn_conv, ffn_w_down.astype(BF16),
                   final_gain.reshape(1, d))

    xx = (ctx, x)
    cond_rows = -(-(bsz + 1) // 8) * 8
    cond = jnp.zeros((cond_rows, d), F32).at[:bsz].set(c).at[bsz].set(c_ctx)
    cos, sin = _rope_tables(n_ctx, n_lat)
    pad_gain = lambda g: _rope_pair_slab(g.reshape(1, A_HEAD_DIM), g.reshape(1, A_HEAD_DIM))

    for l in range(depth):
        last = l == depth - 1
        j = l // 2
        table = adaln_table(cond, ada_w, ada_b.reshape(depth, 1, N_MOD * d), l)
        m_lat = table[:bsz].reshape(bsz, N_MOD, d)
        m_ctx = jnp.broadcast_to(table[bsz].reshape(1, N_MOD, d), (bsz, N_MOD, d))
        mods = jnp.stack([m_ctx, m_lat], axis=1)
        gain_mix = norm_mix[l].reshape(1, d)

        if l % 2 == 0:
            w_in = even_w_in[j]
            o0 = A_HEADS * A_HEAD_DIM
            o1 = o0 + A_KV_HEADS * A_HEAD_DIM
            o2 = o1 + A_KV_HEADS * A_HEAD_DIM
            o3 = o2 + 2 * B_K + 2 * B_V
            head = lambda hh: w_in[:, hh * A_HEAD_DIM:(hh + 1) * A_HEAD_DIM]
            wqkv = jnp.concatenate(
                [_rope_pair_slab(head(m), head(m + A_HEADS // 2)) for m in range(A_HEADS // 2)]
                + [_rope_pair_slab(head(A_HEADS), head(A_HEADS + 1)), w_in[:, o1:o2]], axis=1).astype(BF16)
            wb = w_in[:, o2:o3].astype(BF16)
            wlr = w_in[:, o3:].astype(BF16)
            zeros = jnp.zeros((B_GATE_RANK, B_K), F32)
            wgate = jnp.concatenate([jnp.concatenate([gla_gate_w_up[j, 0], zeros], axis=1),
                                     jnp.concatenate([zeros, gla_gate_w_up[j, 1]], axis=1)], axis=0)
            bgate = gla_gate_b[j].reshape(1, 2 * B_K)
            s_bound = A_HEAD_DIM ** 0.5 * jnp.max(jnp.abs(attn_q_gain[j])) * jnp.max(jnp.abs(attn_k_gain[j]))
            qaug = jnp.zeros((2, LANES), F32)
            qaug = qaug.at[0, ATT_AUG_LANES[0]].set(-s_bound * LOG2E).at[1, ATT_AUG_LANES[1]].set(-s_bound * LOG2E)
            q, k, v, qb, kb, vb, gb, laf, lab = even_in_proj(
                xx, mods, gain_mix, wqkv, wb, wlr, wgate, bgate,
                pad_gain(attn_q_gain[j]), pad_gain(attn_k_gain[j]), qaug, cos, sin, n_ctx_tiles)
            a = attention(s_bound.reshape(1), q, k, v, n_ctx)
            o_f, o_r = gla_bidir(qb, kb, vb, laf, lab, n_ctx_tiles)
            w_out = even_w_out[j]
            xx = even_out_proj(xx, mods, a, o_f, o_r, gb, gla_out_gain[j].reshape(1, B_DV),
                               w_out[:o0].astype(BF16), w_out[o0:].astype(BF16), n_ctx_tiles)
        else:
            w_in = lru_w_in[j]
            gate, rec = odd_in_proj(xx, mods, gain_mix, w_in[:, :LRU_WIDTH].astype(BF16),
                                    w_in[:, LRU_WIDTH:].astype(BF16), n_ctx)
            wax = jnp.concatenate([lru_w_a[j], lru_w_x[j]], axis=-1).astype(BF16)
            bax = 0.5 * jnp.concatenate([lru_b_a[j].reshape(2, LRU_HEADS, 1, LRU_HEAD_DIM),
                                         lru_b_x[j].reshape(2, LRU_HEADS, 1, LRU_HEAD_DIM)], axis=-1)
            hf, hr = lru_scan(rec, lru_conv[j], wax, bax, lru_lambda[j].reshape(2, 1, LRU_WIDTH), n_ctx)
            xx = odd_out_proj(xx, mods, gate, hf, hr, lru_w_out[j].astype(BF16), n_ctx, latents_only=last)

        if last and xx.shape[1] == n_lat:
            t = FFN_LATENT_TILE if n_lat % FFN_LATENT_TILE == 0 else ROW_TILE
            xx = conv_ffn(xx, mods, l, *ffn_weights, 0, latents_only=False, final_norm=True, t=t)
        else:
            xx = conv_ffn(xx, mods, l, *ffn_weights, n_ctx_tiles, latents_only=last, final_norm=last)
    return xx
```

```python
import functools

import numpy as np
import jax
import jax.numpy as jnp
from jax import lax
from jax.experimental import pallas as pl
from jax.experimental.pallas import tpu as pltpu

F32 = jnp.float32
BF16 = jnp.bfloat16

NORM_EPS = 1e-6
N_MOD = 6
GRID_W = 64
ROPE_THETA = 10000.0

A_HEADS = 8
A_KV_HEADS = 2
A_GROUP = A_HEADS // A_KV_HEADS
A_HEAD_DIM = 64

B_HEADS = 4
B_DK = 64
B_DV = 128
B_K = B_HEADS * B_DK
B_V = B_HEADS * B_DV
B_GATE_RANK = 16
B_GATE_TAU = 16.0
GLA_CHUNK = 64
GLA_SUB = 16
GLA_BATCH = 4

LRU_HEADS = 10
LRU_HEAD_DIM = 128
LRU_WIDTH = LRU_HEADS * LRU_HEAD_DIM
LRU_C = 8.0

LANES = 128
ROW_TILE = 256
EVEN_BATCH = 4
ATT_TQ = 256
ATT_TK = 256
FFN_CN = 256
FFN_HALO = 8
FFN_LATENT_TILE = 512
FFN_ROWS = 512
LRU_TT = 64
ODD_TT = 128
TM_PITCH = 136
VMEM_LIMIT = 56 * 1024 * 1024

LOG2E = float(np.log2(np.e))
ATT_MAX_STATIC_SHIFT = 40.0
ATT_AUG_LANES = (32, 0)

NT_DIMS = (((1,), (1,)), ((), ()))
TN_DIMS = (((0,), (0,)), ((), ()))


def _params(*sem):
    return pltpu.CompilerParams(dimension_semantics=sem, vmem_limit_bytes=VMEM_LIMIT)


def _batch_block(bsz, want):
    return max(n for n in range(1, want + 1) if bsz % n == 0)


def _resident(shape):
    nd = len(shape)
    return pl.BlockSpec(shape, lambda *_: (0,) * nd, pipeline_mode=pl.Buffered(1))


def _resident_layer(shape, layer):
    nd = len(shape)
    return pl.BlockSpec((None,) + tuple(shape[1:]), lambda *_: (layer,) + (0,) * (nd - 1),
                        pipeline_mode=pl.Buffered(1))


def _dot(a, b):
    return jnp.dot(a, b, preferred_element_type=F32)


def _split_bf16(a):
    hi = a.astype(BF16)
    lo = (a - hi.astype(F32)).astype(BF16)
    return hi, lo


def _dot_f32(a, b):
    ah, al = _split_bf16(a)
    bh, bl = _split_bf16(b)
    return _dot(ah, bh) + _dot(ah, bl) + _dot(al, bh)


def _sigmoid(x):
    return 0.5 * jnp.tanh(0.5 * x) + 0.5


def _silu(x):
    return x * _sigmoid(x)


def _softplus(x):
    return jnp.maximum(x, 0.0) + jnp.log1p(jnp.exp(-jnp.abs(x)))


def _norm_mod(x, gain, shift, scale):
    ms = jnp.mean(x * x, axis=-1, keepdims=True)
    return (x * lax.rsqrt(ms + NORM_EPS) * gain) * (1.0 + scale) + shift


def _adaln_kernel(c_ref, w_ref, b_ref, o_ref):
    o_ref[...] = _dot_f32(_silu(c_ref[...]), w_ref[...]) + b_ref[...]


def adaln_table(cond, w, b, layer):
    rows, d = cond.shape
    n = w.shape[2]
    tn = 768
    return pl.pallas_call(
        _adaln_kernel,
        out_shape=jax.ShapeDtypeStruct((rows, n), F32),
        grid=(n // tn,),
        in_specs=[pl.BlockSpec((rows, d), lambda j: (0, 0)),
                  pl.BlockSpec((None, d, tn), lambda j: (layer, 0, j)),
                  pl.BlockSpec((None, 1, tn), lambda j: (layer, 0, j))],
        out_specs=pl.BlockSpec((rows, tn), lambda j: (0, j)),
        compiler_params=_params("arbitrary"),
        name="adaln_table",
    )(cond, w, b)


def _stream_specs(stream, n_ctx_tiles, t, nb=None):
    if isinstance(stream, tuple):
        ctx_arr, lat_arr = stream
        lat_off = 0
    else:
        ctx_arr = lat_arr = stream
        lat_off = n_ctx_tiles
    d = ctx_arr.shape[-1]
    ctx_spec = pl.BlockSpec((nb, t, d), lambda b, i: (b, jnp.minimum(i, n_ctx_tiles - 1), 0))
    lat_spec = pl.BlockSpec((nb, t, d), lambda b, i: (b, jnp.maximum(i - n_ctx_tiles, 0) + lat_off, 0))
    return (ctx_arr, lat_arr), [ctx_spec, lat_spec]


def _stream_tile(xc_ref, xl_ref, n_ctx_tiles):
    return jnp.where(pl.program_id(1) < n_ctx_tiles, xc_ref[...], xl_ref[...])


def _even_in_kernel(xc_ref, xl_ref, mod_ref, gain_ref, wqkv_ref, wb_ref, wlr_ref, wgate_ref, bgate_ref,
                    qg_ref, kg_ref, qaug_ref, cos_ref, sin_ref,
                    q_ref, k_ref, v_ref, qb_ref, kb_ref, vb_ref, gb_ref, laf_ref, lab_ref, *, n_ctx_tiles):
    nb, t, d = xl_ref.shape
    x = _stream_tile(xc_ref, xl_ref, n_ctx_tiles)
    h = _norm_mod(x, gain_ref[...], mod_ref[:, 0:1, :], mod_ref[:, 1:2, :]).reshape(nb * t, d).astype(BF16)
    cos = jnp.concatenate([cos_ref[...]] * nb, axis=0)
    sin = jnp.concatenate([sin_ref[...]] * nb, axis=0)
    lane = lax.broadcasted_iota(jnp.int32, cos.shape, 1)

    def put_head(ref, hh, val):
        for b in range(nb):
            ref[b, hh] = val[b * t:(b + 1) * t]

    def put_rows(ref, val):
        ref[...] = val.reshape(nb, t, val.shape[-1])
    first = lane % (LANES // 2) < LANES // 4
    low = lane < A_HEAD_DIM

    def pair_norm_rope(xp, g):
        sq = xp * xp
        both = jnp.sum(sq, axis=-1, keepdims=True)
        ms_first = jnp.sum(jnp.where(first, sq, 0.0), axis=-1, keepdims=True)
        inv = jnp.where(first, lax.rsqrt(ms_first * (1.0 / A_HEAD_DIM) + NORM_EPS),
                        lax.rsqrt((both - ms_first) * (1.0 / A_HEAD_DIM) + NORM_EPS))
        y = xp * inv * g
        return y * cos + pltpu.roll(y, LANES // 2, 1) * sin

    qkv = _dot(h, wqkv_ref[...])
    slab = lambda i: qkv[:, i * LANES:(i + 1) * LANES]
    n_pairs = A_HEADS // 2
    for m in range(n_pairs):
        qp = pair_norm_rope(slab(m), qg_ref[...]) * (A_HEAD_DIM ** -0.5 * LOG2E)
        put_head(q_ref, m, (jnp.where(first, qp, 0.0) + qaug_ref[0:1, :]).astype(BF16))
        put_head(q_ref, m + n_pairs, (jnp.where(first, 0.0, qp) + qaug_ref[1:2, :]).astype(BF16))
    kp = pair_norm_rope(slab(n_pairs), kg_ref[...])
    put_head(k_ref, 0, (jnp.where(first, kp, 0.0) + (lane == ATT_AUG_LANES[0]).astype(F32)).astype(BF16))
    put_head(k_ref, 1, (jnp.where(first, 0.0, kp) + (lane == ATT_AUG_LANES[1]).astype(F32)).astype(BF16))
    vp = slab(n_pairs + 1)
    put_head(v_ref, 0, jnp.where(low, vp, 1.0).astype(BF16))
    put_head(v_ref, 1, jnp.where(low, pltpu.roll(vp, A_HEAD_DIM, 1), 1.0).astype(BF16))

    pb = _dot(h, wb_ref[...])
    put_rows(qb_ref, pb[:, 0:B_K] * B_DK ** -0.5)
    put_rows(kb_ref, pb[:, B_K:2 * B_K])
    put_rows(vb_ref, pb[:, 2 * B_K:2 * B_K + B_V])
    put_rows(gb_ref, pb[:, 2 * B_K + B_V:2 * B_K + 2 * B_V].astype(BF16))

    lr = _dot(h, wlr_ref[...])
    z = _dot_f32(lr, wgate_ref[...]) + bgate_ref[...]
    la = -_softplus(-z) * (1.0 / B_GATE_TAU)
    put_rows(laf_ref, la[:, 0:B_K])
    put_rows(lab_ref, la[:, B_K:2 * B_K])


def even_in_proj(stream, mods, gain, wqkv, wb, wlr, wgate, bgate, qg, kg, qaug, cos, sin, n_ctx_tiles):
    t = ROW_TILE
    nb = _batch_block(stream[0].shape[0] if isinstance(stream, tuple) else stream.shape[0], EVEN_BATCH)
    (xc, xl), x_specs = _stream_specs(stream, n_ctx_tiles, t, nb)
    bsz, d = xc.shape[0], xc.shape[-1]
    s = cos.shape[0]
    row = lambda w: pl.BlockSpec((nb, t, w), lambda b, i: (b, i, 0))
    heads = lambda n: pl.BlockSpec((nb, n, t, LANES), lambda b, i: (b, 0, i, 0))
    tab = pl.BlockSpec((t, LANES), lambda b, i: (i, 0))
    mods_spec = pl.BlockSpec((nb, None, N_MOD, d), lambda b, i: (b, (i >= n_ctx_tiles).astype(jnp.int32), 0, 0))
    out_shape = [
        jax.ShapeDtypeStruct((bsz, A_HEADS, s, LANES), BF16),
        jax.ShapeDtypeStruct((bsz, A_KV_HEADS, s, LANES), BF16),
        jax.ShapeDtypeStruct((bsz, A_KV_HEADS, s, LANES), BF16),
        jax.ShapeDtypeStruct((bsz, s, B_K), F32),
        jax.ShapeDtypeStruct((bsz, s, B_K), F32),
        jax.ShapeDtypeStruct((bsz, s, B_V), F32),
        jax.ShapeDtypeStruct((bsz, s, B_V), BF16),
        jax.ShapeDtypeStruct((bsz, s, B_K), F32),
        jax.ShapeDtypeStruct((bsz, s, B_K), F32),
    ]
    return pl.pallas_call(
        functools.partial(_even_in_kernel, n_ctx_tiles=n_ctx_tiles),
        out_shape=out_shape,
        grid=(bsz // nb, s // t),
        in_specs=x_specs + [mods_spec, _resident(gain.shape),
                  _resident(wqkv.shape), _resident(wb.shape), _resident(wlr.shape),
                  _resident(wgate.shape), _resident(bgate.shape), _resident(qg.shape), _resident(kg.shape),
                  _resident(qaug.shape), tab, tab],
        out_specs=[heads(A_HEADS), heads(A_KV_HEADS), heads(A_KV_HEADS),
                   row(B_K), row(B_K), row(B_V), row(B_V), row(B_K), row(B_K)],
        compiler_params=_params("parallel", "parallel"),
        name="even_in_proj",
    )(xc, xl, mods, gain, wqkv, wb, wlr, wgate, bgate, qg, kg, qaug, cos, sin)


def _attn_kernel(shift_ref, q_ref, k_ref, v_ref, o_ref, s_ref, p_ref, m_ref, *, n_ctx_q_tiles, n_ctx_rows):
    i = pl.program_id(1)
    tq = q_ref.shape[1]
    rows = A_GROUP * tq
    n_keys = k_ref.shape[1]
    group_q = lambda g: q_ref[g * A_GROUP:(g + 1) * A_GROUP].reshape(rows, LANES)

    def finish(g, acc):
        out = acc / acc[:, A_HEAD_DIM:A_HEAD_DIM + 1]
        low = lax.broadcasted_iota(jnp.int32, (tq, LANES), 1) < A_HEAD_DIM
        for j in range(0, A_GROUP, 2):
            even = out[j * tq:(j + 1) * tq]
            odd = pltpu.roll(out[(j + 1) * tq:(j + 2) * tq], A_HEAD_DIM, 1)
            col = (g * A_GROUP + j) // 2
            o_ref[:, col * LANES:(col + 1) * LANES] = jnp.where(low, even, odd).astype(BF16)

    def attend_static_shift(nk):
        for g in range(A_KV_HEADS):
            qs = group_q(g)
            acc = jnp.zeros((rows, LANES), F32)
            for c0 in range(0, nk, ATT_TK):
                s = lax.dot_general(qs, k_ref[g, c0:c0 + ATT_TK, :], NT_DIMS, preferred_element_type=F32)
                acc = acc + _dot(jnp.exp2(s).astype(BF16), v_ref[g, c0:c0 + ATT_TK, :])
            finish(g, acc)

    def attend_row_max(nk):
        for g in range(A_KV_HEADS):
            qs = group_q(g)
            m_ref[...] = jnp.full(m_ref.shape, -jnp.inf, F32)
            for c0 in range(0, nk, ATT_TK):
                cols = slice(c0, c0 + ATT_TK)
                s = lax.dot_general(qs, k_ref[g, cols, :], NT_DIMS, preferred_element_type=F32)
                s_ref[:, cols] = s
                m_ref[...] = jnp.maximum(m_ref[...], jnp.maximum(s[:, 0:LANES], s[:, LANES:2 * LANES]))
            m = jnp.max(m_ref[...], axis=1, keepdims=True)
            for c0 in range(0, nk, ATT_TK):
                cols = slice(c0, c0 + ATT_TK)
                p_ref[:, cols] = jnp.exp2(s_ref[:, cols] - m).astype(BF16)
            finish(g, _dot(p_ref[:, 0:nk], v_ref[g, 0:nk, :]))

    is_ctx = i < n_ctx_q_tiles
    static_ok = shift_ref[0] <= ATT_MAX_STATIC_SHIFT
    for ctx_tile, nk in ((True, n_ctx_rows), (False, n_keys)):
        tile_match = is_ctx if ctx_tile else jnp.logical_not(is_ctx)

        @pl.when(jnp.logical_and(tile_match, static_ok))
        def _():
            attend_static_shift(nk)

        @pl.when(jnp.logical_and(tile_match, jnp.logical_not(static_ok)))
        def _():
            attend_row_max(nk)


def attention(shift, q, k, v, n_ctx_rows):
    bsz, _, s, _ = q.shape
    tq = ATT_TQ
    assert ATT_TK == 2 * LANES and n_ctx_rows % ATT_TK == 0 and s % ATT_TK == 0
    kernel = functools.partial(_attn_kernel, n_ctx_q_tiles=n_ctx_rows // tq, n_ctx_rows=n_ctx_rows)
    kv_spec = pl.BlockSpec((None, A_KV_HEADS, s, LANES), lambda b, i: (b, 0, 0, 0))
    return pl.pallas_call(
        kernel,
        out_shape=jax.ShapeDtypeStruct((bsz, s, A_HEADS * A_HEAD_DIM), BF16),
        grid=(bsz, s // tq),
        in_specs=[pl.BlockSpec(memory_space=pltpu.SMEM),
                  pl.BlockSpec((None, A_HEADS, tq, LANES), lambda b, i: (b, 0, i, 0)), kv_spec, kv_spec],
        out_specs=pl.BlockSpec((None, tq, A_HEADS * A_HEAD_DIM), lambda b, i: (b, i, 0)),
        scratch_shapes=[pltpu.VMEM((A_GROUP * tq, s), F32),
                        pltpu.VMEM((A_GROUP * tq, s), BF16),
                        pltpu.VMEM((A_GROUP * tq, LANES), F32)],
        compiler_params=_params("parallel", "arbitrary"),
        name="gqa_attention",
    )(shift, q, k, v)


def _gla_consts(reverse):
    c, sub = GLA_CHUNK, GLA_SUB
    t = lax.broadcasted_iota(jnp.int32, (c, c), 0)
    s = lax.broadcasted_iota(jnp.int32, (c, c), 1)
    if reverse:
        tri = s >= t
        blk = s >= (t // sub + 1) * sub
    else:
        tri = s <= t
        blk = s < (t // sub) * sub
    sums = jnp.concatenate([tri, blk], axis=0).astype(BF16)
    sums = jnp.concatenate([sums, sums], axis=1)
    lane = lax.broadcasted_iota(jnp.int32, (1, B_K), 1)
    head_masks = [(lane // B_DK == hh).astype(F32) for hh in range(B_HEADS)]
    key_row = lax.broadcasted_iota(jnp.int32, (c, 1), 0)
    return sums, tri, head_masks, key_row


def _gla_decays(la, consts, reverse):
    sums = consts[0]
    c = GLA_CHUNK
    cr = _dot(sums, jnp.concatenate(_split_bf16(la), axis=0))
    cum = cr[0:c]
    ref = cr[c:2 * c]
    edge = 0 if reverse else c - 1
    return cum, ref, cum[edge:edge + 1]


def _gla_operands(q, k, v, cum, ref, total, consts, reverse):
    _, _, head_masks, key_row = consts
    c, sub = GLA_CHUNK, GLA_SUB

    def stack_heads(x):
        return jnp.concatenate([x * hm for hm in head_masks], axis=0)

    q_in = stack_heads(q * jnp.exp(cum)).astype(BF16)
    q_loc = stack_heads(q * jnp.exp(cum - ref)).astype(BF16)
    k_out = stack_heads(k * jnp.exp(total - cum)).astype(BF16)
    k_sub = []
    for i in range(c // sub):
        ref_i = ref[i * sub:i * sub + 1]
        valid = (key_row >= i * sub) if reverse else (key_row < (i + 1) * sub)
        k_sub.append((k * jnp.exp(jnp.where(valid, ref_i - cum, -jnp.inf))).astype(BF16))
    v_st = jnp.concatenate([v[:, hh * B_DV:(hh + 1) * B_DV] for hh in range(B_HEADS)], axis=0).astype(BF16)
    return q_in, q_loc, k_out, k_sub, v_st


def _gla_scores(q_loc, k_sub):
    c, sub = GLA_CHUNK, GLA_SUB
    pieces = []
    for i in range(c // sub):
        q_i = jnp.concatenate([q_loc[hh * c + i * sub:hh * c + (i + 1) * sub] for hh in range(B_HEADS)], axis=0)
        pieces.append(lax.dot_general(q_i, k_sub[i], NT_DIMS, preferred_element_type=F32))
    return pieces


def _gla_local(pieces, k_out, v_st, consts):
    causal = consts[1]
    c, sub = GLA_CHUNK, GLA_SUB
    o_heads = []
    for hh in range(B_HEADS):
        s_h = jnp.concatenate([p[hh * sub:(hh + 1) * sub] for p in pieces], axis=0)
        s_h = jnp.where(causal, s_h, 0.0).astype(BF16)
        o_heads.append(_dot(s_h, v_st[hh * c:(hh + 1) * c]))
    o_local = jnp.concatenate(o_heads, axis=0)
    st_inc = lax.dot_general(v_st, k_out, TN_DIMS, preferred_element_type=F32)
    return o_local, st_inc


def _gla_kernel(qf_ref, kf_ref, vf_ref, laf_ref, qr_ref, kr_ref, vr_ref, lar_ref,
                of_ref, or_ref, stf_ref, str_ref):
    @pl.when(pl.program_id(1) == 0)
    def _():
        stf_ref[...] = jnp.zeros_like(stf_ref)
        str_ref[...] = jnp.zeros_like(str_ref)

    c = GLA_CHUNK
    nb = qf_ref.shape[0]
    n_chunks = qf_ref.shape[1] // c
    dirs = ((False, qf_ref, kf_ref, vf_ref, laf_ref, of_ref, stf_ref),
            (True, qr_ref, kr_ref, vr_ref, lar_ref, or_ref, str_ref))
    consts = {rev: _gla_consts(rev) for rev in (False, True)}
    units = [(b, d, n) for n in range(n_chunks) for b in range(nb) for d in range(2)]
    rows = lambda n: slice(n * c, (n + 1) * c)

    decays = {}
    for b, d, n in units:
        rev, la_ref = dirs[d][0], dirs[d][4]
        decays[b, d, n] = _gla_decays(la_ref[b, rows(n), :], consts[rev], rev)
    operands = {}
    for b, d, n in units:
        rev, q_ref, k_ref, v_ref = dirs[d][:4]
        operands[b, d, n] = _gla_operands(q_ref[b, rows(n), :], k_ref[b, rows(n), :], v_ref[b, rows(n), :],
                                          *decays[b, d, n], consts[rev], rev)
    scores = {}
    for u in units:
        scores[u] = _gla_scores(operands[u][1], operands[u][3])
    local = {}
    for u in units:
        local[u] = _gla_local(scores[u], operands[u][2], operands[u][4], consts[dirs[u[1]][0]])

    for b in range(nb):
        for d in range(2):
            rev, o_ref, st_ref = dirs[d][0], dirs[d][5], dirs[d][6]
            st = st_ref[b]
            for n in (range(n_chunks - 1, -1, -1) if rev else range(n_chunks)):
                o_local, st_inc = local[b, d, n]
                q_in = operands[b, d, n][0]
                o = o_local + lax.dot_general(q_in, st.astype(BF16), NT_DIMS, preferred_element_type=F32)
                st = st * jnp.exp(decays[b, d, n][2]) + st_inc
                for hh in range(B_HEADS):
                    o_ref[b, rows(n), hh * B_DV:(hh + 1) * B_DV] = o[hh * c:(hh + 1) * c].astype(BF16)
            st_ref[b] = st


def gla_bidir(qb, kb, vb, laf, lab, n_ctx_tiles):
    bsz, s, _ = qb.shape
    t = ROW_TILE
    nt = s // t

    def rev_tile(j):
        return jnp.where(j < n_ctx_tiles, n_ctx_tiles - 1 - j, nt - 1 - (j - n_ctx_tiles))

    nb = _batch_block(bsz, GLA_BATCH)
    fwd = lambda w: pl.BlockSpec((nb, t, w), lambda b, j: (b, j, 0))
    rev = lambda w: pl.BlockSpec((nb, t, w), lambda b, j: (b, rev_tile(j), 0))
    return pl.pallas_call(
        _gla_kernel,
        out_shape=[jax.ShapeDtypeStruct((bsz, s, B_V), BF16)] * 2,
        grid=(bsz // nb, nt),
        in_specs=[fwd(B_K), fwd(B_K), fwd(B_V), fwd(B_K), rev(B_K), rev(B_K), rev(B_V), rev(B_K)],
        out_specs=[fwd(B_V), rev(B_V)],
        scratch_shapes=[pltpu.VMEM((nb, B_DV, B_K), F32)] * 2,
        compiler_params=_params("parallel", "arbitrary"),
        name="gla_bidir",
    )(qb, kb, vb, laf, qb, kb, vb, lab)


def _even_out_kernel(xc_ref, xl_ref, mod_ref, a_ref, of_ref, or_ref, g_ref, og_ref, wa_ref, wb_ref, o_ref, *,
                     n_ctx_tiles):
    nb, t, d = o_ref.shape
    rows = nb * t
    o = (of_ref[...].astype(F32) + or_ref[...].astype(F32)).reshape(rows, B_V)
    g = g_ref[...].astype(F32).reshape(rows, B_V)
    parts = []
    for hh in range(B_HEADS):
        oh = o[:, hh * B_DV:(hh + 1) * B_DV]
        ms = jnp.mean(oh * oh, axis=-1, keepdims=True)
        y = oh * lax.rsqrt(ms + NORM_EPS) * og_ref[...]
        parts.append((y * _silu(g[:, hh * B_DV:(hh + 1) * B_DV])).astype(BF16))
    gla = jnp.concatenate(parts, axis=1)
    y = _dot(a_ref[...].reshape(rows, a_ref.shape[-1]), wa_ref[...]) + _dot(gla, wb_ref[...])
    o_ref[...] = _stream_tile(xc_ref, xl_ref, n_ctx_tiles) + mod_ref[:, 2:3, :] * y.reshape(nb, t, d)


def even_out_proj(stream, mods, a, o_f, o_r, gb, o_gain, wa, wb, n_ctx_tiles):
    t = ROW_TILE
    nb = _batch_block(a.shape[0], EVEN_BATCH)
    (xc, xl), x_specs = _stream_specs(stream, n_ctx_tiles, t, nb)
    bsz, s, d = a.shape[0], a.shape[1], xc.shape[-1]
    row = lambda w: pl.BlockSpec((nb, t, w), lambda b, i: (b, i, 0))
    mods_spec = pl.BlockSpec((nb, None, N_MOD, d), lambda b, i: (b, (i >= n_ctx_tiles).astype(jnp.int32), 0, 0))
    return pl.pallas_call(
        functools.partial(_even_out_kernel, n_ctx_tiles=n_ctx_tiles),
        out_shape=jax.ShapeDtypeStruct((bsz, s, d), F32),
        grid=(bsz // nb, s // t),
        in_specs=x_specs + [mods_spec, row(a.shape[-1]), row(B_V), row(B_V), row(B_V),
                            _resident(o_gain.shape), _resident(wa.shape), _resident(wb.shape)],
        out_specs=row(d),
        compiler_params=_params("parallel", "parallel"),
        name="even_out_proj",
    )(xc, xl, mods, a, o_f, o_r, gb, o_gain, wa, wb)


def _ffn_kernel(x_ref, xp_ref, xn_ref, mod_ref, gain_ref, wu_ref, cw_ref, wd_ref, fg_ref, o_ref,
                h_ref, act_ref, *, tile0, n_ctx_tiles, n_tiles, final_norm):
    ti = pl.program_id(1) + tile0
    nb, t, d = x_ref.shape
    d_ff = wd_ref.shape[0]
    gain = gain_ref[...]
    shift = mod_ref[:, 3:4, :]
    scale = mod_ref[:, 4:5, :]
    x = x_ref[...]
    has_prev = jnp.logical_and(ti != 0, ti != n_ctx_tiles)
    has_next = jnp.logical_and(ti != n_ctx_tiles - 1, ti != n_tiles - 1)
    hp = jnp.where(has_prev, _norm_mod(xp_ref[...], gain, shift, scale), 0.0)
    hn = jnp.where(has_next, _norm_mod(xn_ref[...], gain, shift, scale), 0.0)
    seg = t + 2 * FFN_HALO
    h_ref[...] = jnp.concatenate([hp, _norm_mod(x, gain, shift, scale), hn], axis=1).astype(BF16).reshape(nb * seg, d)

    def body(u):
        return jnp.concatenate([u[b * seg + FFN_HALO:b * seg + FFN_HALO + t] for b in range(nb)], axis=0)

    def conv(u, cw):
        prev = body(pltpu.roll(u, 1, 0))
        nxt = body(pltpu.roll(u, nb * seg - 1, 0))
        return cw[0:1] * prev + cw[1:2] * body(u) + cw[2:3] * nxt

    for c0 in range(0, d_ff, FFN_CN):
        gate_cols = slice(c0, c0 + FFN_CN)
        val_cols = slice(d_ff + c0, d_ff + c0 + FFN_CN)
        h = h_ref[...]
        xh = conv(_dot(h, wu_ref[:, gate_cols]), 0.5 * cw_ref[:, gate_cols])
        val = conv(_dot(h, wu_ref[:, val_cols]), cw_ref[:, val_cols])
        act_ref[:, gate_cols] = (xh * (jnp.tanh(xh) + 1.0) * val).astype(BF16)
    y = x + mod_ref[:, 5:6, :] * _dot(act_ref[...], wd_ref[...]).reshape(nb, t, d)
    if final_norm:
        ms = jnp.mean(y * y, axis=-1, keepdims=True)
        y = y * lax.rsqrt(ms + NORM_EPS) * fg_ref[...]
    o_ref[...] = y


def conv_ffn(xx, mods, layer, gain, wu, cw, wd, final_gain, n_ctx_tiles, latents_only, final_norm, t=ROW_TILE):
    bsz, s, d = xx.shape
    assert s % t == 0 and t % FFN_HALO == 0
    nb = _batch_block(bsz, max(FFN_ROWS // t, 1))
    nt = s // t
    tile0 = n_ctx_tiles if latents_only else 0
    hb = t // FFN_HALO
    last_hb = s // FFN_HALO - 1
    kernel = functools.partial(_ffn_kernel, tile0=tile0, n_ctx_tiles=n_ctx_tiles, n_tiles=nt, final_norm=final_norm)
    return pl.pallas_call(
        kernel,
        out_shape=jax.ShapeDtypeStruct((bsz, s - tile0 * t, d), F32),
        grid=(bsz // nb, nt - tile0),
        in_specs=[pl.BlockSpec((nb, t, d), lambda b, i: (b, i + tile0, 0)),
                  pl.BlockSpec((nb, FFN_HALO, d), lambda b, i: (b, jnp.maximum((i + tile0) * hb - 1, 0), 0)),
                  pl.BlockSpec((nb, FFN_HALO, d), lambda b, i: (b, jnp.minimum((i + tile0 + 1) * hb, last_hb), 0)),
                  pl.BlockSpec((nb, None, N_MOD, d),
                               lambda b, i: (b, (i + tile0 >= n_ctx_tiles).astype(jnp.int32), 0, 0)),
                  _resident_layer(gain.shape, layer), _resident_layer(wu.shape, layer),
                  _resident_layer(cw.shape, layer), _resident_layer(wd.shape, layer),
                  _resident(final_gain.shape)],
        out_specs=pl.BlockSpec((nb, t, d), lambda b, i: (b, i, 0)),
        scratch_shapes=[pltpu.VMEM((nb * (t + 2 * FFN_HALO), d), BF16),
                        pltpu.VMEM((nb * t, wd.shape[1]), BF16)],
        compiler_params=_params("parallel", "parallel"),
        name="conv_ffn_final" if final_norm else "conv_ffn",
    )(xx, xx, xx, mods, gain, wu, cw, wd, final_gain)


def _odd_mods_spec(bsz, n_ctx_blocks, d):
    return pl.BlockSpec((bsz, None, N_MOD, d), lambda j: (0, (j >= n_ctx_blocks).astype(jnp.int32), 0, 0))


def _odd_in_kernel(x_ref, mod_ref, gain_ref, wg_ref, wr_ref, gate_ref, rec_ref, slab_ref):
    bsz, tt, d = x_ref.shape
    w = wr_ref.shape[1]
    h = _norm_mod(x_ref[...], gain_ref[...], mod_ref[:, 0:1, :], mod_ref[:, 1:2, :])
    h = h.reshape(bsz * tt, d).astype(BF16)
    gate_ref[...] = jax.nn.gelu(_dot(h, wg_ref[...]), approximate=True).astype(BF16).reshape(bsz, tt, w)
    rec = _dot(h, wr_ref[...])
    for b in range(bsz):
        for sl in range(w // LANES):
            slab_ref[sl, b * TM_PITCH:b * TM_PITCH + tt, :] = rec[b * tt:(b + 1) * tt, sl * LANES:(sl + 1) * LANES]
    for t in range(tt):
        for sl in range(w // LANES):
            rec_ref[t, :, sl * LANES:(sl + 1) * LANES] = slab_ref[sl, pl.ds(t, bsz, stride=TM_PITCH), :]


def odd_in_proj(xx, mods, gain, wg, wr, n_ctx_rows):
    bsz, s, d = xx.shape
    tt = ODD_TT
    w = wg.shape[1]
    assert tt <= TM_PITCH and TM_PITCH % 8 == 0
    return pl.pallas_call(
        _odd_in_kernel,
        out_shape=[jax.ShapeDtypeStruct((bsz, s, w), BF16), jax.ShapeDtypeStruct((s, bsz, w), F32)],
        grid=(s // tt,),
        in_specs=[pl.BlockSpec((bsz, tt, d), lambda j: (0, j, 0)), _odd_mods_spec(bsz, n_ctx_rows // tt, d),
                  _resident(gain.shape), _resident(wg.shape), _resident(wr.shape)],
        out_specs=[pl.BlockSpec((bsz, tt, w), lambda j: (0, j, 0)),
                   pl.BlockSpec((tt, bsz, w), lambda j: (j, 0, 0))],
        scratch_shapes=[pltpu.VMEM((w // LANES, bsz * TM_PITCH, LANES), F32)],
        compiler_params=_params("parallel"),
        name="odd_in_proj",
    )(xx, mods, gain, wg, wr)


def _lru_kernel(uf_ref, ufp_ref, ufn_ref, ur_ref, urp_ref, urn_ref, cw_ref, wax_ref, bax_ref, lam_ref,
                hf_ref, hr_ref, ue_ref, a_ref, x_ref, hst_ref, *, n_ctx_blocks, n_blocks):
    j = pl.program_id(0)
    tt, bsz, w = uf_ref.shape

    @pl.when(j == 0)
    def _():
        hst_ref[...] = jnp.zeros_like(hst_ref)

    rev_blk = jnp.where(j < n_ctx_blocks, n_ctx_blocks - 1 - j, n_blocks - 1 - (j - n_ctx_blocks))
    for d, (blk, u_ref, up_ref, un_ref, o_ref) in enumerate(
            ((j, uf_ref, ufp_ref, ufn_ref, hf_ref), (rev_blk, ur_ref, urp_ref, urn_ref, hr_ref))):
        has_prev = jnp.logical_and(blk != 0, blk != n_ctx_blocks)
        has_next = jnp.logical_and(blk != n_ctx_blocks - 1, blk != n_blocks - 1)
        ue_ref[0:1] = jnp.where(has_prev, up_ref[...], 0.0)
        ue_ref[1:tt + 1] = u_ref[...]
        ue_ref[tt + 1:tt + 3] = jnp.where(has_next, un_ref[...], 0.0)
        cw = 0.5 * cw_ref[...]
        uc = cw[0:1] * ue_ref[0:tt] + cw[1:2] * ue_ref[1:tt + 1] + cw[2:3] * ue_ref[2:tt + 2] + cw[3:4] * ue_ref[3:tt + 3]
        uc = uc.reshape(tt * bsz, w)
        half_c = (-0.5 * LRU_C) * _softplus(-lam_ref[d])
        for hh in range(LRU_HEADS):
            cols = slice(hh * LRU_HEAD_DIM, (hh + 1) * LRU_HEAD_DIM)
            uh = uc[:, cols]
            th = jnp.tanh(_dot(uh.astype(BF16), wax_ref[d, hh]) + bax_ref[d, hh])
            log_a = th[:, 0:LRU_HEAD_DIM] * half_c[:, cols] + half_c[:, cols]
            a = jnp.exp(log_a)
            m2 = jnp.tanh(log_a) * (-1.0 - a * a)
            mult = m2 * lax.rsqrt(jnp.maximum(m2, jnp.finfo(F32).tiny))
            a_ref[:, :, cols] = a.reshape(tt, bsz, LRU_HEAD_DIM)
            x_ref[:, :, cols] = (mult * (th[:, LRU_HEAD_DIM:2 * LRU_HEAD_DIM] + 1.0) * uh).reshape(tt, bsz, LRU_HEAD_DIM)
        h = hst_ref[d]
        for step in range(tt):
            tcur = tt - 1 - step if d == 1 else step
            h = a_ref[tcur] * h + x_ref[tcur]
            o_ref[tcur] = h
        hst_ref[d] = h


def lru_scan(u, cw, wax, bax, lam, n_ctx_rows):
    s, bsz, w = u.shape
    tt = LRU_TT
    nb = s // tt
    ncb = n_ctx_rows // tt

    def rev_blk(j):
        return jnp.where(j < ncb, ncb - 1 - j, nb - 1 - (j - ncb))

    def specs(blk):
        return [pl.BlockSpec((tt, bsz, w), lambda j: (blk(j), 0, 0)),
                pl.BlockSpec((1, bsz, w), lambda j: (jnp.maximum(blk(j) * tt - 1, 0), 0, 0)),
                pl.BlockSpec((2, bsz, w), lambda j: (jnp.minimum((blk(j) + 1) * (tt // 2), s // 2 - 1), 0, 0))]

    fwd_blk = lambda j: j
    kernel = functools.partial(_lru_kernel, n_ctx_blocks=ncb, n_blocks=nb)
    return pl.pallas_call(
        kernel,
        out_shape=[jax.ShapeDtypeStruct((s, bsz, w), F32)] * 2,
        grid=(nb,),
        in_specs=specs(fwd_blk) + specs(rev_blk) + [_resident(cw.shape), _resident(wax.shape),
                                                     _resident(bax.shape), _resident(lam.shape)],
        out_specs=[pl.BlockSpec((tt, bsz, w), lambda j: (j, 0, 0)),
                   pl.BlockSpec((tt, bsz, w), lambda j: (rev_blk(j), 0, 0))],
        scratch_shapes=[pltpu.VMEM((tt + 3, bsz, w), F32), pltpu.VMEM((tt, bsz, w), F32),
                        pltpu.VMEM((tt, bsz, w), F32), pltpu.VMEM((2, bsz, w), F32)],
        compiler_params=_params("arbitrary"),
        name="lru_scan",
    )(u, u, u, u, u, u, cw, wax, bax, lam)


def _odd_out_kernel(x_ref, mod_ref, gate_ref, hf_ref, hr_ref, w_ref, o_ref, slab_ref):
    bsz, tt, d = x_ref.shape
    w = w_ref.shape[0]
    for t in range(tt):
        hs = hf_ref[t] + hr_ref[t]
        for sl in range(w // LANES):
            slab_ref[sl, pl.ds(t, bsz, stride=TM_PITCH), :] = hs[:, sl * LANES:(sl + 1) * LANES]
    rec = jnp.concatenate(
        [jnp.concatenate([slab_ref[sl, b * TM_PITCH:b * TM_PITCH + tt, :] for sl in range(w // LANES)], axis=1)
         for b in range(bsz)], axis=0)
    mixed = (gate_ref[...].reshape(bsz * tt, w) * rec).astype(BF16)
    y = _dot(mixed, w_ref[...]).reshape(bsz, tt, d)
    o_ref[...] = x_ref[...] + mod_ref[:, 2:3, :] * y


def odd_out_proj(xx, mods, gate, hf, hr, w_out, n_ctx_rows, latents_only):
    bsz, s, d = xx.shape
    tt = ODD_TT
    w = gate.shape[-1]
    j0 = n_ctx_rows // tt if latents_only else 0
    bmaj = lambda n: pl.BlockSpec((bsz, tt, n), lambda j: (0, j + j0, 0))
    tmaj = pl.BlockSpec((tt, bsz, w), lambda j: (j + j0, 0, 0))
    mods_spec = pl.BlockSpec((bsz, None, N_MOD, d),
                             lambda j: (0, (j + j0 >= n_ctx_rows // tt).astype(jnp.int32), 0, 0))
    return pl.pallas_call(
        _odd_out_kernel,
        out_shape=jax.ShapeDtypeStruct((bsz, s - j0 * tt, d), F32),
        grid=(s // tt - j0,),
        in_specs=[bmaj(d), mods_spec, bmaj(w), tmaj, tmaj, _resident(w_out.shape)],
        out_specs=pl.BlockSpec((bsz, tt, d), lambda j: (0, j, 0)),
        scratch_shapes=[pltpu.VMEM((w // LANES, bsz * TM_PITCH, LANES), F32)],
        compiler_params=_params("parallel"),
        name="odd_out_proj",
    )(xx, mods, gate, hf, hr, w_out)


def _rope_pair_slab(a, b):
    n = A_HEAD_DIM // 4
    part = lambda t, i: t[..., i * n:(i + 1) * n]
    return jnp.concatenate([part(a, 0), part(a, 2), part(b, 0), part(b, 2),
                            part(a, 1), part(a, 3), part(b, 1), part(b, 3)], axis=-1)


def _rope_tables(n_ctx, n_lat):
    rows = n_lat // GRID_W
    row = jnp.repeat(jnp.arange(rows, dtype=F32), GRID_W)
    col = jnp.tile(jnp.arange(GRID_W, dtype=F32), rows)
    n_freq = A_HEAD_DIM // 4
    inv_freq = ROPE_THETA ** (-jnp.arange(n_freq, dtype=F32) / n_freq)
    ar = row[:, None] * inv_freq
    ac = col[:, None] * inv_freq
    cos = jnp.concatenate([jnp.cos(ar), jnp.cos(ar), jnp.cos(ac), jnp.cos(ac)], axis=-1)
    sin = jnp.concatenate([-jnp.sin(ar), jnp.sin(ar), -jnp.sin(ac), jnp.sin(ac)], axis=-1)
    cos = jnp.concatenate([jnp.ones((n_ctx, A_HEAD_DIM), F32), cos], axis=0)
    sin = jnp.concatenate([jnp.zeros((n_ctx, A_HEAD_DIM), F32), sin], axis=0)
    return _rope_pair_slab(cos, cos), _rope_pair_slab(sin, sin)


def kernel(x, c, ctx, c_ctx, ada_w, ada_b, norm_mix, norm_ffn, ffn_w_up, ffn_conv, ffn_w_down, even_w_in, even_w_out, attn_q_gain, attn_k_gain, gla_gate_w_up, gla_gate_b, gla_out_gain, lru_w_in, lru_conv, lru_lambda, lru_w_a, lru_b_a, lru_w_x, lru_b_x, lru_w_out, final_gain):
    bsz, n_lat, d = x.shape
    n_ctx = ctx.shape[1]
    depth = ada_w.shape[0]
    d_ff = ffn_w_down.shape[1]
    assert n_ctx % ROW_TILE == 0 and n_lat % ROW_TILE == 0 and d_ff % FFN_CN == 0
    n_ctx_tiles = n_ctx // ROW_TILE
    s = n_ctx + n_lat

    ffn_weights = (norm_ffn.reshape(depth, 1, d), ffn_w_up.astype(BF16), ffn_conv, ffn_w_down.astype(BF16),
                   final_gain.reshape(1, d))

    xx = (ctx, x)
    cond_rows = -(-(bsz + 1) // 8) * 8
    cond = jnp.zeros((cond_rows, d), F32).at[:bsz].set(c).at[bsz].set(c_ctx)
    cos, sin = _rope_tables(n_ctx, n_lat)
    pad_gain = lambda g: _rope_pair_slab(g.reshape(1, A_HEAD_DIM), g.reshape(1, A_HEAD_DIM))

    for l in range(depth):
        last = l == depth - 1
        j = l // 2
        table = adaln_table(cond, ada_w, ada_b.reshape(depth, 1, N_MOD * d), l)
        m_lat = table[:bsz].reshape(bsz, N_MOD, d)
        m_ctx = jnp.broadcast_to(table[bsz].reshape(1, N_MOD, d), (bsz, N_MOD, d))
        mods = jnp.stack([m_ctx, m_lat], axis=1)
        gain_mix = norm_mix[l].reshape(1, d)

        if l % 2 == 0:
            w_in = even_w_in[j]
            o0 = A_HEADS * A_HEAD_DIM
            o1 = o0 + A_KV_HEADS * A_HEAD_DIM
            o2 = o1 + A_KV_HEADS * A_HEAD_DIM
            o3 = o2 + 2 * B_K + 2 * B_V
            head = lambda hh: w_in[:, hh * A_HEAD_DIM:(hh + 1) * A_HEAD_DIM]
            wqkv = jnp.concatenate(
                [_rope_pair_slab(head(m), head(m + A_HEADS // 2)) for m in range(A_HEADS // 2)]
                + [_rope_pair_slab(head(A_HEADS), head(A_HEADS + 1)), w_in[:, o1:o2]], axis=1).astype(BF16)
            wb = w_in[:, o2:o3].astype(BF16)
            wlr = w_in[:, o3:].astype(BF16)
            zeros = jnp.zeros((B_GATE_RANK, B_K), F32)
            wgate = jnp.concatenate([jnp.concatenate([gla_gate_w_up[j, 0], zeros], axis=1),
                                     jnp.concatenate([zeros, gla_gate_w_up[j, 1]], axis=1)], axis=0)
            bgate = gla_gate_b[j].reshape(1, 2 * B_K)
            s_bound = A_HEAD_DIM ** 0.5 * jnp.max(jnp.abs(attn_q_gain[j])) * jnp.max(jnp.abs(attn_k_gain[j]))
            qaug = jnp.zeros((2, LANES), F32)
            qaug = qaug.at[0, ATT_AUG_LANES[0]].set(-s_bound * LOG2E).at[1, ATT_AUG_LANES[1]].set(-s_bound * LOG2E)
            q, k, v, qb, kb, vb, gb, laf, lab = even_in_proj(
                xx, mods, gain_mix, wqkv, wb, wlr, wgate, bgate,
                pad_gain(attn_q_gain[j]), pad_gain(attn_k_gain[j]), qaug, cos, sin, n_ctx_tiles)
            a = attention(s_bound.reshape(1), q, k, v, n_ctx)
            o_f, o_r = gla_bidir(qb, kb, vb, laf, lab, n_ctx_tiles)
            w_out = even_w_out[j]
            xx = even_out_proj(xx, mods, a, o_f, o_r, gb, gla_out_gain[j].reshape(1, B_DV),
                               w_out[:o0].astype(BF16), w_out[o0:].astype(BF16), n_ctx_tiles)
        else:
            w_in = lru_w_in[j]
            gate, rec = odd_in_proj(xx, mods, gain_mix, w_in[:, :LRU_WIDTH].astype(BF16),
                                    w_in[:, LRU_WIDTH:].astype(BF16), n_ctx)
            wax = jnp.concatenate([lru_w_a[j], lru_w_x[j]], axis=-1).astype(BF16)
            bax = 0.5 * jnp.concatenate([lru_b_a[j].reshape(2, LRU_HEADS, 1, LRU_HEAD_DIM),
                                         lru_b_x[j].reshape(2, LRU_HEADS, 1, LRU_HEAD_DIM)], axis=-1)
            hf, hr = lru_scan(rec, lru_conv[j], wax, bax, lru_lambda[j].reshape(2, 1, LRU_WIDTH), n_ctx)
            xx = odd_out_proj(xx, mods, gate, hf, hr, lru_w_out[j].astype(BF16), n_ctx, latents_only=last)

        if last and xx.shape[1] == n_lat:
            t = FFN_LATENT_TILE if n_lat % FFN_LATENT_TILE == 0 else ROW_TILE
            xx = conv_ffn(xx, mods, l, *ffn_weights, 0, latents_only=False, final_norm=True, t=t)
        else:
            xx = conv_ffn(xx, mods, l, *ffn_weights, n_ctx_tiles, latents_only=last, final_norm=last)
    return xx
```
